```python
import jax
import jax.numpy as jnp
from jax import lax
import numpy as np

D_MODEL = 1024
BATCH = 16
SEQ = 4096
DEPTH = 4

GRID_W = 64
CTX_LEN = 256
N_MIXERS = 3
CHUNK = 64
NORM_EPS = 1e-6
D_FF = 4 * D_MODEL
RET_HEADS = 4
RET_DK = D_MODEL // RET_HEADS
RET_DV = 2 * D_MODEL // RET_HEADS
ROPE_BASE = 10000.0
GLA_HEADS = 4
GLA_DK = D_MODEL // 2 // GLA_HEADS
GLA_DV = D_MODEL // GLA_HEADS
GLA_RANK = 16
GLA_TAU = 16.0
MLSTM_INNER = 2 * D_MODEL
MLSTM_HEADS = 4
MLSTM_DH = MLSTM_INNER // MLSTM_HEADS
MLSTM_CONV = 3
N_RET = (DEPTH + 2) // 3
N_GLA = (DEPTH + 1) // 3
N_MLSTM = DEPTH // 3

kernel_name = "hybrid_ret_gla_mlstm_prefix_dit"

F32 = jnp.float32


def _rmsnorm(x, g):
    xf = x.astype(F32)
    y = xf * lax.rsqrt(jnp.mean(xf * xf, axis=-1, keepdims=True) + NORM_EPS)
    return (y * g.astype(F32)).astype(x.dtype)


def _head_rmsnorm(y, g):
    y = y * lax.rsqrt(jnp.mean(y * y, axis=-1, keepdims=True) + NORM_EPS)
    return y * g.astype(F32)[:, None, :]


def _heads(u, n):
    b, t, _ = u.shape
    return u.reshape(b, t, n, -1).transpose(0, 2, 1, 3)


def _merge(u):
    b, n, t, d = u.shape
    return u.transpose(0, 2, 1, 3).reshape(b, t, n * d)


def _to_chunks(a):
    b, h, t = a.shape[:3]
    return jnp.moveaxis(a.reshape(b, h, t // CHUNK, CHUNK, *a.shape[3:]), 2, 0)


def _from_chunks(a):
    a = jnp.moveaxis(a, 0, 2)
    return a.reshape(a.shape[0], a.shape[1], -1, *a.shape[4:])


def _flip(a):
    return jnp.flip(a, axis=2)


def _rope_2d(u):
    t = u.shape[2]
    rows = t // GRID_W
    row = jnp.repeat(jnp.arange(rows, dtype=F32), GRID_W)
    col = jnp.tile(jnp.arange(GRID_W, dtype=F32), rows)
    n_f = u.shape[-1] // 4
    inv = ROPE_BASE ** (-jnp.arange(n_f, dtype=F32) / n_f)
    ang = jnp.concatenate([row[:, None] * inv, col[:, None] * inv], axis=-1)
    cos, sin = jnp.cos(ang), jnp.sin(ang)
    u1, u2 = u[..., 0::2], u[..., 1::2]
    return jnp.stack([u1 * cos - u2 * sin, u1 * sin + u2 * cos], axis=-1).reshape(u.shape)


def _dwconv(u, w, b):
    k = w.shape[0]
    y = lax.conv_general_dilated(u, w[:, None, :].astype(u.dtype), window_strides=(1,),
                                 padding=[(k // 2, k // 2)], dimension_numbers=('NWC', 'WIO', 'NWC'),
                                 feature_group_count=u.shape[-1])
    return y + b.astype(u.dtype)


def _gated_linear_scan(q, k, v, log_a, s0):
    L = CHUNK
    causal = jnp.tril(jnp.ones((L, L), bool))

    def step(s, inp):
        qi, ki, vi, ai = inp
        b = jnp.cumsum(ai, axis=-2)
        b_ref = b[..., L // 2:L // 2 + 1, :]
        b_end = b[..., -1:, :]
        att = jnp.einsum('bhld,bhsd->bhls', qi * jnp.exp(b - b_ref), ki * jnp.exp(b_ref - b))
        att = jnp.where(causal, att, 0.0)
        o = (jnp.einsum('bhls,bhsv->bhlv', att, vi)
             + jnp.einsum('bhld,bhdv->bhlv', qi * jnp.exp(b), s))
        s = (jnp.exp(b_end)[:, :, 0, :, None] * s
             + jnp.einsum('bhsd,bhsv->bhdv', ki * jnp.exp(b_end - b), vi))
        return s, o

    s, o = lax.scan(step, s0, tuple(_to_chunks(t) for t in (q, k, v, log_a)))
    return _from_chunks(o), s


def _mlstm_scan(q, k, v, log_i, log_f, state0):
    L = CHUNK
    causal = jnp.tril(jnp.ones((L, L), bool))

    def step(state, inp):
        C, n, m = state
        qi, ki, vi, ii, fi = inp
        b = jnp.cumsum(fi, axis=-1)
        d_in = jnp.where(causal, b[..., :, None] - b[..., None, :] + ii[..., None, :], -jnp.inf)
        g = b + m[..., None]
        m_t = jnp.maximum(g, jnp.max(d_in, axis=-1))
        w = jnp.exp(d_in - m_t[..., None])
        w_prev = jnp.exp(g - m_t)
        s = jnp.einsum('bhld,bhsd->bhls', qi, ki) * w
        num = (jnp.einsum('bhls,bhsv->bhlv', s, vi)
               + w_prev[..., None] * jnp.einsum('bhld,bhvd->bhlv', qi, C))
        den = jnp.sum(s, axis=-1) + w_prev * jnp.einsum('bhld,bhd->bhl', qi, n)
        h = num / jnp.maximum(jnp.abs(den), jnp.exp(-m_t))[..., None]
        b_end = b[..., -1]
        d_end = b_end[..., None] - b + ii
        m_new = jnp.maximum(b_end + m, jnp.max(d_end, axis=-1))
        w_end = jnp.exp(d_end - m_new[..., None])
        a_prev = jnp.exp(b_end + m - m_new)
        C = a_prev[..., None, None] * C + jnp.einsum('bhsv,bhsd->bhvd', vi * w_end[..., None], ki)
        n = a_prev[..., None] * n + jnp.einsum('bhs,bhsd->bhd', w_end, ki)
        return (C, n, m_new), h

    state, h = lax.scan(step, state0, tuple(_to_chunks(t) for t in (q, k, v, log_i, log_f)))
    return _from_chunks(h), state


def _bidirectional(scan, ctx_f, ctx_b, lat_f, lat_b, state0):
    yc_f, st_f = scan(*ctx_f, state0)
    yc_b, st_b = scan(*[_flip(t) for t in ctx_b], state0)
    yl_f, _ = scan(*lat_f, st_f)
    yl_b, _ = scan(*[_flip(t) for t in lat_b], st_b)
    return yl_f + _flip(yl_b), yc_f + _flip(yc_b)


def _retention(xn, cn, w_in, log_decay_f, log_decay_b, norm_g, w_out, need_ctx):
    qk_w, v_w = RET_HEADS * RET_DK, RET_HEADS * RET_DV

    def project(u, rotate):
        p = (u @ w_in).astype(F32)
        q, k, v, g = jnp.split(p, [qk_w, 2 * qk_w, 2 * qk_w + v_w], axis=-1)
        q, k, v = _heads(q, RET_HEADS), _heads(k, RET_HEADS) * RET_DK ** -0.5, _heads(v, RET_HEADS)
        if rotate:
            q, k = _rope_2d(q), _rope_2d(k)
        return q, k, v, g

    def decay(p, like):
        return jnp.broadcast_to(-jnp.exp(p.astype(F32))[None, :, None, None], like.shape[:3] + (1,))

    qc, kc, vc, gc = project(cn, False)
    ql, kl, vl, gl = project(xn, True)
    s0 = jnp.zeros(qc.shape[:2] + (RET_DK, RET_DV), F32)
    yl, yc = _bidirectional(_gated_linear_scan,
                            (qc, kc, vc, decay(log_decay_f, qc)), (qc, kc, vc, decay(log_decay_b, qc)),
                            (ql, kl, vl, decay(log_decay_f, ql)), (ql, kl, vl, decay(log_decay_b, ql)), s0)

    def readout(y, g, dtype):
        return (jax.nn.silu(g) * _merge(_head_rmsnorm(y, norm_g))).astype(dtype) @ w_out

    return readout(yl, gl, xn.dtype), (readout(yc, gc, cn.dtype) if need_ctx else None)


def _gla(xn, cn, w_in, wa1_f, wa2_f, ba_f, wa1_b, wa2_b, ba_b, norm_g, w_out, need_ctx):
    H = GLA_HEADS
    kw, vw = H * GLA_DK, H * GLA_DV

    def project(u):
        p = (u @ w_in).astype(F32)
        q, k, v, r = jnp.split(p, [kw, 2 * kw, 2 * kw + vw], axis=-1)
        return _heads(q, H) * GLA_DK ** -0.5, _heads(k, H), _heads(v, H), r

    def log_gate(u, w1, w2, b):
        z = ((u @ w1) @ w2).astype(F32) + b.astype(F32)
        return _heads(jax.nn.log_sigmoid(z) / GLA_TAU, H)

    qc, kc, vc, rc = project(cn)
    ql, kl, vl, rl = project(xn)
    s0 = jnp.zeros(qc.shape[:2] + (GLA_DK, GLA_DV), F32)
    yl, yc = _bidirectional(_gated_linear_scan,
                            (qc, kc, vc, log_gate(cn, wa1_f, wa2_f, ba_f)),
                            (qc, kc, vc, log_gate(cn, wa1_b, wa2_b, ba_b)),
                            (ql, kl, vl, log_gate(xn, wa1_f, wa2_f, ba_f)),
                            (ql, kl, vl, log_gate(xn, wa1_b, wa2_b, ba_b)), s0)

    def readout(y, r, dtype):
        return (jax.nn.silu(r) * _merge(_head_rmsnorm(y, norm_g))).astype(dtype) @ w_out

    return readout(yl, rl, xn.dtype), (readout(yc, rc, cn.dtype) if need_ctx else None)


def _mlstm(xn, cn, w_in, conv_w, conv_b, w_q, w_k, w_v, w_if_f, b_if_f, w_if_b, b_if_b, skip, w_out, need_ctx):
    H, DH = MLSTM_HEADS, MLSTM_DH

    def project(u):
        a, o_pre = jnp.split(u @ w_in, 2, axis=-1)
        a_c = jax.nn.silu(_dwconv(a, conv_w, conv_b))
        bsz, t = a.shape[:2]
        a_ch = a_c.reshape(bsz, t, H, DH).astype(F32)
        a_h = a.reshape(bsz, t, H, DH).astype(F32)
        q = jnp.einsum('bthd,hde->bhte', a_ch, w_q.astype(F32))
        k = jnp.einsum('bthd,hde->bhte', a_ch, w_k.astype(F32)) * DH ** -0.5
        v = jnp.einsum('bthd,hde->bhte', a_h, w_v.astype(F32))
        return q, k, v, a_c, o_pre

    def gates(q, k, v, w_if, b_if):
        w_if = w_if.astype(F32)
        pre = (jnp.einsum('bhtd,hdg->bgt', q, w_if[0]) + jnp.einsum('bhtd,hdg->bgt', k, w_if[1])
               + jnp.einsum('bhtd,hdg->bgt', v, w_if[2]) + b_if.astype(F32)[None, :, None])
        return pre[:, :H], jax.nn.log_sigmoid(pre[:, H:])

    qc, kc, vc, acc, oc = project(cn)
    ql, kl, vl, acl, ol = project(xn)
    bsz = qc.shape[0]
    state0 = (jnp.zeros((bsz, H, DH, DH), F32), jnp.zeros((bsz, H, DH), F32), jnp.zeros((bsz, H), F32))
    hl, hc = _bidirectional(_mlstm_scan,
                            (qc, kc, vc, *gates(qc, kc, vc, w_if_f, b_if_f)),
                            (qc, kc, vc, *gates(qc, kc, vc, w_if_b, b_if_b)),
                            (ql, kl, vl, *gates(ql, kl, vl, w_if_f, b_if_f)),
                            (ql, kl, vl, *gates(ql, kl, vl, w_if_b, b_if_b)), state0)

    def readout(h, a_c, o_pre, dtype):
        y = jax.nn.sigmoid(o_pre.astype(F32)) * _merge(h) + skip.astype(F32) * a_c.astype(F32)
        return y.astype(dtype) @ w_out

    return readout(hl, acl, ol, xn.dtype), (readout(hc, acc, oc, cn.dtype) if need_ctx else None)


def _sqrelu_mlp(h, w1, w2):
    return jnp.square(jax.nn.relu(h @ w1)) @ w2


def _modulate(h, g, shift, scale):
    return _rmsnorm(h, g) * (1 + scale) + shift


def setup_inputs(seed: int = 0) -> dict:
    key = jax.random.key(seed)
    ks = iter(jax.random.split(key, 64))
    D = D_MODEL

    def nrm(shape, std):
        return std * jax.random.normal(next(ks), shape, F32)

    def gains(shape):
        return 1.0 + nrm(shape, 0.02)

    ret_base = jnp.log(-jnp.log1p(-(2.0 ** (-5.0 - jnp.arange(RET_HEADS, dtype=F32)))))

    def mlstm_bias():
        ib = nrm((N_MLSTM, MLSTM_HEADS), 0.1)
        fb = jnp.linspace(3.0, 6.0, MLSTM_HEADS, dtype=F32)[None] + nrm((N_MLSTM, MLSTM_HEADS), 0.1)
        return jnp.concatenate([ib, fb], axis=-1)

    ret_in = 2 * RET_HEADS * RET_DK + 2 * RET_HEADS * RET_DV
    gla_in = 2 * GLA_HEADS * GLA_DK + 2 * GLA_HEADS * GLA_DV
    return {
        "x": nrm((BATCH, SEQ, D), 1.0),
        "c": nrm((BATCH, D), 1.0),
        "ctx": nrm((BATCH, CTX_LEN, D), 1.0),
        "c_ctx": nrm((D,), 1.0),
        "ada_w": nrm((DEPTH, D, 6 * D), 0.5 * D ** -0.5),
        "ada_b": nrm((DEPTH, 6 * D), 0.02),
        "norm1_g": gains((DEPTH, D)),
        "norm2_g": gains((DEPTH, D)),
        "ffn_w1": nrm((DEPTH, D, D_FF), D ** -0.5),
        "ffn_w2": nrm((DEPTH, D_FF, D), D_FF ** -0.5),
        "final_g": gains((D,)),
        "ret_w_in": nrm((N_RET, D, ret_in), D ** -0.5),
        "ret_log_decay_f": ret_base[None] + nrm((N_RET, RET_HEADS), 0.05),
        "ret_log_decay_b": ret_base[None] + nrm((N_RET, RET_HEADS), 0.05),
        "ret_norm_g": gains((N_RET, RET_HEADS, RET_DV)),
        "ret_w_out": nrm((N_RET, RET_HEADS * RET_DV, D), (RET_HEADS * RET_DV) ** -0.5),
        "gla_w_in": nrm((N_GLA, D, gla_in), D ** -0.5),
        "gla_wa1_f": nrm((N_GLA, D, GLA_RANK), D ** -0.5),
        "gla_wa2_f": nrm((N_GLA, GLA_RANK, GLA_HEADS * GLA_DK), GLA_RANK ** -0.5),
        "gla_ba_f": nrm((N_GLA, GLA_HEADS * GLA_DK), 0.02),
        "gla_wa1_b": nrm((N_GLA, D, GLA_RANK), D ** -0.5),
        "gla_wa2_b": nrm((N_GLA, GLA_RANK, GLA_HEADS * GLA_DK), GLA_RANK ** -0.5),
        "gla_ba_b": nrm((N_GLA, GLA_HEADS * GLA_DK), 0.02),
        "gla_norm_g": gains((N_GLA, GLA_HEADS, GLA_DV)),
        "gla_w_out": nrm((N_GLA, GLA_HEADS * GLA_DV, D), (GLA_HEADS * GLA_DV) ** -0.5),
        "ml_w_in": nrm((N_MLSTM, D, 2 * MLSTM_INNER), D ** -0.5),
        "ml_conv_w": nrm((N_MLSTM, MLSTM_CONV, MLSTM_INNER), MLSTM_CONV ** -0.5),
        "ml_conv_b": nrm((N_MLSTM, MLSTM_INNER), 0.02),
        "ml_w_q": nrm((N_MLSTM, MLSTM_HEADS, MLSTM_DH, MLSTM_DH), MLSTM_DH ** -0.5),
        "ml_w_k": nrm((N_MLSTM, MLSTM_HEADS, MLSTM_DH, MLSTM_DH), MLSTM_DH ** -0.5),
        "ml_w_v": nrm((N_MLSTM, MLSTM_HEADS, MLSTM_DH, MLSTM_DH), MLSTM_DH ** -0.5),
        "ml_w_if_f": nrm((N_MLSTM, 3, MLSTM_HEADS, MLSTM_DH, 2 * MLSTM_HEADS), (3 * MLSTM_INNER) ** -0.5),
        "ml_b_if_f": mlstm_bias(),
        "ml_w_if_b": nrm((N_MLSTM, 3, MLSTM_HEADS, MLSTM_DH, 2 * MLSTM_HEADS), (3 * MLSTM_INNER) ** -0.5),
        "ml_b_if_b": mlstm_bias(),
        "ml_skip": gains((N_MLSTM, MLSTM_INNER)),
        "ml_w_out": nrm((N_MLSTM, MLSTM_INNER, D), MLSTM_INNER ** -0.5),
    }


def reference(x, c, ctx, c_ctx, ada_w, ada_b, norm1_g, norm2_g, ffn_w1, ffn_w2, final_g,
              ret_w_in, ret_log_decay_f, ret_log_decay_b, ret_norm_g, ret_w_out,
              gla_w_in, gla_wa1_f, gla_wa2_f, gla_ba_f, gla_wa1_b, gla_wa2_b, gla_ba_b, gla_norm_g, gla_w_out,
              ml_w_in, ml_conv_w, ml_conv_b, ml_w_q, ml_w_k, ml_w_v, ml_w_if_f, ml_b_if_f, ml_w_if_b, ml_b_if_b,
              ml_skip, ml_w_out):
    h_lat, h_ctx = x, ctx
    s_lat = jax.nn.silu(c)
    s_ctx = jax.nn.silu(c_ctx)
    for i in range(DEPTH):
        need_ctx = i < DEPTH - 1
        kind, j = i % N_MIXERS, i // N_MIXERS
        mod_l = (s_lat @ ada_w[i] + ada_b[i])[:, None, :]
        mod_c = s_ctx @ ada_w[i] + ada_b[i]
        sh1, sc1, g1, sh2, sc2, g2 = jnp.split(mod_l, 6, axis=-1)
        csh1, csc1, cg1, csh2, csc2, cg2 = jnp.split(mod_c, 6, axis=-1)

        xn = _modulate(h_lat, norm1_g[i], sh1, sc1)
        cn = _modulate(h_ctx, norm1_g[i], csh1, csc1)
        if kind == 0:
            y_lat, y_ctx = _retention(xn, cn, ret_w_in[j], ret_log_decay_f[j], ret_log_decay_b[j],
                                      ret_norm_g[j], ret_w_out[j], need_ctx)
        elif kind == 1:
            y_lat, y_ctx = _gla(xn, cn, gla_w_in[j], gla_wa1_f[j], gla_wa2_f[j], gla_ba_f[j],
                                gla_wa1_b[j], gla_wa2_b[j], gla_ba_b[j], gla_norm_g[j], gla_w_out[j], need_ctx)
        else:
            y_lat, y_ctx = _mlstm(xn, cn, ml_w_in[j], ml_conv_w[j], ml_conv_b[j], ml_w_q[j], ml_w_k[j],
                                  ml_w_v[j], ml_w_if_f[j], ml_b_if_f[j], ml_w_if_b[j], ml_b_if_b[j],
                                  ml_skip[j], ml_w_out[j], need_ctx)
        h_lat = h_lat + g1 * y_lat
        h_lat = h_lat + g2 * _sqrelu_mlp(_modulate(h_lat, norm2_g[i], sh2, sc2), ffn_w1[i], ffn_w2[i])
        if need_ctx:
            h_ctx = h_ctx + cg1 * y_ctx
            h_ctx = h_ctx + cg2 * _sqrelu_mlp(_modulate(h_ctx, norm2_g[i], csh2, csc2), ffn_w1[i], ffn_w2[i])
    return _rmsnorm(h_lat, final_g)
```

```python
import functools

import jax
import jax.numpy as jnp
from jax import lax
from jax.experimental import pallas as pl
from jax.experimental.pallas import tpu as pltpu

F32 = jnp.float32
BF16 = jnp.bfloat16

D_MODEL = 1024
DEPTH = 4
GRID_W = 64
CTX_LEN = 256
NORM_EPS = 1e-6
D_FF = 4 * D_MODEL
RET_HEADS = 4
RET_DK = D_MODEL // RET_HEADS
RET_DV = 2 * D_MODEL // RET_HEADS
ROPE_BASE = 10000.0
GLA_HEADS = 4
GLA_DK = D_MODEL // 2 // GLA_HEADS
GLA_DV = D_MODEL // GLA_HEADS
GLA_RANK = 16
GLA_TAU = 16.0
MLSTM_INNER = 2 * D_MODEL
MLSTM_HEADS = 4
MLSTM_DH = MLSTM_INNER // MLSTM_HEADS

TM = CTX_LEN
LANES = 128
BF16_ROWS = 16
RET_CHUNK = 256
GLA_CHUNK = 64
MLSTM_CHUNK = 256
VMEM_LIMIT = 56 * 1024 * 1024

_NT = (((1,), (1,)), ((), ()))
_TN = (((0,), (0,)), ((), ()))


def _dot(a, b):
    return jnp.dot(a, b, preferred_element_type=F32)


def _silu(x):
    return x * jax.nn.sigmoid(x)


def _log_sigmoid(z):
    return jnp.minimum(z, 0.0) - jnp.log1p(jnp.exp(-jnp.abs(z)))


def _modnorm(x, g, shift, scale):
    y = x * lax.rsqrt(jnp.mean(x * x, axis=-1, keepdims=True) + NORM_EPS) * g
    return y * (1.0 + scale) + shift


def _params(n_axes):
    return pltpu.CompilerParams(dimension_semantics=("arbitrary",) * n_axes,
                                vmem_limit_bytes=VMEM_LIMIT)


def _const_spec(shape):
    nd = len(shape)
    return pl.BlockSpec(shape, lambda *_: (0,) * nd, pipeline_mode=pl.Buffered(1))


def _row_spec(width, n_tiles, off):
    return pl.BlockSpec((TM, width), lambda b, t: (b * n_tiles + t + off, 0))


def _mod_spec(n_batch, off, has_ctx):
    if has_ctx:
        return pl.BlockSpec((1, 1, D_MODEL), lambda b, t: (jnp.where(t + off == 0, n_batch, b), 0, 0))
    return pl.BlockSpec((1, 1, D_MODEL), lambda b, t: (b, 0, 0))


def _ada_kernel(c_ref, w_ref, b_ref, o_ref):
    s = _silu(c_ref[...]).astype(BF16)
    o_ref[0] = _dot(s, w_ref[0].astype(BF16)) + b_ref[0]


def _ada(cvec, ada_w, ada_b):
    rows = cvec.shape[0]
    tn = 512
    return pl.pallas_call(
        _ada_kernel,
        grid=(DEPTH, 6 * D_MODEL // tn),
        in_specs=[pl.BlockSpec((rows, D_MODEL), lambda l, j: (0, 0)),
                  pl.BlockSpec((1, D_MODEL, tn), lambda l, j: (l, 0, j)),
                  pl.BlockSpec((1, 1, tn), lambda l, j: (l, 0, j))],
        out_specs=pl.BlockSpec((1, rows, tn), lambda l, j: (l, 0, j)),
        out_shape=jax.ShapeDtypeStruct((DEPTH, rows, 6 * D_MODEL), F32),
        compiler_params=_params(2),
        name="adaln",
    )(cvec, ada_w, ada_b.reshape(DEPTH, 1, 6 * D_MODEL))


def _ffn_kernel(x_ref, g_ref, sh_ref, sc_ref, gate_ref, fg_ref, w1_ref, w2_ref, o_ref, *, final_norm):
    x = x_ref[...]
    xn = _modnorm(x, g_ref[...], sh_ref[0], sc_ref[0]).astype(BF16)
    fc = 1024
    acc = jnp.zeros((TM, D_MODEL), F32)
    for j in range(D_FF // fc):
        hid = _dot(xn, w1_ref[:, j * fc:(j + 1) * fc])
        hid = jnp.square(jnp.maximum(hid, 0.0)).astype(BF16)
        acc = acc + _dot(hid, w2_ref[j * fc:(j + 1) * fc, :])
    o = x + gate_ref[0] * acc
    if final_norm:
        o = o * lax.rsqrt(jnp.mean(o * o, axis=-1, keepdims=True) + NORM_EPS) * fg_ref[...]
    o_ref[...] = o


def _ffn(h, n_batch, n_t, has_ctx, gain, shift, scale, gate, final_g, w1, w2, *, final_norm):
    ms = _mod_spec(n_batch, 0, has_ctx)
    return pl.pallas_call(
        functools.partial(_ffn_kernel, final_norm=final_norm),
        grid=(n_batch, n_t),
        in_specs=[_row_spec(D_MODEL, n_t, 0), _const_spec((1, D_MODEL)), ms, ms, ms,
                  _const_spec((1, D_MODEL)), _const_spec((D_MODEL, D_FF)), _const_spec((D_FF, D_MODEL))],
        out_specs=_row_spec(D_MODEL, n_t, 0),
        out_shape=jax.ShapeDtypeStruct((n_batch * n_t * TM, D_MODEL), F32),
        compiler_params=_params(2),
        name="ffn",
    )(h, gain, shift, scale, gate, final_g, w1, w2)


def _ret_proj_kernel(x_ref, g_ref, sh_ref, sc_ref, cos_ref, sin_ref, w_ref, q_ref, k_ref, v_ref, gg_ref):
    xn = _modnorm(x_ref[...], g_ref[...], sh_ref[0], sc_ref[0]).astype(BF16)
    cos, sin = cos_ref[...], sin_ref[...]
    half = RET_DK // 2
    qk_w = RET_HEADS * RET_DK

    def rope(u):
        u1, u2 = u[:, :half], u[:, half:]
        return jnp.concatenate([u1 * cos - u2 * sin, u1 * sin + u2 * cos], axis=-1)

    for hd in range(RET_HEADS):
        lo = hd * RET_DK
        q_ref[:, lo:lo + RET_DK] = rope(_dot(xn, w_ref[:, lo:lo + RET_DK])).astype(BF16)
        u = _dot(xn, w_ref[:, qk_w + lo:qk_w + lo + RET_DK]) * RET_DK ** -0.5
        k_ref[:, lo:lo + RET_DK] = rope(u).astype(BF16)
    for hd in range(RET_HEADS):
        lo = hd * RET_DV
        v_ref[:, lo:lo + RET_DV] = _dot(xn, w_ref[:, 2 * qk_w + lo:2 * qk_w + lo + RET_DV]).astype(BF16)
        base = 2 * qk_w + RET_HEADS * RET_DV
        gg_ref[:, lo:lo + RET_DV] = _dot(xn, w_ref[:, base + lo:base + lo + RET_DV]).astype(BF16)


def _ret_proj(h, n_batch, n_t, gain, shift, scale, cos, sin, w_in):
    ms = _mod_spec(n_batch, 0, True)
    rows = h.shape[0]
    qk_w, v_w = RET_HEADS * RET_DK, RET_HEADS * RET_DV
    tab = pl.BlockSpec((TM, RET_DK // 2), lambda b, t: (t, 0))
    return pl.pallas_call(
        _ret_proj_kernel,
        grid=(n_batch, n_t),
        in_specs=[_row_spec(D_MODEL, n_t, 0), _const_spec((1, D_MODEL)), ms, ms, tab, tab,
                  _const_spec(w_in.shape)],
        out_specs=[_row_spec(qk_w, n_t, 0), _row_spec(qk_w, n_t, 0), _row_spec(v_w, n_t, 0), _row_spec(v_w, n_t, 0)],
        out_shape=[jax.ShapeDtypeStruct((rows, qk_w), BF16), jax.ShapeDtypeStruct((rows, qk_w), BF16),
                   jax.ShapeDtypeStruct((rows, v_w), BF16), jax.ShapeDtypeStruct((rows, v_w), BF16)],
        compiler_params=_params(2),
        name="ret_proj",
    )(h, gain, shift, scale, cos, sin, w_in)


def _chunk_of(step, direction, n_chunks, n_ctx_chunks):
    if direction == 0:
        return step
    return jnp.where(step < n_ctx_chunks, n_ctx_chunks - 1 - step, n_chunks - 1 + n_ctx_chunks - step)


def _ret_scan_kernel(ld_ref, q_ref, k_ref, v_ref, o_ref, s_ref, dm_ref, *, chunk, seq, ctx):
    L = chunk
    hd = pl.program_id(1)
    n_chunks, n_ctx = seq // L, ctx // L
    li = lax.broadcasted_iota(jnp.int32, (L, L), 0)
    si = lax.broadcasted_iota(jnp.int32, (L, L), 1)
    row = lax.broadcasted_iota(jnp.int32, (L, 1), 0).astype(F32)
    for direction in (0, 1):
        lg = -jnp.exp(jnp.full((1, 1), ld_ref[direction, hd], F32))
        if direction == 0:
            dist = li - si
            q_dec = jnp.exp((row + 1.0) * lg)
            k_dec = jnp.exp((L - 1.0 - row) * lg)
        else:
            dist = si - li
            q_dec = jnp.exp((L - row) * lg)
            k_dec = jnp.exp(row * lg)
        dm_ref[...] = jnp.where(dist >= 0, jnp.exp(jnp.maximum(dist, 0).astype(F32) * lg), 0.0)
        s_dec = jnp.exp(float(L) * lg)
        s_ref[...] = jnp.zeros_like(s_ref)

        def body(step, carry, direction=direction, q_dec=q_dec, k_dec=k_dec, s_dec=s_dec):
            r0 = pl.multiple_of(_chunk_of(step, direction, n_chunks, n_ctx) * L, L)
            qc = q_ref[pl.ds(r0, L), :]
            kc = k_ref[pl.ds(r0, L), :]
            vc = v_ref[pl.ds(r0, L), :]
            att = lax.dot_general(qc, kc, _NT, preferred_element_type=F32) * dm_ref[...]
            state = s_ref[...]
            out = _dot(att.astype(BF16), vc) + q_dec * _dot(qc, state.astype(BF16))
            kd = (kc.astype(F32) * k_dec).astype(BF16)
            s_ref[...] = s_dec * state + lax.dot_general(kd, vc, _TN, preferred_element_type=F32)
            if direction == 0:
                o_ref[pl.ds(r0, L), :] = out
            else:
                o_ref[pl.ds(r0, L), :] += out
            return carry

        lax.fori_loop(0, n_chunks, body, 0)


def _ret_scan(log_decay, q, k, v, n_batch, seq):
    return pl.pallas_call(
        functools.partial(_ret_scan_kernel, chunk=RET_CHUNK, seq=seq, ctx=CTX_LEN),
        grid=(n_batch, RET_HEADS),
        in_specs=[pl.BlockSpec(memory_space=pltpu.SMEM),
                  pl.BlockSpec((seq, RET_DK), lambda b, h: (b, h)),
                  pl.BlockSpec((seq, RET_DK), lambda b, h: (b, h)),
                  pl.BlockSpec((seq, RET_DV), lambda b, h: (b, h))],
        out_specs=pl.BlockSpec((seq, RET_DV), lambda b, h: (b, h)),
        out_shape=jax.ShapeDtypeStruct((n_batch * seq, RET_HEADS * RET_DV), F32),
        scratch_shapes=[pltpu.VMEM((RET_DK, RET_DV), F32), pltpu.VMEM((RET_CHUNK, RET_CHUNK), F32)],
        compiler_params=_params(2),
        name="ret_scan",
    )(log_decay, q, k, v)


def _readout_kernel(y_ref, g_ref, x_ref, ng_ref, gate_ref, w_ref, o_ref, *, heads, dv):
    acc = jnp.zeros((TM, D_MODEL), F32)
    for hd in range(heads):
        lo = hd * dv
        y = y_ref[:, lo:lo + dv]
        yn = y * lax.rsqrt(jnp.mean(y * y, axis=-1, keepdims=True) + NORM_EPS) * ng_ref[hd:hd + 1, :]
        z = (_silu(g_ref[:, lo:lo + dv].astype(F32)) * yn).astype(BF16)
        acc = acc + _dot(z, w_ref[lo:lo + dv, :])
    o_ref[...] = x_ref[...] + gate_ref[0] * acc


def _readout(y, g, h, n_batch, n_t, off, norm_g, gate, w_out, heads, dv):
    n_out = n_t - off
    return pl.pallas_call(
        functools.partial(_readout_kernel, heads=heads, dv=dv),
        grid=(n_batch, n_out),
        in_specs=[_row_spec(heads * dv, n_t, off), _row_spec(heads * dv, n_t, off), _row_spec(D_MODEL, n_t, off),
                  _const_spec((heads, dv)), _mod_spec(n_batch, off, off == 0), _const_spec((heads * dv, D_MODEL))],
        out_specs=_row_spec(D_MODEL, n_out, 0),
        out_shape=jax.ShapeDtypeStruct((n_batch * n_out * TM, D_MODEL), F32),
        compiler_params=_params(2),
        name="readout",
    )(y, g, h, norm_g, gate, w_out)


def _gla_proj_kernel(x_ref, g_ref, sh_ref, sc_ref, w_ref, wa1_ref, wa2_ref, ba_ref,
                     q_ref, k_ref, v_ref, r_ref, ga_ref):
    xn = _modnorm(x_ref[...], g_ref[...], sh_ref[0], sc_ref[0]).astype(BF16)
    kw, vw = GLA_HEADS * GLA_DK, GLA_HEADS * GLA_DV
    q_ref[...] = (_dot(xn, w_ref[:, :kw]) * GLA_DK ** -0.5).astype(BF16)
    k_ref[...] = _dot(xn, w_ref[:, kw:2 * kw]).astype(BF16)
    v_ref[...] = _dot(xn, w_ref[:, 2 * kw:2 * kw + vw]).astype(BF16)
    r_ref[...] = _dot(xn, w_ref[:, 2 * kw + vw:]).astype(BF16)
    low = _dot(xn, wa1_ref[...]).astype(BF16)
    z = _dot(low, wa2_ref[...]) + ba_ref[...]
    ga_ref[...] = _log_sigmoid(z) / GLA_TAU


def _gla_proj(h, n_batch, n_t, gain, shift, scale, w_in, wa1, wa2, ba):
    ms = _mod_spec(n_batch, 0, True)
    rows = h.shape[0]
    kw, vw = GLA_HEADS * GLA_DK, GLA_HEADS * GLA_DV
    return pl.pallas_call(
        _gla_proj_kernel,
        grid=(n_batch, n_t),
        in_specs=[_row_spec(D_MODEL, n_t, 0), _const_spec((1, D_MODEL)), ms, ms, _const_spec(w_in.shape),
                  _const_spec(wa1.shape), _const_spec(wa2.shape), _const_spec(ba.shape)],
        out_specs=[_row_spec(kw, n_t, 0), _row_spec(kw, n_t, 0), _row_spec(vw, n_t, 0), _row_spec(vw, n_t, 0),
                   _row_spec(2 * kw, n_t, 0)],
        out_shape=[jax.ShapeDtypeStruct((rows, kw), BF16), jax.ShapeDtypeStruct((rows, kw), BF16),
                   jax.ShapeDtypeStruct((rows, vw), BF16), jax.ShapeDtypeStruct((rows, vw), BF16),
                   jax.ShapeDtypeStruct((rows, 2 * kw), F32)],
        compiler_params=_params(2),
        name="gla_proj",
    )(h, gain, shift, scale, w_in, wa1, wa2, ba)


def _gla_scan_kernel(q_ref, k_ref, v_ref, gf_ref, gb_ref, o_ref, s_ref, *, chunk, seq, ctx):
    L = chunk
    n_chunks, n_ctx = seq // L, ctx // L
    li = lax.broadcasted_iota(jnp.int32, (L, L), 0)
    si = lax.broadcasted_iota(jnp.int32, (L, L), 1)
    eye = (lax.broadcasted_iota(jnp.int32, (GLA_DK, GLA_DK), 0)
           == lax.broadcasted_iota(jnp.int32, (GLA_DK, GLA_DK), 1))
    for direction in (0, 1):
        gate_ref = gf_ref if direction == 0 else gb_ref
        tri = (li >= si) if direction == 0 else (si >= li)
        tri_f = tri.astype(F32)
        ref_row = L // 2 if direction == 0 else L - 1 - L // 2
        end_row = L - 1 if direction == 0 else 0
        s_ref[...] = jnp.zeros_like(s_ref)

        def body(step, carry, direction=direction, gate_ref=gate_ref, tri=tri, tri_f=tri_f,
                 ref_row=ref_row, end_row=end_row):
            r0 = pl.multiple_of(_chunk_of(step, direction, n_chunks, n_ctx) * L, L)
            qf = q_ref[pl.ds(r0, L), :].astype(F32)
            kf = k_ref[pl.ds(r0, L), :].astype(F32)
            vc = v_ref[pl.ds(r0, L), :]
            cum = jnp.dot(tri_f, gate_ref[pl.ds(r0, L), :], precision=lax.Precision.HIGHEST,
                          preferred_element_type=F32)
            c_ref = cum[ref_row:ref_row + 1]
            c_end = cum[end_row:end_row + 1]
            qt = (qf * jnp.exp(cum - c_ref)).astype(BF16)
            kt = (kf * jnp.exp(c_ref - cum)).astype(BF16)
            att = jnp.where(tri, lax.dot_general(qt, kt, _NT, preferred_element_type=F32), 0.0)
            state = s_ref[...]
            out = _dot(att.astype(BF16), vc) + _dot((qf * jnp.exp(cum)).astype(BF16), state.astype(BF16))
            kd = (kf * jnp.exp(c_end - cum)).astype(BF16)
            e_col = jnp.sum(jnp.where(eye, jnp.exp(c_end), 0.0), axis=1, keepdims=True)
            s_ref[...] = e_col * state + lax.dot_general(kd, vc, _TN, preferred_element_type=F32)
            if direction == 0:
                o_ref[pl.ds(r0, L), :] = out
            else:
                o_ref[pl.ds(r0, L), :] += out
            return carry

        lax.fori_loop(0, n_chunks, body, 0)


def _gla_scan(q, k, v, ga, n_batch, seq):
    return pl.pallas_call(
        functools.partial(_gla_scan_kernel, chunk=GLA_CHUNK, seq=seq, ctx=CTX_LEN),
        grid=(n_batch, GLA_HEADS),
        in_specs=[pl.BlockSpec((seq, GLA_DK), lambda b, h: (b, h)),
                  pl.BlockSpec((seq, GLA_DK), lambda b, h: (b, h)),
                  pl.BlockSpec((seq, GLA_DV), lambda b, h: (b, h)),
                  pl.BlockSpec((seq, GLA_DK), lambda b, h: (b, h)),
                  pl.BlockSpec((seq, GLA_DK), lambda b, h: (b, GLA_HEADS + h))],
        out_specs=pl.BlockSpec((seq, GLA_DV), lambda b, h: (b, h)),
        out_shape=jax.ShapeDtypeStruct((n_batch * seq, GLA_HEADS * GLA_DV), F32),
        scratch_shapes=[pltpu.VMEM((GLA_DK, GLA_DV), F32)],
        compiler_params=_params(2),
        name="gla_scan",
    )(q, k, v, ga, ga)


def _ml_proj_kernel(x_ref, g_ref, sh_ref, sc_ref, w_ref, a_ref, o_ref):
    xn = _modnorm(x_ref[...], g_ref[...], sh_ref[0], sc_ref[0]).astype(BF16)
    cw = 1024
    for j in range(MLSTM_INNER // cw):
        a_ref[:, j * cw:(j + 1) * cw] = _dot(xn, w_ref[:, j * cw:(j + 1) * cw]).astype(BF16)
        o_ref[:, j * cw:(j + 1) * cw] = _dot(
            xn, w_ref[:, MLSTM_INNER + j * cw:MLSTM_INNER + (j + 1) * cw]).astype(BF16)


def _ml_proj(h, n_batch, n_t, gain, shift, scale, w_in):
    ms = _mod_spec(n_batch, 0, True)
    rows = h.shape[0]
    return pl.pallas_call(
        _ml_proj_kernel,
        grid=(n_batch, n_t),
        in_specs=[_row_spec(D_MODEL, n_t, 0), _const_spec((1, D_MODEL)), ms, ms, _const_spec(w_in.shape)],
        out_specs=[_row_spec(MLSTM_INNER, n_t, 0), _row_spec(MLSTM_INNER, n_t, 0)],
        out_shape=[jax.ShapeDtypeStruct((rows, MLSTM_INNER), BF16)] * 2,
        compiler_params=_params(2),
        name="ml_proj",
    )(h, gain, shift, scale, w_in)


def _ml_qkv_kernel(a_ref, prev_ref, next_ref, cw_ref, cb_ref, wq_ref, wk_ref, wv_ref, wif_ref, bif_ref,
                   ac_ref, q_ref, k_ref, v_ref, gc_ref, gr_ref, *, n_t):
    t = pl.program_id(1)
    has_prev = jnp.where(t >= 2, 1.0, 0.0).astype(F32)
    has_next = jnp.where(jnp.logical_and(t >= 1, t <= n_t - 2), 1.0, 0.0).astype(F32)
    row = lax.broadcasted_iota(jnp.int32, (TM, 1), 0)
    dh = MLSTM_DH
    pre = jnp.zeros((TM, LANES), F32)
    for hd in range(MLSTM_HEADS):
        lo = hd * dh
        a_bf = a_ref[:, lo:lo + dh]
        a = a_bf.astype(F32)
        before = prev_ref[:, lo:lo + dh].astype(F32)[BF16_ROWS - 1:BF16_ROWS] * has_prev
        after = next_ref[:, lo:lo + dh].astype(F32)[0:1] * has_next
        up = jnp.where(row == 0, before, pltpu.roll(a, 1, axis=0))
        dn = jnp.where(row == TM - 1, after, pltpu.roll(a, TM - 1, axis=0))
        conv = (cw_ref[0:1, lo:lo + dh] * up + cw_ref[1:2, lo:lo + dh] * a + cw_ref[2:3, lo:lo + dh] * dn
                + cb_ref[:, lo:lo + dh])
        ac = _silu(conv).astype(BF16)
        ac_ref[:, lo:lo + dh] = ac
        q = _dot(ac, wq_ref[hd]).astype(BF16)
        k = (_dot(ac, wk_ref[hd]) * dh ** -0.5).astype(BF16)
        v = _dot(a_bf, wv_ref[hd]).astype(BF16)
        q_ref[:, lo:lo + dh] = q
        k_ref[:, lo:lo + dh] = k
        v_ref[:, lo:lo + dh] = v
        pre = pre + _dot(q, wif_ref[0, hd]) + _dot(k, wif_ref[1, hd]) + _dot(v, wif_ref[2, hd])
    pre = pre + bif_ref[...]
    lane = lax.broadcasted_iota(jnp.int32, (TM, LANES), 1)
    is_forget = (lane % (2 * MLSTM_HEADS)) >= MLSTM_HEADS
    gates = jnp.where(is_forget, _log_sigmoid(pre), pre)
    n_g = 4 * MLSTM_HEADS
    gc_ref[...] = gates[:, :n_g]
    gr_ref[...] = gates.T[:n_g, :]


def _ml_qkv(a, n_batch, n_t, conv_w, conv_b, wq, wk, wv, wif, bif):
    rows = a.shape[0]
    per_tile = TM // BF16_ROWS
    n_halo = rows // BF16_ROWS
    n_g = 4 * MLSTM_HEADS
    wide = _row_spec(MLSTM_INNER, n_t, 0)
    prev_spec = pl.BlockSpec((BF16_ROWS, MLSTM_INNER),
                             lambda b, t: (jnp.maximum((b * n_t + t) * per_tile - 1, 0), 0))
    next_spec = pl.BlockSpec((BF16_ROWS, MLSTM_INNER),
                             lambda b, t: (jnp.minimum((b * n_t + t + 1) * per_tile, n_halo - 1), 0))
    return pl.pallas_call(
        functools.partial(_ml_qkv_kernel, n_t=n_t),
        grid=(n_batch, n_t),
        in_specs=[wide, prev_spec, next_spec, _const_spec(conv_w.shape), _const_spec(conv_b.shape),
                  _const_spec(wq.shape), _const_spec(wk.shape), _const_spec(wv.shape),
                  _const_spec(wif.shape), _const_spec(bif.shape)],
        out_specs=[wide, wide, wide, wide, _row_spec(n_g, n_t, 0),
                   pl.BlockSpec((n_g, TM), lambda b, t: (0, b * n_t + t))],
        out_shape=[jax.ShapeDtypeStruct((rows, MLSTM_INNER), BF16)] * 4
        + [jax.ShapeDtypeStruct((rows, n_g), F32), jax.ShapeDtypeStruct((n_g, rows), F32)],
        compiler_params=_params(2),
        name="ml_qkv",
    )(a, a, a, conv_w, conv_b, wq, wk, wv, wif, bif)


def _ml_scan_kernel(q_ref, k_ref, v_ref, gc_ref, gr_ref, o_ref, ct_ref, n_ref, *, chunk, seq, ctx):
    L = chunk
    n_chunks, n_ctx = seq // L, ctx // L
    li = lax.broadcasted_iota(jnp.int32, (L, L), 0)
    si = lax.broadcasted_iota(jnp.int32, (L, L), 1)
    for direction in (0, 1):
        col_i, col_f = (0, 1) if direction == 0 else (2, 3)
        seen = (li >= si) if direction == 0 else (si >= li)
        seen_t = (li <= si) if direction == 0 else (si <= li)
        end_row = L - 1 if direction == 0 else 0
        ct_ref[...] = jnp.zeros_like(ct_ref)
        n_ref[...] = jnp.zeros_like(n_ref)

        def body(step, m_prev, direction=direction, col_i=col_i, col_f=col_f, seen=seen, seen_t=seen_t,
                 end_row=end_row):
            r0 = pl.multiple_of(_chunk_of(step, direction, n_chunks, n_ctx) * L, L)
            qc = q_ref[pl.ds(r0, L), :]
            kc = k_ref[pl.ds(r0, L), :]
            vc = v_ref[pl.ds(r0, L), :]
            f_col = gc_ref[pl.ds(r0, L), col_f:col_f + 1]
            i_col = gc_ref[pl.ds(r0, L), col_i:col_i + 1]
            f_row = gr_ref[col_f:col_f + 1, pl.ds(r0, L)]
            i_row = gr_ref[col_i:col_i + 1, pl.ds(r0, L)]
            b_col = jnp.sum(jnp.where(seen, f_row, 0.0), axis=1, keepdims=True)
            b_row = jnp.sum(jnp.where(seen_t, f_col, 0.0), axis=0, keepdims=True)
            d_in = jnp.where(seen, b_col - b_row + i_row, -jnp.inf)
            g = b_col + m_prev
            m_t = jnp.maximum(g, jnp.max(d_in, axis=1, keepdims=True))
            w = jnp.exp(d_in - m_t)
            w_prev = jnp.exp(g - m_t)
            s = lax.dot_general(qc, kc, _NT, preferred_element_type=F32) * w
            ct = ct_ref[...]
            num = _dot(s.astype(BF16), vc) + w_prev * _dot(qc, ct.astype(BF16))
            qn = jnp.sum(qc.astype(F32) * n_ref[...], axis=1, keepdims=True)
            den = jnp.sum(s, axis=1, keepdims=True) + w_prev * qn
            out = num / jnp.maximum(jnp.abs(den), jnp.exp(-m_t))
            b_end = b_col[end_row:end_row + 1]
            d_end = b_end - b_col + i_col
            m_new = jnp.maximum(b_end + m_prev, jnp.max(d_end, axis=0, keepdims=True))
            w_end = jnp.exp(d_end - m_new)
            a_prev = jnp.exp(b_end + m_prev - m_new)
            kw = kc.astype(F32) * w_end
            ct_ref[...] = a_prev * ct + lax.dot_general(kw.astype(BF16), vc, _TN, preferred_element_type=F32)
            n_ref[...] = a_prev * n_ref[...] + jnp.sum(kw, axis=0, keepdims=True)
            if direction == 0:
                o_ref[pl.ds(r0, L), :] = out.astype(o_ref.dtype)
            else:
                o_ref[pl.ds(r0, L), :] = (o_ref[pl.ds(r0, L), :].astype(F32) + out).astype(o_ref.dtype)
            return m_new

        lax.fori_loop(0, n_chunks, body, jnp.zeros((1, 1), F32))


def _ml_scan(q, k, v, gc, gr, n_batch, seq):
    dh = MLSTM_DH
    blk = pl.BlockSpec((seq, dh), lambda b, h: (b, h))
    return pl.pallas_call(
        functools.partial(_ml_scan_kernel, chunk=MLSTM_CHUNK, seq=seq, ctx=CTX_LEN),
        grid=(n_batch, MLSTM_HEADS),
        in_specs=[blk, blk, blk,
                  pl.BlockSpec((None, None, seq, 4), lambda b, h: (b, h, 0, 0)),
                  pl.BlockSpec((None, None, 4, seq), lambda b, h: (b, h, 0, 0))],
        out_specs=blk,
        out_shape=jax.ShapeDtypeStruct((n_batch * seq, MLSTM_INNER), BF16),
        scratch_shapes=[pltpu.VMEM((dh, dh), F32), pltpu.VMEM((1, dh), F32)],
        compiler_params=_params(2),
        name="ml_scan",
    )(q, k, v, gc, gr)


def _ml_readout_kernel(hs_ref, op_ref, ac_ref, x_ref, skip_ref, gate_ref, w_ref, o_ref):
    cw = 1024
    acc = jnp.zeros((TM, D_MODEL), F32)
    for j in range(MLSTM_INNER // cw):
        sl = slice(j * cw, (j + 1) * cw)
        y = (jax.nn.sigmoid(op_ref[:, sl].astype(F32)) * hs_ref[:, sl].astype(F32)
             + skip_ref[:, sl] * ac_ref[:, sl].astype(F32))
        acc = acc + _dot(y.astype(BF16), w_ref[sl, :])
    o_ref[...] = x_ref[...] + gate_ref[0] * acc


def _ml_readout(hs, o_pre, ac, h, n_batch, n_t, off, skip, gate, w_out):
    n_out = n_t - off
    wide = _row_spec(MLSTM_INNER, n_t, off)
    return pl.pallas_call(
        _ml_readout_kernel,
        grid=(n_batch, n_out),
        in_specs=[wide, wide, wide, _row_spec(D_MODEL, n_t, off), _const_spec((1, MLSTM_INNER)),
                  _mod_spec(n_batch, off, off == 0), _const_spec((MLSTM_INNER, D_MODEL))],
        out_specs=_row_spec(D_MODEL, n_out, 0),
        out_shape=jax.ShapeDtypeStruct((n_batch * n_out * TM, D_MODEL), F32),
        compiler_params=_params(2),
        name="ml_readout",
    )(hs, o_pre, ac, h, skip, gate, w_out)


def _rope_tables(t_lat):
    n_f = RET_DK // 4
    pos = jnp.arange(t_lat)
    inv = ROPE_BASE ** (-jnp.arange(n_f, dtype=F32) / n_f)
    ang = jnp.concatenate([(pos // GRID_W).astype(F32)[:, None] * inv,
                           (pos % GRID_W).astype(F32)[:, None] * inv], axis=-1)
    cos = jnp.concatenate([jnp.ones((CTX_LEN, 2 * n_f), F32), jnp.cos(ang)], axis=0)
    sin = jnp.concatenate([jnp.zeros((CTX_LEN, 2 * n_f), F32), jnp.sin(ang)], axis=0)
    return cos, sin


def _ret_weights(w_in):
    perm = jnp.concatenate([jnp.arange(0, RET_DK, 2), jnp.arange(1, RET_DK, 2)])
    qk_cols = (jnp.arange(2 * RET_HEADS)[:, None] * RET_DK + perm[None, :]).reshape(-1)
    cols = jnp.concatenate([qk_cols, jnp.arange(2 * RET_HEADS * RET_DK, w_in.shape[1])])
    return w_in[:, cols].astype(BF16)


def kernel(x, c, ctx, c_ctx, ada_w, ada_b, norm1_g, norm2_g, ffn_w1, ffn_w2, final_g, ret_w_in, ret_log_decay_f, ret_log_decay_b, ret_norm_g, ret_w_out, gla_w_in, gla_wa1_f, gla_wa2_f, gla_ba_f, gla_wa1_b, gla_wa2_b, gla_ba_b, gla_norm_g, gla_w_out, ml_w_in, ml_conv_w, ml_conv_b, ml_w_q, ml_w_k, ml_w_v, ml_w_if_f, ml_b_if_f, ml_w_if_b, ml_b_if_b, ml_skip, ml_w_out):
    n_batch, t_lat, _ = x.shape
    assert ctx.shape[1] == CTX_LEN == TM and t_lat % TM == 0
    seq = CTX_LEN + t_lat
    n_t = seq // TM

    h = jnp.concatenate([ctx, x], axis=1).reshape(n_batch * seq, D_MODEL)

    n_rows = n_batch + 1
    pad = -n_rows % 8
    cvec = jnp.concatenate([c, c_ctx[None, :], jnp.zeros((pad, D_MODEL), F32)], axis=0)
    mod = _ada(cvec, ada_w, ada_b)[:, :n_rows]
    mod = mod.reshape(DEPTH, n_rows, 6, 1, D_MODEL).transpose(0, 2, 1, 3, 4)

    cos, sin = _rope_tables(t_lat)
    final_g2 = final_g.reshape(1, D_MODEL)

    for i in range(DEPTH):
        last = i == DEPTH - 1
        off = 1 if last else 0
        kind, j = i % 3, i // 3
        sh1, sc1, g1, sh2, sc2, g2 = (mod[i, p] for p in range(6))
        gain1 = norm1_g[i].reshape(1, D_MODEL)
        gain2 = norm2_g[i].reshape(1, D_MODEL)
        if kind == 0:
            q, k, v, g = _ret_proj(h, n_batch, n_t, gain1, sh1, sc1, cos, sin, _ret_weights(ret_w_in[j]))
            decay = jnp.stack([ret_log_decay_f[j], ret_log_decay_b[j]])
            y = _ret_scan(decay, q, k, v, n_batch, seq)
            h = _readout(y, g, h, n_batch, n_t, off, ret_norm_g[j], g1, ret_w_out[j].astype(BF16),
                         RET_HEADS, RET_DV)
        elif kind == 1:
            kw = GLA_HEADS * GLA_DK
            wa1 = jnp.zeros((D_MODEL, LANES), F32)
            wa1 = wa1.at[:, :GLA_RANK].set(gla_wa1_f[j]).at[:, GLA_RANK:2 * GLA_RANK].set(gla_wa1_b[j])
            wa2 = jnp.zeros((LANES, 2 * kw), F32)
            wa2 = wa2.at[:GLA_RANK, :kw].set(gla_wa2_f[j]).at[GLA_RANK:2 * GLA_RANK, kw:].set(gla_wa2_b[j])
            ba = jnp.concatenate([gla_ba_f[j], gla_ba_b[j]]).reshape(1, 2 * kw)
            q, k, v, r, ga = _gla_proj(h, n_batch, n_t, gain1, sh1, sc1, gla_w_in[j].astype(BF16),
                                       wa1.astype(BF16), wa2.astype(BF16), ba)
            y = _gla_scan(q, k, v, ga, n_batch, seq)
            h = _readout(y, r, h, n_batch, n_t, off, gla_norm_g[j], g1, gla_w_out[j].astype(BF16),
                         GLA_HEADS, GLA_DV)
        else:
            a, o_pre = _ml_proj(h, n_batch, n_t, gain1, sh1, sc1, ml_w_in[j].astype(BF16))
            n_g = 4 * MLSTM_HEADS
            wif = jnp.concatenate([ml_w_if_f[j], ml_w_if_b[j]], axis=-1)
            wif = jnp.pad(wif, ((0, 0), (0, 0), (0, 0), (0, LANES - n_g))).astype(BF16)
            bif = jnp.pad(jnp.concatenate([ml_b_if_f[j], ml_b_if_b[j]]), (0, LANES - n_g)).reshape(1, LANES)
            ac, q, k, v, gc, gr = _ml_qkv(a, n_batch, n_t, ml_conv_w[j], ml_conv_b[j].reshape(1, MLSTM_INNER),
                                          ml_w_q[j].astype(BF16), ml_w_k[j].astype(BF16),
                                          ml_w_v[j].astype(BF16), wif, bif)
            gc = gc.reshape(n_batch, seq, 4, MLSTM_HEADS).transpose(0, 3, 1, 2)
            gr = gr.reshape(4, MLSTM_HEADS, n_batch, seq).transpose(2, 1, 0, 3)
            hs = _ml_scan(q, k, v, gc, gr, n_batch, seq)
            h = _ml_readout(hs, o_pre, ac, h, n_batch, n_t, off, ml_skip[j].reshape(1, MLSTM_INNER), g1,
                            ml_w_out[j].astype(BF16))
        h = _ffn(h, n_batch, n_t - off, not last, gain2, sh2, sc2, g2, final_g2, ffn_w1[i].astype(BF16),
                 ffn_w2[i].astype(BF16), final_norm=last)
    return h.reshape(n_batch, t_lat, D_MODEL)
```

```python
import functools

import jax
import jax.numpy as jnp
from jax import lax
from jax.experimental import pallas as pl
from jax.experimental.pallas import tpu as pltpu

F32 = jnp.float32
BF16 = jnp.bfloat16

D_MODEL = 1024
DEPTH = 4
GRID_W = 64
CTX_LEN = 256
NORM_EPS = 1e-6
D_FF = 4 * D_MODEL
RET_HEADS = 4
RET_DK = D_MODEL // RET_HEADS
RET_DV = 2 * D_MODEL // RET_HEADS
ROPE_BASE = 10000.0
GLA_HEADS = 4
GLA_DK = D_MODEL // 2 // GLA_HEADS
GLA_DV = D_MODEL // GLA_HEADS
GLA_RANK = 16
GLA_TAU = 16.0
MLSTM_INNER = 2 * D_MODEL
MLSTM_HEADS = 4
MLSTM_DH = MLSTM_INNER // MLSTM_HEADS

TM = CTX_LEN
LANES = 128
BF16_ROWS = 16
RET_CHUNK = 256
GLA_CHUNK = 64
MLSTM_CHUNK = 256
VMEM_LIMIT = 56 * 1024 * 1024

_NT = (((1,), (1,)), ((), ()))
_TN = (((0,), (0,)), ((), ()))


def _dot(a, b):
    return jnp.dot(a, b, preferred_element_type=F32)


def _silu(x):
    return x * jax.nn.sigmoid(x)


def _log_sigmoid(z):
    return jnp.minimum(z, 0.0) - jnp.log1p(jnp.exp(-jnp.abs(z)))


def _modnorm(x, g, shift, scale):
    y = x * lax.rsqrt(jnp.mean(x * x, axis=-1, keepdims=True) + NORM_EPS) * g
    return y * (1.0 + scale) + shift


def _params(n_axes):
    return pltpu.CompilerParams(dimension_semantics=("arbitrary",) * n_axes,
                                vmem_limit_bytes=VMEM_LIMIT)


def _const_spec(shape):
    nd = len(shape)
    return pl.BlockSpec(shape, lambda *_: (0,) * nd, pipeline_mode=pl.Buffered(1))


def _row_spec(width, n_tiles, off):
    return pl.BlockSpec((TM, width), lambda b, t: (b * n_tiles + t + off, 0))


def _mod_spec(n_batch, off, has_ctx):
    if has_ctx:
        return pl.BlockSpec((1, 1, D_MODEL), lambda b, t: (jnp.where(t + off == 0, n_batch, b), 0, 0))
    return pl.BlockSpec((1, 1, D_MODEL), lambda b, t: (b, 0, 0))


def _ada_kernel(c_ref, w_ref, b_ref, o_ref):
    s = _silu(c_ref[...]).astype(BF16)
    o_ref[0] = _dot(s, w_ref[0].astype(BF16)) + b_ref[0]


def _ada(cvec, ada_w, ada_b):
    rows = cvec.shape[0]
    tn = 512
    return pl.pallas_call(
        _ada_kernel,
        grid=(DEPTH, 6 * D_MODEL // tn),
        in_specs=[pl.BlockSpec((rows, D_MODEL), lambda l, j: (0, 0)),
                  pl.BlockSpec((1, D_MODEL, tn), lambda l, j: (l, 0, j)),
                  pl.BlockSpec((1, 1, tn), lambda l, j: (l, 0, j))],
        out_specs=pl.BlockSpec((1, rows, tn), lambda l, j: (l, 0, j)),
        out_shape=jax.ShapeDtypeStruct((DEPTH, rows, 6 * D_MODEL), F32),
        compiler_params=_params(2),
        name="adaln",
    )(cvec, ada_w, ada_b.reshape(DEPTH, 1, 6 * D_MODEL))


def _ffn_kernel(x_ref, g_ref, sh_ref, sc_ref, gate_ref, fg_ref, w1_ref, w2_ref, o_ref, *, final_norm):
    x = x_ref[...]
    xn = _modnorm(x, g_ref[...], sh_ref[0], sc_ref[0]).astype(BF16)
    fc = 1024
    acc = jnp.zeros((TM, D_MODEL), F32)
    for j in range(D_FF // fc):
        hid = _dot(xn, w1_ref[:, j * fc:(j + 1) * fc])
        hid = jnp.square(jnp.maximum(hid, 0.0)).astype(BF16)
        acc = acc + _dot(hid, w2_ref[j * fc:(j + 1) * fc, :])
    o = x + gate_ref[0] * acc
    if final_norm:
        o = o * lax.rsqrt(jnp.mean(o * o, axis=-1, keepdims=True) + NORM_EPS) * fg_ref[...]
    o_ref[...] = o


def _ffn(h, n_batch, n_t, has_ctx, gain, shift, scale, gate, final_g, w1, w2, *, final_norm):
    ms = _mod_spec(n_batch, 0, has_ctx)
    return pl.pallas_call(
        functools.partial(_ffn_kernel, final_norm=final_norm),
        grid=(n_batch, n_t),
        in_specs=[_row_spec(D_MODEL, n_t, 0), _const_spec((1, D_MODEL)), ms, ms, ms,
                  _const_spec((1, D_MODEL)), _const_spec((D_MODEL, D_FF)), _const_spec((D_FF, D_MODEL))],
        out_specs=_row_spec(D_MODEL, n_t, 0),
        out_shape=jax.ShapeDtypeStruct((n_batch * n_t * TM, D_MODEL), F32),
        compiler_params=_params(2),
        name="ffn",
    )(h, gain, shift, scale, gate, final_g, w1, w2)


def _ret_proj_kernel(x_ref, g_ref, sh_ref, sc_ref, cos_ref, sin_ref, w_ref, q_ref, k_ref, v_ref, gg_ref):
    xn = _modnorm(x_ref[...], g_ref[...], sh_ref[0], sc_ref[0]).astype(BF16)
    cos, sin = cos_ref[...], sin_ref[...]
    half = RET_DK // 2
    qk_w = RET_HEADS * RET_DK

    def rope(u):
        u1, u2 = u[:, :half], u[:, half:]
        return jnp.concatenate([u1 * cos - u2 * sin, u1 * sin + u2 * cos], axis=-1)

    for hd in range(RET_HEADS):
        lo = hd * RET_DK
        q_ref[:, lo:lo + RET_DK] = rope(_dot(xn, w_ref[:, lo:lo + RET_DK])).astype(BF16)
        u = _dot(xn, w_ref[:, qk_w + lo:qk_w + lo + RET_DK]) * RET_DK ** -0.5
        k_ref[:, lo:lo + RET_DK] = rope(u).astype(BF16)
    for hd in range(RET_HEADS):
        lo = hd * RET_DV
        v_ref[:, lo:lo + RET_DV] = _dot(xn, w_ref[:, 2 * qk_w + lo:2 * qk_w + lo + RET_DV]).astype(BF16)
        base = 2 * qk_w + RET_HEADS * RET_DV
        gg_ref[:, lo:lo + RET_DV] = _dot(xn, w_ref[:, base + lo:base + lo + RET_DV]).astype(BF16)


def _ret_proj(h, n_batch, n_t, gain, shift, scale, cos, sin, w_in):
    ms = _mod_spec(n_batch, 0, True)
    rows = h.shape[0]
    qk_w, v_w = RET_HEADS * RET_DK, RET_HEADS * RET_DV
    tab = pl.BlockSpec((TM, RET_DK // 2), lambda b, t: (t, 0))
    return pl.pallas_call(
        _ret_proj_kernel,
        grid=(n_batch, n_t),
        in_specs=[_row_spec(D_MODEL, n_t, 0), _const_spec((1, D_MODEL)), ms, ms, tab, tab,
                  _const_spec(w_in.shape)],
        out_specs=[_row_spec(qk_w, n_t, 0), _row_spec(qk_w, n_t, 0), _row_spec(v_w, n_t, 0), _row_spec(v_w, n_t, 0)],
        out_shape=[jax.ShapeDtypeStruct((rows, qk_w), BF16), jax.ShapeDtypeStruct((rows, qk_w), BF16),
                   jax.ShapeDtypeStruct((rows, v_w), BF16), jax.ShapeDtypeStruct((rows, v_w), BF16)],
        compiler_params=_params(2),
        name="ret_proj",
    )(h, gain, shift, scale, cos, sin, w_in)


def _block_starts(step, n_blocks):
    return step, jnp.where(step == 0, 0, n_blocks - step)


def _ret_scan_kernel(ld_ref, q_ref, k_ref, v_ref, o_ref, sf_ref, sb_ref, dmf_ref, dmb_ref, *, chunk, seq):
    L = chunk
    hd = pl.program_id(1)
    n_blocks = seq // L
    li = lax.broadcasted_iota(jnp.int32, (L, L), 0)
    si = lax.broadcasted_iota(jnp.int32, (L, L), 1)
    row = lax.broadcasted_iota(jnp.int32, (L, 1), 0).astype(F32)
    consts = []
    for direction, dm_ref in ((0, dmf_ref), (1, dmb_ref)):
        lg = -jnp.exp(jnp.full((1, 1), ld_ref[direction, hd], F32))
        if direction == 0:
            dist = li - si
            q_dec = jnp.exp((row + 1.0) * lg)
            k_dec = jnp.exp((L - 1.0 - row) * lg)
        else:
            dist = si - li
            q_dec = jnp.exp((L - row) * lg)
            k_dec = jnp.exp(row * lg)
        dm_ref[...] = jnp.where(dist >= 0, jnp.exp(jnp.maximum(dist, 0).astype(F32) * lg), 0.0)
        consts.append((q_dec, k_dec, jnp.exp(float(L) * lg)))
    sf_ref[...] = jnp.zeros_like(sf_ref)
    sb_ref[...] = jnp.zeros_like(sb_ref)
    o_ref[...] = jnp.zeros_like(o_ref)

    def chunk_step(r0, s_ref, dm_ref, q_dec, k_dec, s_dec):
        qc = q_ref[pl.ds(r0, L), :]
        kc = k_ref[pl.ds(r0, L), :]
        vc = v_ref[pl.ds(r0, L), :]
        att = lax.dot_general(qc, kc, _NT, preferred_element_type=F32) * dm_ref[...]
        state = s_ref[...]
        out = _dot(att.astype(BF16), vc) + q_dec * _dot(qc, state.astype(BF16))
        kd = (kc.astype(F32) * k_dec).astype(BF16)
        s_ref[...] = s_dec * state + lax.dot_general(kd, vc, _TN, preferred_element_type=F32)
        o_ref[pl.ds(r0, L), :] += out

    def body(step, carry):
        blk_f, blk_b = _block_starts(step, n_blocks)
        chunk_step(pl.multiple_of(blk_f * L, L), sf_ref, dmf_ref, *consts[0])
        chunk_step(pl.multiple_of(blk_b * L, L), sb_ref, dmb_ref, *consts[1])
        return carry

    lax.fori_loop(0, n_blocks, body, 0)


def _ret_scan(log_decay, q, k, v, n_batch, seq):
    assert RET_CHUNK == CTX_LEN
    return pl.pallas_call(
        functools.partial(_ret_scan_kernel, chunk=RET_CHUNK, seq=seq),
        grid=(n_batch, RET_HEADS),
        in_specs=[pl.BlockSpec(memory_space=pltpu.SMEM),
                  pl.BlockSpec((seq, RET_DK), lambda b, h: (b, h)),
                  pl.BlockSpec((seq, RET_DK), lambda b, h: (b, h)),
                  pl.BlockSpec((seq, RET_DV), lambda b, h: (b, h))],
        out_specs=pl.BlockSpec((seq, RET_DV), lambda b, h: (b, h)),
        out_shape=jax.ShapeDtypeStruct((n_batch * seq, RET_HEADS * RET_DV), F32),
        scratch_shapes=[pltpu.VMEM((RET_DK, RET_DV), F32), pltpu.VMEM((RET_DK, RET_DV), F32),
                        pltpu.VMEM((RET_CHUNK, RET_CHUNK), F32), pltpu.VMEM((RET_CHUNK, RET_CHUNK), F32)],
        compiler_params=_params(2),
        name="ret_scan",
    )(log_decay, q, k, v)


def _readout_kernel(y_ref, g_ref, x_ref, ng_ref, gate_ref, w_ref, o_ref, *, heads, dv):
    acc = jnp.zeros((TM, D_MODEL), F32)
    for hd in range(heads):
        lo = hd * dv
        y = y_ref[:, lo:lo + dv]
        yn = y * lax.rsqrt(jnp.mean(y * y, axis=-1, keepdims=True) + NORM_EPS) * ng_ref[hd:hd + 1, :]
        z = (_silu(g_ref[:, lo:lo + dv].astype(F32)) * yn).astype(BF16)
        acc = acc + _dot(z, w_ref[lo:lo + dv, :])
    o_ref[...] = x_ref[...] + gate_ref[0] * acc


def _readout(y, g, h, n_batch, n_t, off, norm_g, gate, w_out, heads, dv):
    n_out = n_t - off
    return pl.pallas_call(
        functools.partial(_readout_kernel, heads=heads, dv=dv),
        grid=(n_batch, n_out),
        in_specs=[_row_spec(heads * dv, n_t, off), _row_spec(heads * dv, n_t, off), _row_spec(D_MODEL, n_t, off),
                  _const_spec((heads, dv)), _mod_spec(n_batch, off, off == 0), _const_spec((heads * dv, D_MODEL))],
        out_specs=_row_spec(D_MODEL, n_out, 0),
        out_shape=jax.ShapeDtypeStruct((n_batch * n_out * TM, D_MODEL), F32),
        compiler_params=_params(2),
        name="readout",
    )(y, g, h, norm_g, gate, w_out)


def _gla_proj_kernel(x_ref, g_ref, sh_ref, sc_ref, w_ref, wa1_ref, wa2_ref, ba_ref,
                     q_ref, k_ref, v_ref, r_ref, ga_ref):
    xn = _modnorm(x_ref[...], g_ref[...], sh_ref[0], sc_ref[0]).astype(BF16)
    kw, vw = GLA_HEADS * GLA_DK, GLA_HEADS * GLA_DV
    q_ref[...] = (_dot(xn, w_ref[:, :kw]) * GLA_DK ** -0.5).astype(BF16)
    k_ref[...] = _dot(xn, w_ref[:, kw:2 * kw]).astype(BF16)
    v_ref[...] = _dot(xn, w_ref[:, 2 * kw:2 * kw + vw]).astype(BF16)
    r_ref[...] = _dot(xn, w_ref[:, 2 * kw + vw:]).astype(BF16)
    low = _dot(xn, wa1_ref[...]).astype(BF16)
    z = _dot(low, wa2_ref[...]) + ba_ref[...]
    ga_ref[...] = _log_sigmoid(z) / GLA_TAU


def _gla_proj(h, n_batch, n_t, gain, shift, scale, w_in, wa1, wa2, ba):
    ms = _mod_spec(n_batch, 0, True)
    rows = h.shape[0]
    kw, vw = GLA_HEADS * GLA_DK, GLA_HEADS * GLA_DV
    return pl.pallas_call(
        _gla_proj_kernel,
        grid=(n_batch, n_t),
        in_specs=[_row_spec(D_MODEL, n_t, 0), _const_spec((1, D_MODEL)), ms, ms, _const_spec(w_in.shape),
                  _const_spec(wa1.shape), _const_spec(wa2.shape), _const_spec(ba.shape)],
        out_specs=[_row_spec(kw, n_t, 0), _row_spec(kw, n_t, 0), _row_spec(vw, n_t, 0), _row_spec(vw, n_t, 0),
                   _row_spec(2 * kw, n_t, 0)],
        out_shape=[jax.ShapeDtypeStruct((rows, kw), BF16), jax.ShapeDtypeStruct((rows, kw), BF16),
                   jax.ShapeDtypeStruct((rows, vw), BF16), jax.ShapeDtypeStruct((rows, vw), BF16),
                   jax.ShapeDtypeStruct((rows, 2 * kw), F32)],
        compiler_params=_params(2),
        name="gla_proj",
    )(h, gain, shift, scale, w_in, wa1, wa2, ba)


def _gla_scan_kernel(q_ref, k_ref, v_ref, gf_ref, gb_ref, o_ref, sf_ref, sb_ref, trif_ref, trib_ref,
                     *, sub, block, seq):
    C, blk = sub, block
    n_sub, n_blocks = blk // C, seq // blk
    li = lax.broadcasted_iota(jnp.int32, (blk, blk), 0)
    si = lax.broadcasted_iota(jnp.int32, (blk, blk), 1)
    trif_ref[...] = (li >= si).astype(BF16)
    trib_ref[...] = (si >= li).astype(BF16)
    dl = lax.broadcasted_iota(jnp.int32, (C, C), 0)
    ds = lax.broadcasted_iota(jnp.int32, (C, C), 1)
    diag_masks = (dl >= ds, ds >= dl)
    eye = (lax.broadcasted_iota(jnp.int32, (GLA_DK, GLA_DK), 0)
           == lax.broadcasted_iota(jnp.int32, (GLA_DK, GLA_DK), 1))
    sf_ref[...] = jnp.zeros_like(sf_ref)
    sb_ref[...] = jnp.zeros_like(sb_ref)
    o_ref[...] = jnp.zeros_like(o_ref)

    def block_step(direction, r0):
        fwd = direction == 0
        gate_ref, tri_ref, s_ref = (gf_ref, trif_ref, sf_ref) if fwd else (gb_ref, trib_ref, sb_ref)
        g = gate_ref[pl.ds(r0, blk), :]
        g_hi = g.astype(BF16)
        rem = g - g_hi.astype(F32)
        g_mid = rem.astype(BF16)
        g_lo = (rem - g_mid.astype(F32)).astype(BF16)
        tri = tri_ref[...]
        cum = _dot(tri, g_hi) + _dot(tri, g_mid) + _dot(tri, g_lo)
        qf = q_ref[pl.ds(r0, blk), :].astype(F32)
        kf = k_ref[pl.ds(r0, blk), :].astype(F32)
        vb = v_ref[pl.ds(r0, blk), :]
        end_row = blk - 1 if fwd else 0
        c_end = cum[end_row:end_row + 1]
        state = s_ref[...]
        inter = _dot((qf * jnp.exp(cum)).astype(BF16), state.astype(BF16))
        kd = (kf * jnp.exp(c_end - cum)).astype(BF16)
        e_col = jnp.sum(jnp.where(eye, jnp.exp(c_end), 0.0), axis=1, keepdims=True)
        s_ref[...] = e_col * state + lax.dot_general(kd, vb, _TN, preferred_element_type=F32)
        outs = {}
        for i in range(n_sub):
            lo = i * C if fwd else blk - (i + 1) * C
            rows = slice(lo, lo + C)
            mid = lo + (C // 2 if fwd else C - 1 - C // 2)
            cq = cum[rows]
            c_mid = cum[mid:mid + 1]
            qd = (qf[rows] * jnp.exp(cq - c_mid)).astype(BF16)
            kg = (kf[rows] * jnp.exp(c_mid - cq)).astype(BF16)
            att = jnp.where(diag_masks[direction], lax.dot_general(qd, kg, _NT, preferred_element_type=F32), 0.0)
            o_i = _dot(att.astype(BF16), vb[rows])
            if i > 0:
                prev = slice(0, lo) if fwd else slice(lo + C, blk)
                edge = lo - 1 if fwd else lo + C
                c_edge = cum[edge:edge + 1]
                ql = (qf[rows] * jnp.exp(cq - c_edge)).astype(BF16)
                kk = (kf[prev] * jnp.exp(c_edge - cum[prev])).astype(BF16)
                att = lax.dot_general(ql, kk, _NT, preferred_element_type=F32)
                o_i = o_i + _dot(att.astype(BF16), vb[prev])
            outs[lo] = o_i
        intra = jnp.concatenate([outs[lo] for lo in sorted(outs)], axis=0)
        o_ref[pl.ds(r0, blk), :] += intra + inter

    def body(step, carry):
        blk_f, blk_b = _block_starts(step, n_blocks)
        block_step(0, pl.multiple_of(blk_f * blk, blk))
        block_step(1, pl.multiple_of(blk_b * blk, blk))
        return carry

    lax.fori_loop(0, n_blocks, body, 0)


def _gla_scan(q, k, v, ga, n_batch, seq):
    return pl.pallas_call(
        functools.partial(_gla_scan_kernel, sub=GLA_CHUNK, block=CTX_LEN, seq=seq),
        grid=(n_batch, GLA_HEADS),
        in_specs=[pl.BlockSpec((seq, GLA_DK), lambda b, h: (b, h)),
                  pl.BlockSpec((seq, GLA_DK), lambda b, h: (b, h)),
                  pl.BlockSpec((seq, GLA_DV), lambda b, h: (b, h)),
                  pl.BlockSpec((seq, GLA_DK), lambda b, h: (b, h)),
                  pl.BlockSpec((seq, GLA_DK), lambda b, h: (b, GLA_HEADS + h))],
        out_specs=pl.BlockSpec((seq, GLA_DV), lambda b, h: (b, h)),
        out_shape=jax.ShapeDtypeStruct((n_batch * seq, GLA_HEADS * GLA_DV), F32),
        scratch_shapes=[pltpu.VMEM((GLA_DK, GLA_DV), F32), pltpu.VMEM((GLA_DK, GLA_DV), F32),
                        pltpu.VMEM((CTX_LEN, CTX_LEN), BF16), pltpu.VMEM((CTX_LEN, CTX_LEN), BF16)],
        compiler_params=_params(2),
        name="gla_scan",
    )(q, k, v, ga, ga)


def _ml_proj_kernel(x_ref, g_ref, sh_ref, sc_ref, w_ref, a_ref, o_ref):
    xn = _modnorm(x_ref[...], g_ref[...], sh_ref[0], sc_ref[0]).astype(BF16)
    cw = 1024
    for j in range(MLSTM_INNER // cw):
        a_ref[:, j * cw:(j + 1) * cw] = _dot(xn, w_ref[:, j * cw:(j + 1) * cw]).astype(BF16)
        o_ref[:, j * cw:(j + 1) * cw] = _dot(
            xn, w_ref[:, MLSTM_INNER + j * cw:MLSTM_INNER + (j + 1) * cw]).astype(BF16)


def _ml_proj(h, n_batch, n_t, gain, shift, scale, w_in):
    ms = _mod_spec(n_batch, 0, True)
    rows = h.shape[0]
    return pl.pallas_call(
        _ml_proj_kernel,
        grid=(n_batch, n_t),
        in_specs=[_row_spec(D_MODEL, n_t, 0), _const_spec((1, D_MODEL)), ms, ms, _const_spec(w_in.shape)],
        out_specs=[_row_spec(MLSTM_INNER, n_t, 0), _row_spec(MLSTM_INNER, n_t, 0)],
        out_shape=[jax.ShapeDtypeStruct((rows, MLSTM_INNER), BF16)] * 2,
        compiler_params=_params(2),
        name="ml_proj",
    )(h, gain, shift, scale, w_in)


def _ml_qkv_kernel(a_ref, prev_ref, next_ref, cw_ref, cb_ref, wq_ref, wk_ref, wv_ref, wif_ref, bif_ref,
                   ac_ref, q_ref, k_ref, v_ref, gc_ref, gr_ref, *, n_t):
    t = pl.program_id(1)
    has_prev = jnp.where(t >= 2, 1.0, 0.0).astype(F32)
    has_next = jnp.where(jnp.logical_and(t >= 1, t <= n_t - 2), 1.0, 0.0).astype(F32)
    row = lax.broadcasted_iota(jnp.int32, (TM, 1), 0)
    dh = MLSTM_DH
    pre = jnp.zeros((TM, LANES), F32)
    for hd in range(MLSTM_HEADS):
        lo = hd * dh
        a_bf = a_ref[:, lo:lo + dh]
        a = a_bf.astype(F32)
        before = prev_ref[:, lo:lo + dh].astype(F32)[BF16_ROWS - 1:BF16_ROWS] * has_prev
        after = next_ref[:, lo:lo + dh].astype(F32)[0:1] * has_next
        up = jnp.where(row == 0, before, pltpu.roll(a, 1, axis=0))
        dn = jnp.where(row == TM - 1, after, pltpu.roll(a, TM - 1, axis=0))
        conv = (cw_ref[0:1, lo:lo + dh] * up + cw_ref[1:2, lo:lo + dh] * a + cw_ref[2:3, lo:lo + dh] * dn
                + cb_ref[:, lo:lo + dh])
        ac = _silu(conv).astype(BF16)
        ac_ref[:, lo:lo + dh] = ac
        q = _dot(ac, wq_ref[hd]).astype(BF16)
        k = (_dot(ac, wk_ref[hd]) * dh ** -0.5).astype(BF16)
        v = _dot(a_bf, wv_ref[hd]).astype(BF16)
        q_ref[:, lo:lo + dh] = q
        k_ref[:, lo:lo + dh] = k
        v_ref[:, lo:lo + dh] = v
        pre = pre + _dot(q, wif_ref[0, hd]) + _dot(k, wif_ref[1, hd]) + _dot(v, wif_ref[2, hd])
    pre = pre + bif_ref[...]
    lane = lax.broadcasted_iota(jnp.int32, (TM, LANES), 1)
    is_forget = (lane % (2 * MLSTM_HEADS)) >= MLSTM_HEADS
    gates = jnp.where(is_forget, _log_sigmoid(pre), pre)
    n_g = 4 * MLSTM_HEADS
    gc_ref[...] = gates[:, :n_g]
    gr_ref[...] = gates.T[:n_g, :]


def _ml_qkv(a, n_batch, n_t, conv_w, conv_b, wq, wk, wv, wif, bif):
    rows = a.shape[0]
    per_tile = TM // BF16_ROWS
    n_halo = rows // BF16_ROWS
    n_g = 4 * MLSTM_HEADS
    wide = _row_spec(MLSTM_INNER, n_t, 0)
    prev_spec = pl.BlockSpec((BF16_ROWS, MLSTM_INNER),
                             lambda b, t: (jnp.maximum((b * n_t + t) * per_tile - 1, 0), 0))
    next_spec = pl.BlockSpec((BF16_ROWS, MLSTM_INNER),
                             lambda b, t: (jnp.minimum((b * n_t + t + 1) * per_tile, n_halo - 1), 0))
    return pl.pallas_call(
        functools.partial(_ml_qkv_kernel, n_t=n_t),
        grid=(n_batch, n_t),
        in_specs=[wide, prev_spec, next_spec, _const_spec(conv_w.shape), _const_spec(conv_b.shape),
                  _const_spec(wq.shape), _const_spec(wk.shape), _const_spec(wv.shape),
                  _const_spec(wif.shape), _const_spec(bif.shape)],
        out_specs=[wide, wide, wide, wide, _row_spec(n_g, n_t, 0),
                   pl.BlockSpec((n_g, TM), lambda b, t: (0, b * n_t + t))],
        out_shape=[jax.ShapeDtypeStruct((rows, MLSTM_INNER), BF16)] * 4
        + [jax.ShapeDtypeStruct((rows, n_g), F32), jax.ShapeDtypeStruct((n_g, rows), F32)],
        compiler_params=_params(2),
        name="ml_qkv",
    )(a, a, a, conv_w, conv_b, wq, wk, wv, wif, bif)


def _ml_scan_kernel(q_ref, k_ref, v_ref, gc_ref, gr_ref, o_ref, ctf_ref, ctb_ref, nf_ref, nb_ref, *, chunk, seq):
    L = chunk
    n_blocks = seq // L
    li = lax.broadcasted_iota(jnp.int32, (L, L), 0)
    si = lax.broadcasted_iota(jnp.int32, (L, L), 1)
    for ref in (ctf_ref, ctb_ref, nf_ref, nb_ref, o_ref):
        ref[...] = jnp.zeros_like(ref)

    def chunk_step(direction, r0, m_prev):
        ct_ref, n_ref = (ctf_ref, nf_ref) if direction == 0 else (ctb_ref, nb_ref)
        col_i, col_f = (0, 1) if direction == 0 else (2, 3)
        seen = (li >= si) if direction == 0 else (si >= li)
        seen_t = (li <= si) if direction == 0 else (si <= li)
        end_row = L - 1 if direction == 0 else 0
        qc = q_ref[pl.ds(r0, L), :]
        kc = k_ref[pl.ds(r0, L), :]
        vc = v_ref[pl.ds(r0, L), :]
        f_col = gc_ref[pl.ds(r0, L), col_f:col_f + 1]
        i_col = gc_ref[pl.ds(r0, L), col_i:col_i + 1]
        f_row = gr_ref[col_f:col_f + 1, pl.ds(r0, L)]
        i_row = gr_ref[col_i:col_i + 1, pl.ds(r0, L)]
        b_col = jnp.sum(jnp.where(seen, f_row, 0.0), axis=1, keepdims=True)
        b_row = jnp.sum(jnp.where(seen_t, f_col, 0.0), axis=0, keepdims=True)
        d_in = jnp.where(seen, b_col - b_row + i_row, -jnp.inf)
        g = b_col + m_prev
        m_t = jnp.maximum(g, jnp.max(d_in, axis=1, keepdims=True))
        w = jnp.exp(d_in - m_t)
        w_prev = jnp.exp(g - m_t)
        s = lax.dot_general(qc, kc, _NT, preferred_element_type=F32) * w
        ct = ct_ref[...]
        num = _dot(s.astype(BF16), vc) + w_prev * _dot(qc, ct.astype(BF16))
        qn = jnp.sum(qc.astype(F32) * n_ref[...], axis=1, keepdims=True)
        den = jnp.sum(s, axis=1, keepdims=True) + w_prev * qn
        out = num / jnp.maximum(jnp.abs(den), jnp.exp(-m_t))
        b_end = b_col[end_row:end_row + 1]
        d_end = b_end - b_col + i_col
        m_new = jnp.maximum(b_end + m_prev, jnp.max(d_end, axis=0, keepdims=True))
        w_end = jnp.exp(d_end - m_new)
        a_prev = jnp.exp(b_end + m_prev - m_new)
        kw = kc.astype(F32) * w_end
        ct_ref[...] = a_prev * ct + lax.dot_general(kw.astype(BF16), vc, _TN, preferred_element_type=F32)
        n_ref[...] = a_prev * n_ref[...] + jnp.sum(kw, axis=0, keepdims=True)
        o_ref[pl.ds(r0, L), :] = (o_ref[pl.ds(r0, L), :].astype(F32) + out).astype(o_ref.dtype)
        return m_new

    def body(step, carry):
        blk_f, blk_b = _block_starts(step, n_blocks)
        m_f = chunk_step(0, pl.multiple_of(blk_f * L, L), carry[0])
        m_b = chunk_step(1, pl.multiple_of(blk_b * L, L), carry[1])
        return m_f, m_b

    lax.fori_loop(0, n_blocks, body, (jnp.zeros((1, 1), F32), jnp.zeros((1, 1), F32)))


def _ml_scan(q, k, v, gc, gr, n_batch, seq):
    assert MLSTM_CHUNK == CTX_LEN
    dh = MLSTM_DH
    blk = pl.BlockSpec((seq, dh), lambda b, h: (b, h))
    return pl.pallas_call(
        functools.partial(_ml_scan_kernel, chunk=MLSTM_CHUNK, seq=seq),
        grid=(n_batch, MLSTM_HEADS),
        in_specs=[blk, blk, blk,
                  pl.BlockSpec((None, None, seq, 4), lambda b, h: (b, h, 0, 0)),
                  pl.BlockSpec((None, None, 4, seq), lambda b, h: (b, h, 0, 0))],
        out_specs=blk,
        out_shape=jax.ShapeDtypeStruct((n_batch * seq, MLSTM_INNER), BF16),
        scratch_shapes=[pltpu.VMEM((dh, dh), F32), pltpu.VMEM((dh, dh), F32),
                        pltpu.VMEM((1, dh), F32), pltpu.VMEM((1, dh), F32)],
        compiler_params=_params(2),
        name="ml_scan",
    )(q, k, v, gc, gr)


def _ml_readout_kernel(hs_ref, op_ref, ac_ref, x_ref, skip_ref, gate_ref, w_ref, o_ref):
    cw = 1024
    acc = jnp.zeros((TM, D_MODEL), F32)
    for j in range(MLSTM_INNER // cw):
        sl = slice(j * cw, (j + 1) * cw)
        y = (jax.nn.sigmoid(op_ref[:, sl].astype(F32)) * hs_ref[:, sl].astype(F32)
             + skip_ref[:, sl] * ac_ref[:, sl].astype(F32))
        acc = acc + _dot(y.astype(BF16), w_ref[sl, :])
    o_ref[...] = x_ref[...] + gate_ref[0] * acc


def _ml_readout(hs, o_pre, ac, h, n_batch, n_t, off, skip, gate, w_out):
    n_out = n_t - off
    wide = _row_spec(MLSTM_INNER, n_t, off)
    return pl.pallas_call(
        _ml_readout_kernel,
        grid=(n_batch, n_out),
        in_specs=[wide, wide, wide, _row_spec(D_MODEL, n_t, off), _const_spec((1, MLSTM_INNER)),
                  _mod_spec(n_batch, off, off == 0), _const_spec((MLSTM_INNER, D_MODEL))],
        out_specs=_row_spec(D_MODEL, n_out, 0),
        out_shape=jax.ShapeDtypeStruct((n_batch * n_out * TM, D_MODEL), F32),
        compiler_params=_params(2),
        name="ml_readout",
    )(hs, o_pre, ac, h, skip, gate, w_out)


def _rope_tables(t_lat):
    n_f = RET_DK // 4
    pos = jnp.arange(t_lat)
    inv = ROPE_BASE ** (-jnp.arange(n_f, dtype=F32) / n_f)
    ang = jnp.concatenate([(pos // GRID_W).astype(F32)[:, None] * inv,
                           (pos % GRID_W).astype(F32)[:, None] * inv], axis=-1)
    cos = jnp.concatenate([jnp.ones((CTX_LEN, 2 * n_f), F32), jnp.cos(ang)], axis=0)
    sin = jnp.concatenate([jnp.zeros((CTX_LEN, 2 * n_f), F32), jnp.sin(ang)], axis=0)
    return cos, sin


def _ret_weights(w_in):
    perm = jnp.concatenate([jnp.arange(0, RET_DK, 2), jnp.arange(1, RET_DK, 2)])
    qk_cols = (jnp.arange(2 * RET_HEADS)[:, None] * RET_DK + perm[None, :]).reshape(-1)
    cols = jnp.concatenate([qk_cols, jnp.arange(2 * RET_HEADS * RET_DK, w_in.shape[1])])
    return w_in[:, cols].astype(BF16)


def kernel(x, c, ctx, c_ctx, ada_w, ada_b, norm1_g, norm2_g, ffn_w1, ffn_w2, final_g, ret_w_in, ret_log_decay_f, ret_log_decay_b, ret_norm_g, ret_w_out, gla_w_in, gla_wa1_f, gla_wa2_f, gla_ba_f, gla_wa1_b, gla_wa2_b, gla_ba_b, gla_norm_g, gla_w_out, ml_w_in, ml_conv_w, ml_conv_b, ml_w_q, ml_w_k, ml_w_v, ml_w_if_f, ml_b_if_f, ml_w_if_b, ml_b_if_b, ml_skip, ml_w_out):
    n_batch, t_lat, _ = x.shape
    assert ctx.shape[1] == CTX_LEN == TM and t_lat % TM == 0
    seq = CTX_LEN + t_lat
    n_t = seq // TM

    h = jnp.concatenate([ctx, x], axis=1).reshape(n_batch * seq, D_MODEL)

    n_rows = n_batch + 1
    pad = -n_rows % 8
    cvec = jnp.concatenate([c, c_ctx[None, :], jnp.zeros((pad, D_MODEL), F32)], axis=0)
    mod = _ada(cvec, ada_w, ada_b)[:, :n_rows]
    mod = mod.reshape(DEPTH, n_rows, 6, 1, D_MODEL).transpose(0, 2, 1, 3, 4)

    cos, sin = _rope_tables(t_lat)
    final_g2 = final_g.reshape(1, D_MODEL)

    for i in range(DEPTH):
        last = i == DEPTH - 1
        off = 1 if last else 0
        kind, j = i % 3, i // 3
        sh1, sc1, g1, sh2, sc2, g2 = (mod[i, p] for p in range(6))
        gain1 = norm1_g[i].reshape(1, D_MODEL)
        gain2 = norm2_g[i].reshape(1, D_MODEL)
        if kind == 0:
            q, k, v, g = _ret_proj(h, n_batch, n_t, gain1, sh1, sc1, cos, sin, _ret_weights(ret_w_in[j]))
            decay = jnp.stack([ret_log_decay_f[j], ret_log_decay_b[j]])
            y = _ret_scan(decay, q, k, v, n_batch, seq)
            h = _readout(y, g, h, n_batch, n_t, off, ret_norm_g[j], g1, ret_w_out[j].astype(BF16),
                         RET_HEADS, RET_DV)
        elif kind == 1:
            kw = GLA_HEADS * GLA_DK
            wa1 = jnp.zeros((D_MODEL, LANES), F32)
            wa1 = wa1.at[:, :GLA_RANK].set(gla_wa1_f[j]).at[:, GLA_RANK:2 * GLA_RANK].set(gla_wa1_b[j])
            wa2 = jnp.zeros((LANES, 2 * kw), F32)
            wa2 = wa2.at[:GLA_RANK, :kw].set(gla_wa2_f[j]).at[GLA_RANK:2 * GLA_RANK, kw:].set(gla_wa2_b[j])
            ba = jnp.concatenate([gla_ba_f[j], gla_ba_b[j]]).reshape(1, 2 * kw)
            q, k, v, r, ga = _gla_proj(h, n_batch, n_t, gain1, sh1, sc1, gla_w_in[j].astype(BF16),
                                       wa1.astype(BF16), wa2.astype(BF16), ba)
            y = _gla_scan(q, k, v, ga, n_batch, seq)
            h = _readout(y, r, h, n_batch, n_t, off, gla_norm_g[j], g1, gla_w_out[j].astype(BF16),
                         GLA_HEADS, GLA_DV)
        else:
            a, o_pre = _ml_proj(h, n_batch, n_t, gain1, sh1, sc1, ml_w_in[j].astype(BF16))
            n_g = 4 * MLSTM_HEADS
            wif = jnp.concatenate([ml_w_if_f[j], ml_w_if_b[j]], axis=-1)
            wif = jnp.pad(wif, ((0, 0), (0, 0), (0, 0), (0, LANES - n_g))).astype(BF16)
            bif = jnp.pad(jnp.concatenate([ml_b_if_f[j], ml_b_if_b[j]]), (0, LANES - n_g)).reshape(1, LANES)
            ac, q, k, v, gc, gr = _ml_qkv(a, n_batch, n_t, ml_conv_w[j], ml_conv_b[j].reshape(1, MLSTM_INNER),
                                          ml_w_q[j].astype(BF16), ml_w_k[j].astype(BF16),
                                          ml_w_v[j].astype(BF16), wif, bif)
            gc = gc.reshape(n_batch, seq, 4, MLSTM_HEADS).transpose(0, 3, 1, 2)
            gr = gr.reshape(4, MLSTM_HEADS, n_batch, seq).transpose(2, 1, 0, 3)
            hs = _ml_scan(q, k, v, gc, gr, n_batch, seq)
            h = _ml_readout(hs, o_pre, ac, h, n_batch, n_t, off, ml_skip[j].reshape(1, MLSTM_INNER), g1,
                            ml_w_out[j].astype(BF16))
        h = _ffn(h, n_batch, n_t - off, not last, gain2, sh2, sc2, g2, final_g2, ffn_w1[i].astype(BF16),
                 ffn_w2[i].astype(BF16), final_norm=last)
    return h.reshape(n_batch, t_lat, D_MODEL)
```

```python
import functools

import jax
import jax.numpy as jnp
from jax import lax
from jax.experimental import pallas as pl
from jax.experimental.pallas import tpu as pltpu

F32 = jnp.float32
BF16 = jnp.bfloat16

D_MODEL = 1024
DEPTH = 4
GRID_W = 64
CTX_LEN = 256
NORM_EPS = 1e-6
D_FF = 4 * D_MODEL
RET_HEADS = 4
RET_DK = D_MODEL // RET_HEADS
RET_DV = 2 * D_MODEL // RET_HEADS
ROPE_BASE = 10000.0
GLA_HEADS = 4
GLA_DK = D_MODEL // 2 // GLA_HEADS
GLA_DV = D_MODEL // GLA_HEADS
GLA_RANK = 16
GLA_TAU = 16.0
MLSTM_INNER = 2 * D_MODEL
MLSTM_HEADS = 4
MLSTM_DH = MLSTM_INNER // MLSTM_HEADS

TM = CTX_LEN
LANES = 128
BF16_ROWS = 16
RET_CHUNK = 256
GLA_CHUNK = 64
MLSTM_CHUNK = 256
VMEM_LIMIT = 56 * 1024 * 1024

_NT = (((1,), (1,)), ((), ()))
_TN = (((0,), (0,)), ((), ()))


def _dot(a, b):
    return jnp.dot(a, b, preferred_element_type=F32)


def _silu(x):
    return x * jax.nn.sigmoid(x)


def _log_sigmoid(z):
    return jnp.minimum(z, 0.0) - jnp.log1p(jnp.exp(-jnp.abs(z)))


def _modnorm(x, g, shift, scale):
    y = x * lax.rsqrt(jnp.mean(x * x, axis=-1, keepdims=True) + NORM_EPS) * g
    return y * (1.0 + scale) + shift


def _params(n_axes):
    return pltpu.CompilerParams(dimension_semantics=("arbitrary",) * n_axes,
                                vmem_limit_bytes=VMEM_LIMIT)


def _const_spec(shape):
    nd = len(shape)
    return pl.BlockSpec(shape, lambda *_: (0,) * nd, pipeline_mode=pl.Buffered(1))


def _row_spec(width, n_tiles, off):
    return pl.BlockSpec((TM, width), lambda b, t: (b * n_tiles + t + off, 0))


def _mod_spec(n_batch, off, has_ctx):
    if has_ctx:
        return pl.BlockSpec((1, 1, D_MODEL), lambda b, t: (jnp.where(t + off == 0, n_batch, b), 0, 0))
    return pl.BlockSpec((1, 1, D_MODEL), lambda b, t: (b, 0, 0))


def _ada_kernel(c_ref, w_ref, b_ref, o_ref):
    s = _silu(c_ref[...]).astype(BF16)
    o_ref[0] = _dot(s, w_ref[0].astype(BF16)) + b_ref[0]


def _ada(cvec, ada_w, ada_b):
    rows = cvec.shape[0]
    tn = 512
    return pl.pallas_call(
        _ada_kernel,
        grid=(DEPTH, 6 * D_MODEL // tn),
        in_specs=[pl.BlockSpec((rows, D_MODEL), lambda l, j: (0, 0)),
                  pl.BlockSpec((1, D_MODEL, tn), lambda l, j: (l, 0, j)),
                  pl.BlockSpec((1, 1, tn), lambda l, j: (l, 0, j))],
        out_specs=pl.BlockSpec((1, rows, tn), lambda l, j: (l, 0, j)),
        out_shape=jax.ShapeDtypeStruct((DEPTH, rows, 6 * D_MODEL), F32),
        compiler_params=_params(2),
        name="adaln",
    )(cvec, ada_w, ada_b.reshape(DEPTH, 1, 6 * D_MODEL))


def _gated_mix(y_ref, g_ref, ng_ref, w_ref, *, heads, dv):
    acc = jnp.zeros((TM, D_MODEL), F32)
    for hd in range(heads):
        lo = hd * dv
        y = y_ref[:, lo:lo + dv]
        yn = y * lax.rsqrt(jnp.mean(y * y, axis=-1, keepdims=True) + NORM_EPS) * ng_ref[hd:hd + 1, :]
        z = (_silu(g_ref[:, lo:lo + dv].astype(F32)) * yn).astype(BF16)
        acc = acc + _dot(z, w_ref[lo:lo + dv, :])
    return acc


def _ml_mix(hs_ref, op_ref, ac_ref, skip_ref, w_ref):
    cw = 1024
    acc = jnp.zeros((TM, D_MODEL), F32)
    for j in range(MLSTM_INNER // cw):
        sl = slice(j * cw, (j + 1) * cw)
        y = (jax.nn.sigmoid(op_ref[:, sl].astype(F32)) * hs_ref[:, sl].astype(F32)
             + skip_ref[:, sl] * ac_ref[:, sl].astype(F32))
        acc = acc + _dot(y.astype(BF16), w_ref[sl, :])
    return acc


def _mix_ffn_kernel(*refs, n_mix, mix, final_norm):
    mix_refs = refs[:n_mix]
    (wo_ref, x_ref, gate1_ref, g_ref, sh_ref, sc_ref, gate2_ref, fg_ref, w1_ref, w2_ref, o_ref) = refs[n_mix:]
    x = x_ref[...] + gate1_ref[0] * mix(*mix_refs, wo_ref)
    xn = _modnorm(x, g_ref[...], sh_ref[0], sc_ref[0]).astype(BF16)
    fc = 1024
    acc = jnp.zeros((TM, D_MODEL), F32)
    for j in range(D_FF // fc):
        hid = _dot(xn, w1_ref[:, j * fc:(j + 1) * fc])
        hid = jnp.square(jnp.maximum(hid, 0.0)).astype(BF16)
        acc = acc + _dot(hid, w2_ref[j * fc:(j + 1) * fc, :])
    o = x + gate2_ref[0] * acc
    if final_norm:
        o = o * lax.rsqrt(jnp.mean(o * o, axis=-1, keepdims=True) + NORM_EPS) * fg_ref[...]
    o_ref[...] = o


def _mix_ffn(mix, mix_rows, mix_consts, w_out, h, n_batch, n_t, off, gate1, gain, shift, scale, gate2,
             final_g, w1, w2, *, final_norm):
    n_out = n_t - off
    ms = _mod_spec(n_batch, off, off == 0)
    in_specs = ([_row_spec(a.shape[1], n_t, off) for a in mix_rows] + [_const_spec(a.shape) for a in mix_consts]
                + [_const_spec(w_out.shape), _row_spec(D_MODEL, n_t, off), ms, _const_spec((1, D_MODEL)), ms, ms, ms,
                   _const_spec((1, D_MODEL)), _const_spec((D_MODEL, D_FF)), _const_spec((D_FF, D_MODEL))])
    return pl.pallas_call(
        functools.partial(_mix_ffn_kernel, n_mix=len(mix_rows) + len(mix_consts), mix=mix, final_norm=final_norm),
        grid=(n_batch, n_out),
        in_specs=in_specs,
        out_specs=_row_spec(D_MODEL, n_out, 0),
        out_shape=jax.ShapeDtypeStruct((n_batch * n_out * TM, D_MODEL), F32),
        compiler_params=_params(2),
        name="mix_ffn",
    )(*mix_rows, *mix_consts, w_out, h, gate1, gain, shift, scale, gate2, final_g, w1, w2)


def _ret_proj_kernel(x_ref, g_ref, sh_ref, sc_ref, cos_ref, sin_ref, w_ref, q_ref, k_ref, v_ref, gg_ref):
    xn = _modnorm(x_ref[...], g_ref[...], sh_ref[0], sc_ref[0]).astype(BF16)
    cos, sin = cos_ref[...], sin_ref[...]
    half = RET_DK // 2
    qk_w = RET_HEADS * RET_DK

    def rope(u):
        u1, u2 = u[:, :half], u[:, half:]
        return jnp.concatenate([u1 * cos - u2 * sin, u1 * sin + u2 * cos], axis=-1)

    for hd in range(RET_HEADS):
        lo = hd * RET_DK
        q_ref[:, lo:lo + RET_DK] = rope(_dot(xn, w_ref[:, lo:lo + RET_DK])).astype(BF16)
        u = _dot(xn, w_ref[:, qk_w + lo:qk_w + lo + RET_DK]) * RET_DK ** -0.5
        k_ref[:, lo:lo + RET_DK] = rope(u).astype(BF16)
    for hd in range(RET_HEADS):
        lo = hd * RET_DV
        v_ref[:, lo:lo + RET_DV] = _dot(xn, w_ref[:, 2 * qk_w + lo:2 * qk_w + lo + RET_DV]).astype(BF16)
        base = 2 * qk_w + RET_HEADS * RET_DV
        gg_ref[:, lo:lo + RET_DV] = _dot(xn, w_ref[:, base + lo:base + lo + RET_DV]).astype(BF16)


def _ret_proj(h, n_batch, n_t, gain, shift, scale, cos, sin, w_in):
    ms = _mod_spec(n_batch, 0, True)
    rows = h.shape[0]
    qk_w, v_w = RET_HEADS * RET_DK, RET_HEADS * RET_DV
    tab = pl.BlockSpec((TM, RET_DK // 2), lambda b, t: (t, 0))
    return pl.pallas_call(
        _ret_proj_kernel,
        grid=(n_batch, n_t),
        in_specs=[_row_spec(D_MODEL, n_t, 0), _const_spec((1, D_MODEL)), ms, ms, tab, tab,
                  _const_spec(w_in.shape)],
        out_specs=[_row_spec(qk_w, n_t, 0), _row_spec(qk_w, n_t, 0), _row_spec(v_w, n_t, 0), _row_spec(v_w, n_t, 0)],
        out_shape=[jax.ShapeDtypeStruct((rows, qk_w), BF16), jax.ShapeDtypeStruct((rows, qk_w), BF16),
                   jax.ShapeDtypeStruct((rows, v_w), BF16), jax.ShapeDtypeStruct((rows, v_w), BF16)],
        compiler_params=_params(2),
        name="ret_proj",
    )(h, gain, shift, scale, cos, sin, w_in)


def _block_starts(step, n_blocks):
    return step, jnp.where(step == 0, 0, n_blocks - step)


def _ret_scan_kernel(ld_ref, q_ref, k_ref, v_ref, o_ref, sf_ref, sb_ref, dmf_ref, dmb_ref, *, chunk, seq):
    L = chunk
    hd = pl.program_id(1)
    n_blocks = seq // L
    li = lax.broadcasted_iota(jnp.int32, (L, L), 0)
    si = lax.broadcasted_iota(jnp.int32, (L, L), 1)
    row = lax.broadcasted_iota(jnp.int32, (L, 1), 0).astype(F32)
    consts = []
    for direction, dm_ref in ((0, dmf_ref), (1, dmb_ref)):
        lg = -jnp.exp(jnp.full((1, 1), ld_ref[direction, hd], F32))
        if direction == 0:
            dist = li - si
            q_dec = jnp.exp((row + 1.0) * lg)
            k_dec = jnp.exp((L - 1.0 - row) * lg)
        else:
            dist = si - li
            q_dec = jnp.exp((L - row) * lg)
            k_dec = jnp.exp(row * lg)
        dm_ref[...] = jnp.where(dist >= 0, jnp.exp(jnp.maximum(dist, 0).astype(F32) * lg), 0.0)
        consts.append((q_dec, k_dec, jnp.exp(float(L) * lg)))
    sf_ref[...] = jnp.zeros_like(sf_ref)
    sb_ref[...] = jnp.zeros_like(sb_ref)
    o_ref[...] = jnp.zeros_like(o_ref)

    def chunk_step(r0, s_ref, dm_ref, q_dec, k_dec, s_dec):
        qc = q_ref[pl.ds(r0, L), :]
        kc = k_ref[pl.ds(r0, L), :]
        vc = v_ref[pl.ds(r0, L), :]
        att = lax.dot_general(qc, kc, _NT, preferred_element_type=F32) * dm_ref[...]
        state = s_ref[...]
        out = _dot(att.astype(BF16), vc) + q_dec * _dot(qc, state.astype(BF16))
        kd = (kc.astype(F32) * k_dec).astype(BF16)
        s_ref[...] = s_dec * state + lax.dot_general(kd, vc, _TN, preferred_element_type=F32)
        o_ref[pl.ds(r0, L), :] += out

    def body(step, carry):
        blk_f, blk_b = _block_starts(step, n_blocks)
        chunk_step(pl.multiple_of(blk_f * L, L), sf_ref, dmf_ref, *consts[0])
        chunk_step(pl.multiple_of(blk_b * L, L), sb_ref, dmb_ref, *consts[1])
        return carry

    lax.fori_loop(0, n_blocks, body, 0)


def _ret_scan(log_decay, q, k, v, n_batch, seq):
    assert RET_CHUNK == CTX_LEN
    return pl.pallas_call(
        functools.partial(_ret_scan_kernel, chunk=RET_CHUNK, seq=seq),
        grid=(n_batch, RET_HEADS),
        in_specs=[pl.BlockSpec(memory_space=pltpu.SMEM),
                  pl.BlockSpec((seq, RET_DK), lambda b, h: (b, h)),
                  pl.BlockSpec((seq, RET_DK), lambda b, h: (b, h)),
                  pl.BlockSpec((seq, RET_DV), lambda b, h: (b, h))],
        out_specs=pl.BlockSpec((seq, RET_DV), lambda b, h: (b, h)),
        out_shape=jax.ShapeDtypeStruct((n_batch * seq, RET_HEADS * RET_DV), F32),
        scratch_shapes=[pltpu.VMEM((RET_DK, RET_DV), F32), pltpu.VMEM((RET_DK, RET_DV), F32),
                        pltpu.VMEM((RET_CHUNK, RET_CHUNK), F32), pltpu.VMEM((RET_CHUNK, RET_CHUNK), F32)],
        compiler_params=_params(2),
        name="ret_scan",
    )(log_decay, q, k, v)


def _gla_proj_kernel(x_ref, g_ref, sh_ref, sc_ref, w_ref, wa1_ref, wa2_ref, ba_ref,
                     q_ref, k_ref, v_ref, r_ref, ga_ref):
    xn = _modnorm(x_ref[...], g_ref[...], sh_ref[0], sc_ref[0]).astype(BF16)
    kw, vw = GLA_HEADS * GLA_DK, GLA_HEADS * GLA_DV
    q_ref[...] = (_dot(xn, w_ref[:, :kw]) * GLA_DK ** -0.5).astype(BF16)
    k_ref[...] = _dot(xn, w_ref[:, kw:2 * kw]).astype(BF16)
    v_ref[...] = _dot(xn, w_ref[:, 2 * kw:2 * kw + vw]).astype(BF16)
    r_ref[...] = _dot(xn, w_ref[:, 2 * kw + vw:]).astype(BF16)
    low = _dot(xn, wa1_ref[...]).astype(BF16)
    z = _dot(low, wa2_ref[...]) + ba_ref[...]
    ga_ref[...] = _log_sigmoid(z) / GLA_TAU


def _gla_proj(h, n_batch, n_t, gain, shift, scale, w_in, wa1, wa2, ba):
    ms = _mod_spec(n_batch, 0, True)
    rows = h.shape[0]
    kw, vw = GLA_HEADS * GLA_DK, GLA_HEADS * GLA_DV
    return pl.pallas_call(
        _gla_proj_kernel,
        grid=(n_batch, n_t),
        in_specs=[_row_spec(D_MODEL, n_t, 0), _const_spec((1, D_MODEL)), ms, ms, _const_spec(w_in.shape),
                  _const_spec(wa1.shape), _const_spec(wa2.shape), _const_spec(ba.shape)],
        out_specs=[_row_spec(kw, n_t, 0), _row_spec(kw, n_t, 0), _row_spec(vw, n_t, 0), _row_spec(vw, n_t, 0),
                   _row_spec(2 * kw, n_t, 0)],
        out_shape=[jax.ShapeDtypeStruct((rows, kw), BF16), jax.ShapeDtypeStruct((rows, kw), BF16),
                   jax.ShapeDtypeStruct((rows, vw), BF16), jax.ShapeDtypeStruct((rows, vw), BF16),
                   jax.ShapeDtypeStruct((rows, 2 * kw), F32)],
        compiler_params=_params(2),
        name="gla_proj",
    )(h, gain, shift, scale, w_in, wa1, wa2, ba)


def _gla_scan_kernel(q_ref, k_ref, v_ref, gf_ref, gb_ref, o_ref, sf_ref, sb_ref, trif_ref, trib_ref,
                     *, sub, block, seq):
    C, blk = sub, block
    n_sub, n_blocks = blk // C, seq // blk
    li = lax.broadcasted_iota(jnp.int32, (blk, blk), 0)
    si = lax.broadcasted_iota(jnp.int32, (blk, blk), 1)
    trif_ref[...] = (li >= si).astype(BF16)
    trib_ref[...] = (si >= li).astype(BF16)
    dl = lax.broadcasted_iota(jnp.int32, (C, C), 0)
    ds = lax.broadcasted_iota(jnp.int32, (C, C), 1)
    diag_masks = (dl >= ds, ds >= dl)
    eye = (lax.broadcasted_iota(jnp.int32, (GLA_DK, GLA_DK), 0)
           == lax.broadcasted_iota(jnp.int32, (GLA_DK, GLA_DK), 1))
    sf_ref[...] = jnp.zeros_like(sf_ref)
    sb_ref[...] = jnp.zeros_like(sb_ref)
    o_ref[...] = jnp.zeros_like(o_ref)

    def block_step(direction, r0):
        fwd = direction == 0
        gate_ref, tri_ref, s_ref = (gf_ref, trif_ref, sf_ref) if fwd else (gb_ref, trib_ref, sb_ref)
        g = gate_ref[pl.ds(r0, blk), :]
        g_hi = g.astype(BF16)
        rem = g - g_hi.astype(F32)
        g_mid = rem.astype(BF16)
        g_lo = (rem - g_mid.astype(F32)).astype(BF16)
        tri = tri_ref[...]
        cum = _dot(tri, g_hi) + _dot(tri, g_mid) + _dot(tri, g_lo)
        qf = q_ref[pl.ds(r0, blk), :].astype(F32)
        kf = k_ref[pl.ds(r0, blk), :].astype(F32)
        vb = v_ref[pl.ds(r0, blk), :]
        end_row = blk - 1 if fwd else 0
        c_end = cum[end_row:end_row + 1]
        state = s_ref[...]
        inter = _dot((qf * jnp.exp(cum)).astype(BF16), state.astype(BF16))
        kd = (kf * jnp.exp(c_end - cum)).astype(BF16)
        e_col = jnp.sum(jnp.where(eye, jnp.exp(c_end), 0.0), axis=1, keepdims=True)
        s_ref[...] = e_col * state + lax.dot_general(kd, vb, _TN, preferred_element_type=F32)
        outs = {}
        for i in range(n_sub):
            lo = i * C if fwd else blk - (i + 1) * C
            rows = slice(lo, lo + C)
            mid = lo + (C // 2 if fwd else C - 1 - C // 2)
            cq = cum[rows]
            c_mid = cum[mid:mid + 1]
            qd = (qf[rows] * jnp.exp(cq - c_mid)).astype(BF16)
            kg = (kf[rows] * jnp.exp(c_mid - cq)).astype(BF16)
            att = jnp.where(diag_masks[direction], lax.dot_general(qd, kg, _NT, preferred_element_type=F32), 0.0)
            o_i = _dot(att.astype(BF16), vb[rows])
            if i > 0:
                prev = slice(0, lo) if fwd else slice(lo + C, blk)
                edge = lo - 1 if fwd else lo + C
                c_edge = cum[edge:edge + 1]
                ql = (qf[rows] * jnp.exp(cq - c_edge)).astype(BF16)
                kk = (kf[prev] * jnp.exp(c_edge - cum[prev])).astype(BF16)
                att = lax.dot_general(ql, kk, _NT, preferred_element_type=F32)
                o_i = o_i + _dot(att.astype(BF16), vb[prev])
            outs[lo] = o_i
        intra = jnp.concatenate([outs[lo] for lo in sorted(outs)], axis=0)
        o_ref[pl.ds(r0, blk), :] += intra + inter

    def body(step, carry):
        blk_f, blk_b = _block_starts(step, n_blocks)
        block_step(0, pl.multiple_of(blk_f * blk, blk))
        block_step(1, pl.multiple_of(blk_b * blk, blk))
        return carry

    lax.fori_loop(0, n_blocks, body, 0)


def _gla_scan(q, k, v, ga, n_batch, seq):
    return pl.pallas_call(
        functools.partial(_gla_scan_kernel, sub=GLA_CHUNK, block=CTX_LEN, seq=seq),
        grid=(n_batch, GLA_HEADS),
        in_specs=[pl.BlockSpec((seq, GLA_DK), lambda b, h: (b, h)),
                  pl.BlockSpec((seq, GLA_DK), lambda b, h: (b, h)),
                  pl.BlockSpec((seq, GLA_DV), lambda b, h: (b, h)),
                  pl.BlockSpec((seq, GLA_DK), lambda b, h: (b, h)),
                  pl.BlockSpec((seq, GLA_DK), lambda b, h: (b, GLA_HEADS + h))],
        out_specs=pl.BlockSpec((seq, GLA_DV), lambda b, h: (b, h)),
        out_shape=jax.ShapeDtypeStruct((n_batch * seq, GLA_HEADS * GLA_DV), F32),
        scratch_shapes=[pltpu.VMEM((GLA_DK, GLA_DV), F32), pltpu.VMEM((GLA_DK, GLA_DV), F32),
                        pltpu.VMEM((CTX_LEN, CTX_LEN), BF16), pltpu.VMEM((CTX_LEN, CTX_LEN), BF16)],
        compiler_params=_params(2),
        name="gla_scan",
    )(q, k, v, ga, ga)


def _ml_proj_kernel(x_ref, g_ref, sh_ref, sc_ref, w_ref, a_ref, o_ref):
    xn = _modnorm(x_ref[...], g_ref[...], sh_ref[0], sc_ref[0]).astype(BF16)
    cw = 1024
    for j in range(MLSTM_INNER // cw):
        a_ref[:, j * cw:(j + 1) * cw] = _dot(xn, w_ref[:, j * cw:(j + 1) * cw]).astype(BF16)
        o_ref[:, j * cw:(j + 1) * cw] = _dot(
            xn, w_ref[:, MLSTM_INNER + j * cw:MLSTM_INNER + (j + 1) * cw]).astype(BF16)


def _ml_proj(h, n_batch, n_t, gain, shift, scale, w_in):
    ms = _mod_spec(n_batch, 0, True)
    rows = h.shape[0]
    return pl.pallas_call(
        _ml_proj_kernel,
        grid=(n_batch, n_t),
        in_specs=[_row_spec(D_MODEL, n_t, 0), _const_spec((1, D_MODEL)), ms, ms, _const_spec(w_in.shape)],
        out_specs=[_row_spec(MLSTM_INNER, n_t, 0), _row_spec(MLSTM_INNER, n_t, 0)],
        out_shape=[jax.ShapeDtypeStruct((rows, MLSTM_INNER), BF16)] * 2,
        compiler_params=_params(2),
        name="ml_proj",
    )(h, gain, shift, scale, w_in)


def _ml_qkv_kernel(a_ref, prev_ref, next_ref, cw_ref, cb_ref, wq_ref, wkt_ref, wv_ref, wif_ref, wifkt_ref, bif_ref,
                   ac_ref, q_ref, kt_ref, v_ref, gc_ref, gr_ref, *, n_t):
    t = pl.program_id(1)
    has_prev = jnp.where(t >= 2, 1.0, 0.0).astype(F32)
    has_next = jnp.where(jnp.logical_and(t >= 1, t <= n_t - 2), 1.0, 0.0).astype(F32)
    row = lax.broadcasted_iota(jnp.int32, (TM, 1), 0)
    dh = MLSTM_DH
    pre = jnp.zeros((TM, LANES), F32)
    pre_t = jnp.zeros((LANES, TM), F32)
    for hd in range(MLSTM_HEADS):
        lo = hd * dh
        a_bf = a_ref[:, lo:lo + dh]
        a = a_bf.astype(F32)
        before = prev_ref[:, lo:lo + dh].astype(F32)[BF16_ROWS - 1:BF16_ROWS] * has_prev
        after = next_ref[:, lo:lo + dh].astype(F32)[0:1] * has_next
        up = jnp.where(row == 0, before, pltpu.roll(a, 1, axis=0))
        dn = jnp.where(row == TM - 1, after, pltpu.roll(a, TM - 1, axis=0))
        conv = (cw_ref[0:1, lo:lo + dh] * up + cw_ref[1:2, lo:lo + dh] * a + cw_ref[2:3, lo:lo + dh] * dn
                + cb_ref[:, lo:lo + dh])
        ac = _silu(conv).astype(BF16)
        ac_ref[:, lo:lo + dh] = ac
        q = _dot(ac, wq_ref[hd]).astype(BF16)
        kt = (lax.dot_general(wkt_ref[hd], ac, _NT, preferred_element_type=F32) * dh ** -0.5).astype(BF16)
        v = _dot(a_bf, wv_ref[hd]).astype(BF16)
        q_ref[:, lo:lo + dh] = q
        kt_ref[lo:lo + dh, :] = kt
        v_ref[:, lo:lo + dh] = v
        pre = pre + _dot(q, wif_ref[0, hd]) + _dot(v, wif_ref[1, hd])
        pre_t = pre_t + _dot(wifkt_ref[hd], kt)
    pre = pre + pre_t.T + bif_ref[...]
    lane = lax.broadcasted_iota(jnp.int32, (TM, LANES), 1)
    is_forget = (lane % (2 * MLSTM_HEADS)) >= MLSTM_HEADS
    gates = jnp.where(is_forget, _log_sigmoid(pre), pre)
    n_g = 4 * MLSTM_HEADS
    gc_ref[...] = gates[:, :n_g]
    gr_ref[...] = gates.T[:n_g, :]


def _ml_qkv(a, n_batch, n_t, conv_w, conv_b, wq, wkt, wv, wif, wifkt, bif):
    rows = a.shape[0]
    per_tile = TM // BF16_ROWS
    n_halo = rows // BF16_ROWS
    n_g = 4 * MLSTM_HEADS
    wide = _row_spec(MLSTM_INNER, n_t, 0)
    prev_spec = pl.BlockSpec((BF16_ROWS, MLSTM_INNER),
                             lambda b, t: (jnp.maximum((b * n_t + t) * per_tile - 1, 0), 0))
    next_spec = pl.BlockSpec((BF16_ROWS, MLSTM_INNER),
                             lambda b, t: (jnp.minimum((b * n_t + t + 1) * per_tile, n_halo - 1), 0))
    return pl.pallas_call(
        functools.partial(_ml_qkv_kernel, n_t=n_t),
        grid=(n_batch, n_t),
        in_specs=[wide, prev_spec, next_spec, _const_spec(conv_w.shape), _const_spec(conv_b.shape),
                  _const_spec(wq.shape), _const_spec(wkt.shape), _const_spec(wv.shape),
                  _const_spec(wif.shape), _const_spec(wifkt.shape), _const_spec(bif.shape)],
        out_specs=[wide, wide, pl.BlockSpec((MLSTM_INNER, TM), lambda b, t: (0, b * n_t + t)), wide,
                   _row_spec(n_g, n_t, 0), pl.BlockSpec((n_g, TM), lambda b, t: (0, b * n_t + t))],
        out_shape=[jax.ShapeDtypeStruct((rows, MLSTM_INNER), BF16)] * 2
        + [jax.ShapeDtypeStruct((MLSTM_INNER, rows), BF16), jax.ShapeDtypeStruct((rows, MLSTM_INNER), BF16),
           jax.ShapeDtypeStruct((rows, n_g), F32), jax.ShapeDtypeStruct((n_g, rows), F32)],
        compiler_params=_params(2),
        name="ml_qkv",
    )(a, a, a, conv_w, conv_b, wq, wkt, wv, wif, wifkt, bif)


def _ml_scan_kernel(q_ref, kt_ref, v_ref, gc_ref, gr_ref, o_ref, ct_ref, *, chunk, seq):
    L = chunk
    dv = v_ref.shape[1]
    n_blocks = seq // L
    li = lax.broadcasted_iota(jnp.int32, (L, L), 0)
    si = lax.broadcasted_iota(jnp.int32, (L, L), 1)
    ones = jnp.ones((L, LANES), BF16)
    for direction in (0, 1):
        col_i, col_f = (0, 1) if direction == 0 else (2, 3)
        seen = (li >= si) if direction == 0 else (si >= li)
        seen_t = (li <= si) if direction == 0 else (si <= li)
        end_row = L - 1 if direction == 0 else 0
        ct_ref[...] = jnp.zeros_like(ct_ref)

        def body(step, m_prev, direction=direction, col_i=col_i, col_f=col_f, seen=seen, seen_t=seen_t,
                 end_row=end_row):
            blk = _block_starts(step, n_blocks)[direction]
            r0 = pl.multiple_of(blk * L, L)
            qc = q_ref[pl.ds(r0, L), :]
            ktc = kt_ref[:, pl.ds(r0, L)]
            vc = v_ref[pl.ds(r0, L), :]
            f_col = gc_ref[pl.ds(r0, L), col_f:col_f + 1]
            f_row = gr_ref[col_f:col_f + 1, pl.ds(r0, L)]
            i_row = gr_ref[col_i:col_i + 1, pl.ds(r0, L)]
            b_col = jnp.sum(jnp.where(seen, f_row, 0.0), axis=1, keepdims=True)
            b_row = jnp.sum(jnp.where(seen_t, f_col, 0.0), axis=0, keepdims=True)
            d_in = jnp.where(seen, b_col - b_row + i_row, -jnp.inf)
            g = b_col + m_prev
            m_t = jnp.maximum(g, jnp.max(d_in, axis=1, keepdims=True))
            w = jnp.exp(d_in - m_t)
            w_prev = jnp.exp(g - m_t)
            s = _dot(qc, ktc) * w
            state = ct_ref[...]
            carried = _dot(qc, state.astype(BF16))
            num = _dot(s.astype(BF16), vc) + w_prev * carried[:, :dv]
            den = jnp.sum(s, axis=1, keepdims=True) + w_prev * carried[:, dv:dv + 1]
            out = num * (1.0 / jnp.maximum(jnp.abs(den), jnp.exp(-m_t)))
            b_end = b_col[end_row:end_row + 1]
            d_end = b_end - b_row + i_row
            m_new = jnp.maximum(b_end + m_prev, jnp.max(d_end, axis=1, keepdims=True))
            a_prev = jnp.exp(b_end + m_prev - m_new)
            ktw = (ktc.astype(F32) * jnp.exp(d_end - m_new)).astype(BF16)
            ct_ref[:, :dv] = a_prev * state[:, :dv] + _dot(ktw, vc)
            ct_ref[:, dv:] = a_prev * state[:, dv:] + _dot(ktw, ones)
            if direction == 0:
                o_ref[pl.ds(r0, L), :] = out.astype(o_ref.dtype)
            else:
                o_ref[pl.ds(r0, L), :] = (o_ref[pl.ds(r0, L), :].astype(F32) + out).astype(o_ref.dtype)
            return m_new

        lax.fori_loop(0, n_blocks, body, jnp.zeros((1, 1), F32))


def _ml_scan(q, kt, v, gc, gr, n_batch, seq):
    assert MLSTM_CHUNK == CTX_LEN
    dh = MLSTM_DH
    blk = pl.BlockSpec((seq, dh), lambda b, h: (b, h))
    return pl.pallas_call(
        functools.partial(_ml_scan_kernel, chunk=MLSTM_CHUNK, seq=seq),
        grid=(n_batch, MLSTM_HEADS),
        in_specs=[blk, pl.BlockSpec((dh, seq), lambda b, h: (h, b)), blk,
                  pl.BlockSpec((None, None, seq, 4), lambda b, h: (b, h, 0, 0)),
                  pl.BlockSpec((None, None, 4, seq), lambda b, h: (b, h, 0, 0))],
        out_specs=blk,
        out_shape=jax.ShapeDtypeStruct((n_batch * seq, MLSTM_INNER), BF16),
        scratch_shapes=[pltpu.VMEM((dh, dh + LANES), F32)],
        compiler_params=_params(2),
        name="ml_scan",
    )(q, kt, v, gc, gr)


def _rope_tables(t_lat):
    n_f = RET_DK // 4
    pos = jnp.arange(t_lat)
    inv = ROPE_BASE ** (-jnp.arange(n_f, dtype=F32) / n_f)
    ang = jnp.concatenate([(pos // GRID_W).astype(F32)[:, None] * inv,
                           (pos % GRID_W).astype(F32)[:, None] * inv], axis=-1)
    cos = jnp.concatenate([jnp.ones((CTX_LEN, 2 * n_f), F32), jnp.cos(ang)], axis=0)
    sin = jnp.concatenate([jnp.zeros((CTX_LEN, 2 * n_f), F32), jnp.sin(ang)], axis=0)
    return cos, sin


def _ret_weights(w_in):
    perm = jnp.concatenate([jnp.arange(0, RET_DK, 2), jnp.arange(1, RET_DK, 2)])
    qk_cols = (jnp.arange(2 * RET_HEADS)[:, None] * RET_DK + perm[None, :]).reshape(-1)
    cols = jnp.concatenate([qk_cols, jnp.arange(2 * RET_HEADS * RET_DK, w_in.shape[1])])
    return w_in[:, cols].astype(BF16)


def kernel(x, c, ctx, c_ctx, ada_w, ada_b, norm1_g, norm2_g, ffn_w1, ffn_w2, final_g, ret_w_in, ret_log_decay_f, ret_log_decay_b, ret_norm_g, ret_w_out, gla_w_in, gla_wa1_f, gla_wa2_f, gla_ba_f, gla_wa1_b, gla_wa2_b, gla_ba_b, gla_norm_g, gla_w_out, ml_w_in, ml_conv_w, ml_conv_b, ml_w_q, ml_w_k, ml_w_v, ml_w_if_f, ml_b_if_f, ml_w_if_b, ml_b_if_b, ml_skip, ml_w_out):
    n_batch, t_lat, _ = x.shape
    assert ctx.shape[1] == CTX_LEN == TM and t_lat % TM == 0
    seq = CTX_LEN + t_lat
    n_t = seq // TM

    h = jnp.concatenate([ctx, x], axis=1).reshape(n_batch * seq, D_MODEL)

    n_rows = n_batch + 1
    pad = -n_rows % 8
    cvec = jnp.concatenate([c, c_ctx[None, :], jnp.zeros((pad, D_MODEL), F32)], axis=0)
    mod = _ada(cvec, ada_w, ada_b)[:, :n_rows]
    mod = mod.reshape(DEPTH, n_rows, 6, 1, D_MODEL).transpose(0, 2, 1, 3, 4)

    cos, sin = _rope_tables(t_lat)
    final_g2 = final_g.reshape(1, D_MODEL)

    for i in range(DEPTH):
        last = i == DEPTH - 1
        off = 1 if last else 0
        kind, j = i % 3, i // 3
        sh1, sc1, g1, sh2, sc2, g2 = (mod[i, p] for p in range(6))
        gain1 = norm1_g[i].reshape(1, D_MODEL)
        gain2 = norm2_g[i].reshape(1, D_MODEL)
        if kind == 0:
            q, k, v, g = _ret_proj(h, n_batch, n_t, gain1, sh1, sc1, cos, sin, _ret_weights(ret_w_in[j]))
            decay = jnp.stack([ret_log_decay_f[j], ret_log_decay_b[j]])
            y = _ret_scan(decay, q, k, v, n_batch, seq)
            mix = functools.partial(_gated_mix, heads=RET_HEADS, dv=RET_DV)
            mix_rows, mix_consts, w_out = [y, g], [ret_norm_g[j]], ret_w_out[j]
        elif kind == 1:
            kw = GLA_HEADS * GLA_DK
            wa1 = jnp.zeros((D_MODEL, LANES), F32)
            wa1 = wa1.at[:, :GLA_RANK].set(gla_wa1_f[j]).at[:, GLA_RANK:2 * GLA_RANK].set(gla_wa1_b[j])
            wa2 = jnp.zeros((LANES, 2 * kw), F32)
            wa2 = wa2.at[:GLA_RANK, :kw].set(gla_wa2_f[j]).at[GLA_RANK:2 * GLA_RANK, kw:].set(gla_wa2_b[j])
            ba = jnp.concatenate([gla_ba_f[j], gla_ba_b[j]]).reshape(1, 2 * kw)
            q, k, v, r, ga = _gla_proj(h, n_batch, n_t, gain1, sh1, sc1, gla_w_in[j].astype(BF16),
                                       wa1.astype(BF16), wa2.astype(BF16), ba)
            y = _gla_scan(q, k, v, ga, n_batch, seq)
            mix = functools.partial(_gated_mix, heads=GLA_HEADS, dv=GLA_DV)
            mix_rows, mix_consts, w_out = [y, r], [gla_norm_g[j]], gla_w_out[j]
        else:
            a, o_pre = _ml_proj(h, n_batch, n_t, gain1, sh1, sc1, ml_w_in[j].astype(BF16))
            n_g = 4 * MLSTM_HEADS
            wif = jnp.concatenate([ml_w_if_f[j], ml_w_if_b[j]], axis=-1)
            wif = jnp.pad(wif, ((0, 0), (0, 0), (0, 0), (0, LANES - n_g))).astype(BF16)
            bif = jnp.pad(jnp.concatenate([ml_b_if_f[j], ml_b_if_b[j]]), (0, LANES - n_g)).reshape(1, LANES)
            ac, q, kt, v, gc, gr = _ml_qkv(a, n_batch, n_t, ml_conv_w[j], ml_conv_b[j].reshape(1, MLSTM_INNER),
                                           ml_w_q[j].astype(BF16), ml_w_k[j].transpose(0, 2, 1).astype(BF16),
                                           ml_w_v[j].astype(BF16), wif[jnp.array([0, 2])],
                                           wif[1].transpose(0, 2, 1), bif)
            gc = gc.reshape(n_batch, seq, 4, MLSTM_HEADS).transpose(0, 3, 1, 2)
            gr = gr.reshape(4, MLSTM_HEADS, n_batch, seq).transpose(2, 1, 0, 3)
            hs = _ml_scan(q, kt, v, gc, gr, n_batch, seq)
            mix = _ml_mix
            mix_rows, mix_consts, w_out = [hs, o_pre, ac], [ml_skip[j].reshape(1, MLSTM_INNER)], ml_w_out[j]
        h = _mix_ffn(mix, mix_rows, mix_consts, w_out.astype(BF16), h, n_batch, n_t, off, g1, gain2, sh2, sc2, g2,
                     final_g2, ffn_w1[i].astype(BF16), ffn_w2[i].astype(BF16), final_norm=last)
    return h.reshape(n_batch, t_lat, D_MODEL)
```

```python
import functools
import itertools

import jax
import jax.numpy as jnp
from jax import lax
from jax.experimental import pallas as pl
from jax.experimental.pallas import tpu as pltpu

F32 = jnp.float32
BF16 = jnp.bfloat16

D_MODEL = 1024
DEPTH = 4
GRID_W = 64
CTX_LEN = 256
NORM_EPS = 1e-6
D_FF = 4 * D_MODEL
RET_HEADS = 4
RET_DK = D_MODEL // RET_HEADS
RET_DV = 2 * D_MODEL // RET_HEADS
ROPE_BASE = 10000.0
GLA_HEADS = 4
GLA_DK = D_MODEL // 2 // GLA_HEADS
GLA_DV = D_MODEL // GLA_HEADS
GLA_RANK = 16
GLA_TAU = 16.0
MLSTM_INNER = 2 * D_MODEL
MLSTM_HEADS = 4
MLSTM_DH = MLSTM_INNER // MLSTM_HEADS

TM = CTX_LEN
LANES = 128
BF16_ROWS = 16
RET_CHUNK = 256
GLA_CHUNK = 64
MLSTM_CHUNK = 256
VMEM_LIMIT = 56 * 1024 * 1024

_NT = (((1,), (1,)), ((), ()))
_TN = (((0,), (0,)), ((), ()))


def _dot(a, b):
    return jnp.dot(a, b, preferred_element_type=F32)


def _silu(x):
    return x * jax.nn.sigmoid(x)


def _log_sigmoid(z):
    return jnp.minimum(z, 0.0) - jnp.log1p(jnp.exp(-jnp.abs(z)))


def _modnorm(x, g, shift, scale):
    y = x * lax.rsqrt(jnp.mean(x * x, axis=-1, keepdims=True) + NORM_EPS) * g
    return y * (1.0 + scale) + shift


def _params(n_axes):
    return pltpu.CompilerParams(dimension_semantics=("arbitrary",) * n_axes,
                                vmem_limit_bytes=VMEM_LIMIT)


def _const_spec(shape):
    nd = len(shape)
    return pl.BlockSpec(shape, lambda *_: (0,) * nd, pipeline_mode=pl.Buffered(1))


def _row_spec(width, n_tiles, off):
    return pl.BlockSpec((TM, width), lambda b, t: (b * n_tiles + t + off, 0))


def _mod_spec(n_batch, off, has_ctx):
    if has_ctx:
        return pl.BlockSpec((1, 1, D_MODEL), lambda b, t: (jnp.where(t + off == 0, n_batch, b), 0, 0))
    return pl.BlockSpec((1, 1, D_MODEL), lambda b, t: (b, 0, 0))


def _ada_kernel(c_ref, w_ref, b_ref, o_ref):
    s = _silu(c_ref[...]).astype(BF16)
    o_ref[0] = _dot(s, w_ref[0].astype(BF16)) + b_ref[0]


def _ada(cvec, ada_w, ada_b):
    rows = cvec.shape[0]
    tn = 512
    return pl.pallas_call(
        _ada_kernel,
        grid=(DEPTH, 6 * D_MODEL // tn),
        in_specs=[pl.BlockSpec((rows, D_MODEL), lambda l, j: (0, 0)),
                  pl.BlockSpec((1, D_MODEL, tn), lambda l, j: (l, 0, j)),
                  pl.BlockSpec((1, 1, tn), lambda l, j: (l, 0, j))],
        out_specs=pl.BlockSpec((1, rows, tn), lambda l, j: (l, 0, j)),
        out_shape=jax.ShapeDtypeStruct((DEPTH, rows, 6 * D_MODEL), F32),
        compiler_params=_params(2),
        name="adaln",
    )(cvec, ada_w, ada_b.reshape(DEPTH, 1, 6 * D_MODEL))


def _gated_mix(y_ref, g_ref, ng_ref, w_ref, *, heads, dv):
    acc = jnp.zeros((TM, D_MODEL), F32)
    for hd in range(heads):
        lo = hd * dv
        y = y_ref[:, lo:lo + dv]
        yn = y * lax.rsqrt(jnp.mean(y * y, axis=-1, keepdims=True) + NORM_EPS) * ng_ref[hd:hd + 1, :]
        z = (_silu(g_ref[:, lo:lo + dv].astype(F32)) * yn).astype(BF16)
        acc = acc + _dot(z, w_ref[lo:lo + dv, :])
    return acc


def _ml_mix(hs_ref, op_ref, ac_ref, skip_ref, w_ref):
    cw = 1024
    acc = jnp.zeros((TM, D_MODEL), F32)
    for j in range(MLSTM_INNER // cw):
        sl = slice(j * cw, (j + 1) * cw)
        y = (jax.nn.sigmoid(op_ref[:, sl].astype(F32)) * hs_ref[:, sl].astype(F32)
             + skip_ref[:, sl] * ac_ref[:, sl].astype(F32))
        acc = acc + _dot(y.astype(BF16), w_ref[sl, :])
    return acc


def _mix_ffn_kernel(*refs, n_mix, mix, final_norm):
    mix_refs = refs[:n_mix]
    (wo_ref, x_ref, gate1_ref, g_ref, sh_ref, sc_ref, gate2_ref, fg_ref, w1_ref, w2_ref, o_ref) = refs[n_mix:]
    x = x_ref[...] + gate1_ref[0] * mix(*mix_refs, wo_ref)
    xn = _modnorm(x, g_ref[...], sh_ref[0], sc_ref[0]).astype(BF16)
    fc = 1024
    acc = jnp.zeros((TM, D_MODEL), F32)
    for j in range(D_FF // fc):
        hid = _dot(xn, w1_ref[:, j * fc:(j + 1) * fc])
        hid = jnp.square(jnp.maximum(hid, 0.0)).astype(BF16)
        acc = acc + _dot(hid, w2_ref[j * fc:(j + 1) * fc, :])
    o = x + gate2_ref[0] * acc
    if final_norm:
        o = o * lax.rsqrt(jnp.mean(o * o, axis=-1, keepdims=True) + NORM_EPS) * fg_ref[...]
    o_ref[...] = o


def _mix_ffn(mix, mix_rows, mix_consts, w_out, h, n_batch, n_t, off, gate1, gain, shift, scale, gate2,
             final_g, w1, w2, *, final_norm):
    n_out = n_t - off
    ms = _mod_spec(n_batch, off, off == 0)
    in_specs = ([_row_spec(a.shape[1], n_t, off) for a in mix_rows] + [_const_spec(a.shape) for a in mix_consts]
                + [_const_spec(w_out.shape), _row_spec(D_MODEL, n_t, off), ms, _const_spec((1, D_MODEL)), ms, ms, ms,
                   _const_spec((1, D_MODEL)), _const_spec((D_MODEL, D_FF)), _const_spec((D_FF, D_MODEL))])
    return pl.pallas_call(
        functools.partial(_mix_ffn_kernel, n_mix=len(mix_rows) + len(mix_consts), mix=mix, final_norm=final_norm),
        grid=(n_batch, n_out),
        in_specs=in_specs,
        out_specs=_row_spec(D_MODEL, n_out, 0),
        out_shape=jax.ShapeDtypeStruct((n_batch * n_out * TM, D_MODEL), F32),
        compiler_params=_params(2),
        name="mix_ffn",
    )(*mix_rows, *mix_consts, w_out, h, gate1, gain, shift, scale, gate2, final_g, w1, w2)


def _ret_proj_kernel(x_ref, g_ref, sh_ref, sc_ref, cos_ref, sin_ref, cost_ref, sint_ref, w_ref, wkt_ref,
                     q_ref, kt_ref, v_ref, gg_ref):
    xn = _modnorm(x_ref[...], g_ref[...], sh_ref[0], sc_ref[0]).astype(BF16)
    cos, sin = cos_ref[...], sin_ref[...]
    cos_t, sin_t = cost_ref[...], sint_ref[...]
    half = RET_DK // 2
    qk_w, v_w = RET_HEADS * RET_DK, RET_HEADS * RET_DV
    for hd in range(RET_HEADS):
        lo = hd * RET_DK
        u = _dot(xn, w_ref[:, lo:lo + RET_DK])
        u1, u2 = u[:, :half], u[:, half:]
        q_ref[:, lo:lo + RET_DK] = jnp.concatenate([u1 * cos - u2 * sin, u1 * sin + u2 * cos], axis=1).astype(BF16)
        ut = lax.dot_general(wkt_ref[lo:lo + RET_DK, :], xn, _NT, preferred_element_type=F32) * RET_DK ** -0.5
        u1, u2 = ut[:half], ut[half:]
        kt_ref[lo:lo + RET_DK, :] = jnp.concatenate([u1 * cos_t - u2 * sin_t, u1 * sin_t + u2 * cos_t],
                                                    axis=0).astype(BF16)
    for hd in range(RET_HEADS):
        lo = hd * RET_DV
        v_ref[:, lo:lo + RET_DV] = _dot(xn, w_ref[:, qk_w + lo:qk_w + lo + RET_DV]).astype(BF16)
        gg_ref[:, lo:lo + RET_DV] = _dot(xn, w_ref[:, qk_w + v_w + lo:qk_w + v_w + lo + RET_DV]).astype(BF16)


def _ret_proj(h, n_batch, n_t, gain, shift, scale, cos, sin, w_qvg, wkt):
    ms = _mod_spec(n_batch, 0, True)
    rows = h.shape[0]
    qk_w, v_w = RET_HEADS * RET_DK, RET_HEADS * RET_DV
    tab = pl.BlockSpec((TM, RET_DK // 2), lambda b, t: (t, 0))
    tab_t = pl.BlockSpec((RET_DK // 2, TM), lambda b, t: (0, t))
    return pl.pallas_call(
        _ret_proj_kernel,
        grid=(n_batch, n_t),
        in_specs=[_row_spec(D_MODEL, n_t, 0), _const_spec((1, D_MODEL)), ms, ms, tab, tab, tab_t, tab_t,
                  _const_spec(w_qvg.shape), _const_spec(wkt.shape)],
        out_specs=[_row_spec(qk_w, n_t, 0), pl.BlockSpec((qk_w, TM), lambda b, t: (0, b * n_t + t)),
                   _row_spec(v_w, n_t, 0), _row_spec(v_w, n_t, 0)],
        out_shape=[jax.ShapeDtypeStruct((rows, qk_w), BF16), jax.ShapeDtypeStruct((qk_w, rows), BF16),
                   jax.ShapeDtypeStruct((rows, v_w), BF16), jax.ShapeDtypeStruct((rows, v_w), BF16)],
        compiler_params=_params(2),
        name="ret_proj",
    )(h, gain, shift, scale, cos, sin, cos.T, sin.T, w_qvg, wkt)


def _block_starts(step, n_blocks):
    return step, jnp.where(step == 0, 0, n_blocks - step)


def _ret_scan_kernel(ld_ref, q_ref, kt_ref, v_ref, o_ref, sf_ref, sb_ref, dmf_ref, dmb_ref, *, chunk, seq):
    L = chunk
    hd = pl.program_id(1)
    n_blocks = seq // L
    li = lax.broadcasted_iota(jnp.int32, (L, L), 0)
    si = lax.broadcasted_iota(jnp.int32, (L, L), 1)
    row = lax.broadcasted_iota(jnp.int32, (L, 1), 0).astype(F32)
    col = lax.broadcasted_iota(jnp.int32, (1, L), 1).astype(F32)
    consts = []
    for direction, dm_ref in ((0, dmf_ref), (1, dmb_ref)):
        lg = -jnp.exp(jnp.full((1, 1), ld_ref[direction, hd], F32))
        if direction == 0:
            dist = li - si
            q_dec = jnp.exp((row + 1.0) * lg)
            k_dec = jnp.exp((L - 1.0 - col) * lg)
        else:
            dist = si - li
            q_dec = jnp.exp((L - row) * lg)
            k_dec = jnp.exp(col * lg)
        dm_ref[...] = jnp.where(dist >= 0, jnp.exp(jnp.maximum(dist, 0).astype(F32) * lg), 0.0)
        consts.append((q_dec, k_dec, jnp.exp(float(L) * lg)))
    sf_ref[...] = jnp.zeros_like(sf_ref)
    sb_ref[...] = jnp.zeros_like(sb_ref)
    o_ref[...] = jnp.zeros_like(o_ref)

    def chunk_step(r0, s_ref, dm_ref, q_dec, k_dec, s_dec):
        qc = q_ref[pl.ds(r0, L), :]
        ktc = kt_ref[:, pl.ds(r0, L)]
        vc = v_ref[pl.ds(r0, L), :]
        state = s_ref[...]
        carried = _dot(qc, state.astype(BF16))
        yield
        att = _dot(qc, ktc) * dm_ref[...]
        yield
        kd = (ktc.astype(F32) * k_dec).astype(BF16)
        s_ref[...] = s_dec * state + _dot(kd, vc)
        yield
        o_ref[pl.ds(r0, L), :] += _dot(att.astype(BF16), vc) + q_dec * carried

    def body(step, carry):
        blk_f, blk_b = _block_starts(step, n_blocks)
        for _ in itertools.zip_longest(
                chunk_step(pl.multiple_of(blk_f * L, L), sf_ref, dmf_ref, *consts[0]),
                chunk_step(pl.multiple_of(blk_b * L, L), sb_ref, dmb_ref, *consts[1])):
            pass
        return carry

    lax.fori_loop(0, n_blocks, body, 0)


def _ret_scan(log_decay, q, kt, v, n_batch, seq):
    assert RET_CHUNK == CTX_LEN
    return pl.pallas_call(
        functools.partial(_ret_scan_kernel, chunk=RET_CHUNK, seq=seq),
        grid=(n_batch, RET_HEADS),
        in_specs=[pl.BlockSpec(memory_space=pltpu.SMEM),
                  pl.BlockSpec((seq, RET_DK), lambda b, h: (b, h)),
                  pl.BlockSpec((RET_DK, seq), lambda b, h: (h, b)),
                  pl.BlockSpec((seq, RET_DV), lambda b, h: (b, h))],
        out_specs=pl.BlockSpec((seq, RET_DV), lambda b, h: (b, h)),
        out_shape=jax.ShapeDtypeStruct((n_batch * seq, RET_HEADS * RET_DV), F32),
        scratch_shapes=[pltpu.VMEM((RET_DK, RET_DV), F32), pltpu.VMEM((RET_DK, RET_DV), F32),
                        pltpu.VMEM((RET_CHUNK, RET_CHUNK), F32), pltpu.VMEM((RET_CHUNK, RET_CHUNK), F32)],
        compiler_params=_params(2),
        name="ret_scan",
    )(log_decay, q, kt, v)


def _gla_proj_kernel(x_ref, g_ref, sh_ref, sc_ref, w_ref, wa1_ref, wa2_ref, ba_ref,
                     q_ref, k_ref, v_ref, r_ref, ga_ref):
    xn = _modnorm(x_ref[...], g_ref[...], sh_ref[0], sc_ref[0]).astype(BF16)
    kw, vw = GLA_HEADS * GLA_DK, GLA_HEADS * GLA_DV
    q_ref[...] = (_dot(xn, w_ref[:, :kw]) * GLA_DK ** -0.5).astype(BF16)
    k_ref[...] = _dot(xn, w_ref[:, kw:2 * kw]).astype(BF16)
    v_ref[...] = _dot(xn, w_ref[:, 2 * kw:2 * kw + vw]).astype(BF16)
    r_ref[...] = _dot(xn, w_ref[:, 2 * kw + vw:]).astype(BF16)
    low = _dot(xn, wa1_ref[...]).astype(BF16)
    z = _dot(low, wa2_ref[...]) + ba_ref[...]
    ga_ref[...] = _log_sigmoid(z) / GLA_TAU


def _gla_proj(h, n_batch, n_t, gain, shift, scale, w_in, wa1, wa2, ba):
    ms = _mod_spec(n_batch, 0, True)
    rows = h.shape[0]
    kw, vw = GLA_HEADS * GLA_DK, GLA_HEADS * GLA_DV
    return pl.pallas_call(
        _gla_proj_kernel,
        grid=(n_batch, n_t),
        in_specs=[_row_spec(D_MODEL, n_t, 0), _const_spec((1, D_MODEL)), ms, ms, _const_spec(w_in.shape),
                  _const_spec(wa1.shape), _const_spec(wa2.shape), _const_spec(ba.shape)],
        out_specs=[_row_spec(kw, n_t, 0), _row_spec(kw, n_t, 0), _row_spec(vw, n_t, 0), _row_spec(vw, n_t, 0),
                   _row_spec(2 * kw, n_t, 0)],
        out_shape=[jax.ShapeDtypeStruct((rows, kw), BF16), jax.ShapeDtypeStruct((rows, kw), BF16),
                   jax.ShapeDtypeStruct((rows, vw), BF16), jax.ShapeDtypeStruct((rows, vw), BF16),
                   jax.ShapeDtypeStruct((rows, 2 * kw), F32)],
        compiler_params=_params(2),
        name="gla_proj",
    )(h, gain, shift, scale, w_in, wa1, wa2, ba)


def _gla_scan_kernel(q_ref, k_ref, v_ref, gf_ref, gb_ref, o_ref, sf_ref, sb_ref, trif_ref, trib_ref,
                     *, sub, block, seq):
    C, blk = sub, block
    n_sub, n_blocks = blk // C, seq // blk
    li = lax.broadcasted_iota(jnp.int32, (blk, blk), 0)
    si = lax.broadcasted_iota(jnp.int32, (blk, blk), 1)
    trif_ref[...] = (li >= si).astype(BF16)
    trib_ref[...] = (si >= li).astype(BF16)
    dl = lax.broadcasted_iota(jnp.int32, (C, C), 0)
    ds = lax.broadcasted_iota(jnp.int32, (C, C), 1)
    diag_masks = (dl >= ds, ds >= dl)
    eye = (lax.broadcasted_iota(jnp.int32, (GLA_DK, GLA_DK), 0)
           == lax.broadcasted_iota(jnp.int32, (GLA_DK, GLA_DK), 1))
    sf_ref[...] = jnp.zeros_like(sf_ref)
    sb_ref[...] = jnp.zeros_like(sb_ref)
    o_ref[...] = jnp.zeros_like(o_ref)

    def block_step(direction, r0, result):
        fwd = direction == 0
        gate_ref, tri_ref, s_ref = (gf_ref, trif_ref, sf_ref) if fwd else (gb_ref, trib_ref, sb_ref)
        g = gate_ref[pl.ds(r0, blk), :]
        g_hi = g.astype(BF16)
        rem = g - g_hi.astype(F32)
        g_mid = rem.astype(BF16)
        g_lo = (rem - g_mid.astype(F32)).astype(BF16)
        parts = _dot(tri_ref[...], jnp.concatenate([g_hi, g_mid, g_lo], axis=1))
        yield
        cum = parts[:, :GLA_DK] + parts[:, GLA_DK:2 * GLA_DK] + parts[:, 2 * GLA_DK:]
        qf = q_ref[pl.ds(r0, blk), :].astype(F32)
        kf = k_ref[pl.ds(r0, blk), :].astype(F32)
        vb = v_ref[pl.ds(r0, blk), :]
        end_row = blk - 1 if fwd else 0
        c_end = cum[end_row:end_row + 1]
        state = s_ref[...]
        q_carry = (qf * jnp.exp(cum)).astype(BF16)
        kd = (kf * jnp.exp(c_end - cum)).astype(BF16)
        e_col = jnp.sum(jnp.where(eye, jnp.exp(c_end), 0.0), axis=1, keepdims=True)
        new_state = e_col * state + lax.dot_general(kd, vb, _TN, preferred_element_type=F32)
        att_rows = {}
        for i in range(n_sub):
            yield
            lo = i * C if fwd else blk - (i + 1) * C
            rows = slice(lo, lo + C)
            mid = lo + (C // 2 if fwd else C - 1 - C // 2)
            cq = cum[rows]
            c_mid = cum[mid:mid + 1]
            qd = (qf[rows] * jnp.exp(cq - c_mid)).astype(BF16)
            kg = (kf[rows] * jnp.exp(c_mid - cq)).astype(BF16)
            diag = jnp.where(diag_masks[direction], lax.dot_general(qd, kg, _NT, preferred_element_type=F32), 0.0)
            earlier = None
            if i > 0:
                prev = slice(0, lo) if fwd else slice(lo + C, blk)
                edge = lo - 1 if fwd else lo + C
                c_edge = cum[edge:edge + 1]
                ql = (qf[rows] * jnp.exp(cq - c_edge)).astype(BF16)
                kk = (kf[prev] * jnp.exp(c_edge - cum[prev])).astype(BF16)
                earlier = lax.dot_general(ql, kk, _NT, preferred_element_type=F32)
            unseen = jnp.zeros((C, blk - (i + 1) * C), F32) if i < n_sub - 1 else None
            cols = [earlier, diag, unseen] if fwd else [unseen, diag, earlier]
            att_rows[lo] = jnp.concatenate([c for c in cols if c is not None], axis=1).astype(BF16)
        yield
        att = jnp.concatenate([att_rows[lo] for lo in sorted(att_rows)], axis=0)
        out = _dot(jnp.concatenate([att, q_carry], axis=1), jnp.concatenate([vb, state.astype(BF16)], axis=0))
        result[direction] = (out, new_state)

    def body(step, carry):
        blk_f, blk_b = _block_starts(step, n_blocks)
        rf = pl.multiple_of(blk_f * blk, blk)
        rb = pl.multiple_of(blk_b * blk, blk)
        result = {}
        for _ in itertools.zip_longest(block_step(0, rf, result), block_step(1, rb, result)):
            pass
        sf_ref[...] = result[0][1]
        sb_ref[...] = result[1][1]
        o_ref[pl.ds(rf, blk), :] += result[0][0]
        o_ref[pl.ds(rb, blk), :] += result[1][0]
        return carry

    lax.fori_loop(0, n_blocks, body, 0)


def _gla_scan(q, k, v, ga, n_batch, seq):
    return pl.pallas_call(
        functools.partial(_gla_scan_kernel, sub=GLA_CHUNK, block=CTX_LEN, seq=seq),
        grid=(n_batch, GLA_HEADS),
        in_specs=[pl.BlockSpec((seq, GLA_DK), lambda b, h: (b, h)),
                  pl.BlockSpec((seq, GLA_DK), lambda b, h: (b, h)),
                  pl.BlockSpec((seq, GLA_DV), lambda b, h: (b, h)),
                  pl.BlockSpec((seq, GLA_DK), lambda b, h: (b, h)),
                  pl.BlockSpec((seq, GLA_DK), lambda b, h: (b, GLA_HEADS + h))],
        out_specs=pl.BlockSpec((seq, GLA_DV), lambda b, h: (b, h)),
        out_shape=jax.ShapeDtypeStruct((n_batch * seq, GLA_HEADS * GLA_DV), F32),
        scratch_shapes=[pltpu.VMEM((GLA_DK, GLA_DV), F32), pltpu.VMEM((GLA_DK, GLA_DV), F32),
                        pltpu.VMEM((CTX_LEN, CTX_LEN), BF16), pltpu.VMEM((CTX_LEN, CTX_LEN), BF16)],
        compiler_params=_params(2),
        name="gla_scan",
    )(q, k, v, ga, ga)


def _ml_proj_kernel(x_ref, g_ref, sh_ref, sc_ref, w_ref, a_ref, o_ref):
    xn = _modnorm(x_ref[...], g_ref[...], sh_ref[0], sc_ref[0]).astype(BF16)
    cw = 1024
    for j in range(MLSTM_INNER // cw):
        a_ref[:, j * cw:(j + 1) * cw] = _dot(xn, w_ref[:, j * cw:(j + 1) * cw]).astype(BF16)
        o_ref[:, j * cw:(j + 1) * cw] = _dot(
            xn, w_ref[:, MLSTM_INNER + j * cw:MLSTM_INNER + (j + 1) * cw]).astype(BF16)


def _ml_proj(h, n_batch, n_t, gain, shift, scale, w_in):
    ms = _mod_spec(n_batch, 0, True)
    rows = h.shape[0]
    return pl.pallas_call(
        _ml_proj_kernel,
        grid=(n_batch, n_t),
        in_specs=[_row_spec(D_MODEL, n_t, 0), _const_spec((1, D_MODEL)), ms, ms, _const_spec(w_in.shape)],
        out_specs=[_row_spec(MLSTM_INNER, n_t, 0), _row_spec(MLSTM_INNER, n_t, 0)],
        out_shape=[jax.ShapeDtypeStruct((rows, MLSTM_INNER), BF16)] * 2,
        compiler_params=_params(2),
        name="ml_proj",
    )(h, gain, shift, scale, w_in)


def _ml_qkv_kernel(a_ref, prev_ref, next_ref, cw_ref, cb_ref, wq_ref, wkt_ref, wv_ref, wif_ref, wifkt_ref, bif_ref,
                   ac_ref, q_ref, kt_ref, v_ref, gc_ref, gr_ref, *, n_t):
    t = pl.program_id(1)
    has_prev = jnp.where(t >= 2, 1.0, 0.0).astype(F32)
    has_next = jnp.where(jnp.logical_and(t >= 1, t <= n_t - 2), 1.0, 0.0).astype(F32)
    row = lax.broadcasted_iota(jnp.int32, (TM, 1), 0)
    dh = MLSTM_DH
    pre = jnp.zeros((TM, LANES), F32)
    pre_t = jnp.zeros((LANES, TM), F32)
    for hd in range(MLSTM_HEADS):
        lo = hd * dh
        a_bf = a_ref[:, lo:lo + dh]
        a = a_bf.astype(F32)
        before = prev_ref[:, lo:lo + dh].astype(F32)[BF16_ROWS - 1:BF16_ROWS] * has_prev
        after = next_ref[:, lo:lo + dh].astype(F32)[0:1] * has_next
        up = jnp.where(row == 0, before, pltpu.roll(a, 1, axis=0))
        dn = jnp.where(row == TM - 1, after, pltpu.roll(a, TM - 1, axis=0))
        conv = (cw_ref[0:1, lo:lo + dh] * up + cw_ref[1:2, lo:lo + dh] * a + cw_ref[2:3, lo:lo + dh] * dn
                + cb_ref[:, lo:lo + dh])
        ac = _silu(conv).astype(BF16)
        ac_ref[:, lo:lo + dh] = ac
        q = _dot(ac, wq_ref[hd]).astype(BF16)
        kt = (lax.dot_general(wkt_ref[hd], ac, _NT, preferred_element_type=F32) * dh ** -0.5).astype(BF16)
        v = _dot(a_bf, wv_ref[hd]).astype(BF16)
        q_ref[:, lo:lo + dh] = q
        kt_ref[lo:lo + dh, :] = kt
        v_ref[:, lo:lo + dh] = v
        pre = pre + _dot(q, wif_ref[0, hd]) + _dot(v, wif_ref[1, hd])
        pre_t = pre_t + _dot(wifkt_ref[hd], kt)
    pre = pre + pre_t.T + bif_ref[...]
    lane = lax.broadcasted_iota(jnp.int32, (TM, LANES), 1)
    is_forget = (lane % (2 * MLSTM_HEADS)) >= MLSTM_HEADS
    gates = jnp.where(is_forget, _log_sigmoid(pre), pre)
    n_g = 4 * MLSTM_HEADS
    gc_ref[...] = gates[:, :n_g]
    gr_ref[...] = gates.T[:n_g, :]


def _ml_qkv(a, n_batch, n_t, conv_w, conv_b, wq, wkt, wv, wif, wifkt, bif):
    rows = a.shape[0]
    per_tile = TM // BF16_ROWS
    n_halo = rows // BF16_ROWS
    n_g = 4 * MLSTM_HEADS
    wide = _row_spec(MLSTM_INNER, n_t, 0)
    prev_spec = pl.BlockSpec((BF16_ROWS, MLSTM_INNER),
                             lambda b, t: (jnp.maximum((b * n_t + t) * per_tile - 1, 0), 0))
    next_spec = pl.BlockSpec((BF16_ROWS, MLSTM_INNER),
                             lambda b, t: (jnp.minimum((b * n_t + t + 1) * per_tile, n_halo - 1), 0))
    return pl.pallas_call(
        functools.partial(_ml_qkv_kernel, n_t=n_t),
        grid=(n_batch, n_t),
        in_specs=[wide, prev_spec, next_spec, _const_spec(conv_w.shape), _const_spec(conv_b.shape),
                  _const_spec(wq.shape), _const_spec(wkt.shape), _const_spec(wv.shape),
                  _const_spec(wif.shape), _const_spec(wifkt.shape), _const_spec(bif.shape)],
        out_specs=[wide, wide, pl.BlockSpec((MLSTM_INNER, TM), lambda b, t: (0, b * n_t + t)), wide,
                   _row_spec(n_g, n_t, 0), pl.BlockSpec((n_g, TM), lambda b, t: (0, b * n_t + t))],
        out_shape=[jax.ShapeDtypeStruct((rows, MLSTM_INNER), BF16)] * 2
        + [jax.ShapeDtypeStruct((MLSTM_INNER, rows), BF16), jax.ShapeDtypeStruct((rows, MLSTM_INNER), BF16),
           jax.ShapeDtypeStruct((rows, n_g), F32), jax.ShapeDtypeStruct((n_g, rows), F32)],
        compiler_params=_params(2),
        name="ml_qkv",
    )(a, a, a, conv_w, conv_b, wq, wkt, wv, wif, wifkt, bif)


def _ml_scan_kernel(q_ref, kt_ref, v_ref, gc_ref, gr_ref, o_ref, ctf_ref, ctb_ref, *, chunk, seq):
    L = chunk
    dv = v_ref.shape[1]
    n_blocks = seq // L
    li = lax.broadcasted_iota(jnp.int32, (L, L), 0)
    si = lax.broadcasted_iota(jnp.int32, (L, L), 1)
    ones = jnp.ones((L, LANES), BF16)
    for ref in (ctf_ref, ctb_ref, o_ref):
        ref[...] = jnp.zeros_like(ref)

    def chunk_step(direction, r0, m_prev, result):
        ct_ref = ctf_ref if direction == 0 else ctb_ref
        col_i, col_f = (0, 1) if direction == 0 else (2, 3)
        seen = (li >= si) if direction == 0 else (si >= li)
        seen_t = (li <= si) if direction == 0 else (si <= li)
        end_row = L - 1 if direction == 0 else 0
        qc = q_ref[pl.ds(r0, L), :]
        ktc = kt_ref[:, pl.ds(r0, L)]
        vc = v_ref[pl.ds(r0, L), :]
        state = ct_ref[...]
        carried = _dot(qc, state.astype(BF16))
        yield
        qk = _dot(qc, ktc)
        yield
        f_col = gc_ref[pl.ds(r0, L), col_f:col_f + 1]
        f_row = gr_ref[col_f:col_f + 1, pl.ds(r0, L)]
        i_row = gr_ref[col_i:col_i + 1, pl.ds(r0, L)]
        b_col = jnp.sum(jnp.where(seen, f_row, 0.0), axis=1, keepdims=True)
        b_row = jnp.sum(jnp.where(seen_t, f_col, 0.0), axis=0, keepdims=True)
        b_end = b_col[end_row:end_row + 1]
        d_end = b_end - b_row + i_row
        m_new = jnp.maximum(b_end + m_prev, jnp.max(d_end, axis=1, keepdims=True))
        a_prev = jnp.exp(b_end + m_prev - m_new)
        ktw = (ktc.astype(F32) * jnp.exp(d_end - m_new)).astype(BF16)
        ct_ref[:, :dv] = a_prev * state[:, :dv] + _dot(ktw, vc)
        ct_ref[:, dv:] = a_prev * state[:, dv:] + _dot(ktw, ones)
        yield
        d_in = jnp.where(seen, b_col - b_row + i_row, -jnp.inf)
        g = b_col + m_prev
        m_t = jnp.maximum(g, jnp.max(d_in, axis=1, keepdims=True))
        w_prev = jnp.exp(g - m_t)
        s = qk * jnp.exp(d_in - m_t)
        num = _dot(s.astype(BF16), vc) + w_prev * carried[:, :dv]
        den = jnp.sum(s, axis=1, keepdims=True) + w_prev * carried[:, dv:dv + 1]
        out = num * (1.0 / jnp.maximum(jnp.abs(den), jnp.exp(-m_t)))
        o_ref[pl.ds(r0, L), :] = (o_ref[pl.ds(r0, L), :].astype(F32) + out).astype(o_ref.dtype)
        result[direction] = m_new

    def body(step, carry):
        blk_f, blk_b = _block_starts(step, n_blocks)
        result = {}
        for _ in itertools.zip_longest(
                chunk_step(0, pl.multiple_of(blk_f * L, L), carry[0], result),
                chunk_step(1, pl.multiple_of(blk_b * L, L), carry[1], result)):
            pass
        return result[0], result[1]

    lax.fori_loop(0, n_blocks, body, (jnp.zeros((1, 1), F32), jnp.zeros((1, 1), F32)))


def _ml_scan(q, kt, v, gc, gr, n_batch, seq):
    assert MLSTM_CHUNK == CTX_LEN
    dh = MLSTM_DH
    blk = pl.BlockSpec((seq, dh), lambda b, h: (b, h))
    return pl.pallas_call(
        functools.partial(_ml_scan_kernel, chunk=MLSTM_CHUNK, seq=seq),
        grid=(n_batch, MLSTM_HEADS),
        in_specs=[blk, pl.BlockSpec((dh, seq), lambda b, h: (h, b)), blk,
                  pl.BlockSpec((None, None, seq, 4), lambda b, h: (b, h, 0, 0)),
                  pl.BlockSpec((None, None, 4, seq), lambda b, h: (b, h, 0, 0))],
        out_specs=blk,
        out_shape=jax.ShapeDtypeStruct((n_batch * seq, MLSTM_INNER), BF16),
        scratch_shapes=[pltpu.VMEM((dh, dh + LANES), F32), pltpu.VMEM((dh, dh + LANES), F32)],
        compiler_params=_params(2),
        name="ml_scan",
    )(q, kt, v, gc, gr)


def _rope_tables(t_lat):
    n_f = RET_DK // 4
    pos = jnp.arange(t_lat)
    inv = ROPE_BASE ** (-jnp.arange(n_f, dtype=F32) / n_f)
    ang = jnp.concatenate([(pos // GRID_W).astype(F32)[:, None] * inv,
                           (pos % GRID_W).astype(F32)[:, None] * inv], axis=-1)
    cos = jnp.concatenate([jnp.ones((CTX_LEN, 2 * n_f), F32), jnp.cos(ang)], axis=0)
    sin = jnp.concatenate([jnp.zeros((CTX_LEN, 2 * n_f), F32), jnp.sin(ang)], axis=0)
    return cos, sin


def _ret_weights(w_in):
    qk_w = RET_HEADS * RET_DK
    perm = jnp.concatenate([jnp.arange(0, RET_DK, 2), jnp.arange(1, RET_DK, 2)])
    head_cols = (jnp.arange(RET_HEADS)[:, None] * RET_DK + perm[None, :]).reshape(-1)
    w_qvg = jnp.concatenate([w_in[:, head_cols], w_in[:, 2 * qk_w:]], axis=1).astype(BF16)
    wkt = w_in[:, qk_w + head_cols].T.astype(BF16)
    return w_qvg, wkt


def kernel(x, c, ctx, c_ctx, ada_w, ada_b, norm1_g, norm2_g, ffn_w1, ffn_w2, final_g, ret_w_in, ret_log_decay_f, ret_log_decay_b, ret_norm_g, ret_w_out, gla_w_in, gla_wa1_f, gla_wa2_f, gla_ba_f, gla_wa1_b, gla_wa2_b, gla_ba_b, gla_norm_g, gla_w_out, ml_w_in, ml_conv_w, ml_conv_b, ml_w_q, ml_w_k, ml_w_v, ml_w_if_f, ml_b_if_f, ml_w_if_b, ml_b_if_b, ml_skip, ml_w_out):
    n_batch, t_lat, _ = x.shape
    assert ctx.shape[1] == CTX_LEN == TM and t_lat % TM == 0
    seq = CTX_LEN + t_lat
    n_t = seq // TM

    h = jnp.concatenate([ctx, x], axis=1).reshape(n_batch * seq, D_MODEL)

    n_rows = n_batch + 1
    pad = -n_rows % 8
    cvec = jnp.concatenate([c, c_ctx[None, :], jnp.zeros((pad, D_MODEL), F32)], axis=0)
    mod = _ada(cvec, ada_w, ada_b)[:, :n_rows]
    mod = mod.reshape(DEPTH, n_rows, 6, 1, D_MODEL).transpose(0, 2, 1, 3, 4)

    cos, sin = _rope_tables(t_lat)
    final_g2 = final_g.reshape(1, D_MODEL)

    for i in range(DEPTH):
        last = i == DEPTH - 1
        off = 1 if last else 0
        kind, j = i % 3, i // 3
        sh1, sc1, g1, sh2, sc2, g2 = (mod[i, p] for p in range(6))
        gain1 = norm1_g[i].reshape(1, D_MODEL)
        gain2 = norm2_g[i].reshape(1, D_MODEL)
        if kind == 0:
            q, kt, v, g = _ret_proj(h, n_batch, n_t, gain1, sh1, sc1, cos, sin, *_ret_weights(ret_w_in[j]))
            decay = jnp.stack([ret_log_decay_f[j], ret_log_decay_b[j]])
            y = _ret_scan(decay, q, kt, v, n_batch, seq)
            mix = functools.partial(_gated_mix, heads=RET_HEADS, dv=RET_DV)
            mix_rows, mix_consts, w_out = [y, g], [ret_norm_g[j]], ret_w_out[j]
        elif kind == 1:
            kw = GLA_HEADS * GLA_DK
            wa1 = jnp.zeros((D_MODEL, LANES), F32)
            wa1 = wa1.at[:, :GLA_RANK].set(gla_wa1_f[j]).at[:, GLA_RANK:2 * GLA_RANK].set(gla_wa1_b[j])
            wa2 = jnp.zeros((LANES, 2 * kw), F32)
            wa2 = wa2.at[:GLA_RANK, :kw].set(gla_wa2_f[j]).at[GLA_RANK:2 * GLA_RANK, kw:].set(gla_wa2_b[j])
            ba = jnp.concatenate([gla_ba_f[j], gla_ba_b[j]]).reshape(1, 2 * kw)
            q, k, v, r, ga = _gla_proj(h, n_batch, n_t, gain1, sh1, sc1, gla_w_in[j].astype(BF16),
                                       wa1.astype(BF16), wa2.astype(BF16), ba)
            y = _gla_scan(q, k, v, ga, n_batch, seq)
            mix = functools.partial(_gated_mix, heads=GLA_HEADS, dv=GLA_DV)
            mix_rows, mix_consts, w_out = [y, r], [gla_norm_g[j]], gla_w_out[j]
        else:
            a, o_pre = _ml_proj(h, n_batch, n_t, gain1, sh1, sc1, ml_w_in[j].astype(BF16))
            n_g = 4 * MLSTM_HEADS
            wif = jnp.concatenate([ml_w_if_f[j], ml_w_if_b[j]], axis=-1)
            wif = jnp.pad(wif, ((0, 0), (0, 0), (0, 0), (0, LANES - n_g))).astype(BF16)
            bif = jnp.pad(jnp.concatenate([ml_b_if_f[j], ml_b_if_b[j]]), (0, LANES - n_g)).reshape(1, LANES)
            ac, q, kt, v, gc, gr = _ml_qkv(a, n_batch, n_t, ml_conv_w[j], ml_conv_b[j].reshape(1, MLSTM_INNER),
                                           ml_w_q[j].astype(BF16), ml_w_k[j].transpose(0, 2, 1).astype(BF16),
                                           ml_w_v[j].astype(BF16), wif[jnp.array([0, 2])],
                                           wif[1].transpose(0, 2, 1), bif)
            gc = gc.reshape(n_batch, seq, 4, MLSTM_HEADS).transpose(0, 3, 1, 2)
            gr = gr.reshape(4, MLSTM_HEADS, n_batch, seq).transpose(2, 1, 0, 3)
            hs = _ml_scan(q, kt, v, gc, gr, n_batch, seq)
            mix = _ml_mix
            mix_rows, mix_consts, w_out = [hs, o_pre, ac], [ml_skip[j].reshape(1, MLSTM_INNER)], ml_w_out[j]
        h = _mix_ffn(mix, mix_rows, mix_consts, w_out.astype(BF16), h, n_batch, n_t, off, g1, gain2, sh2, sc2, g2,
                     final_g2, ffn_w1[i].astype(BF16), ffn_w2[i].astype(BF16), final_norm=last)
    return h.reshape(n_batch, t_lat, D_MODEL)
```

```python
import functools
import itertools

import jax
import jax.numpy as jnp
from jax import lax
from jax.experimental import pallas as pl
from jax.experimental.pallas import tpu as pltpu

F32 = jnp.float32
BF16 = jnp.bfloat16

D_MODEL = 1024
DEPTH = 4
GRID_W = 64
CTX_LEN = 256
NORM_EPS = 1e-6
D_FF = 4 * D_MODEL
RET_HEADS = 4
RET_DK = D_MODEL // RET_HEADS
RET_DV = 2 * D_MODEL // RET_HEADS
ROPE_BASE = 10000.0
GLA_HEADS = 4
GLA_DK = D_MODEL // 2 // GLA_HEADS
GLA_DV = D_MODEL // GLA_HEADS
GLA_RANK = 16
GLA_TAU = 16.0
MLSTM_INNER = 2 * D_MODEL
MLSTM_HEADS = 4
MLSTM_DH = MLSTM_INNER // MLSTM_HEADS

TM = CTX_LEN
LANES = 128
BF16_ROWS = 16
RET_CHUNK = 256
GLA_CHUNK = 64
MLSTM_CHUNK = 256
VMEM_LIMIT = 56 * 1024 * 1024

_NT = (((1,), (1,)), ((), ()))
_TN = (((0,), (0,)), ((), ()))


def _dot(a, b):
    return jnp.dot(a, b, preferred_element_type=F32)


def _silu(x):
    return x * jax.nn.sigmoid(x)


def _log_sigmoid(z):
    return jnp.minimum(z, 0.0) - jnp.log1p(jnp.exp(-jnp.abs(z)))


def _modnorm(x, g, shift, scale):
    y = x * lax.rsqrt(jnp.mean(x * x, axis=-1, keepdims=True) + NORM_EPS) * g
    return y * (1.0 + scale) + shift


def _params(n_axes):
    return pltpu.CompilerParams(dimension_semantics=("arbitrary",) * n_axes,
                                vmem_limit_bytes=VMEM_LIMIT)


def _const_spec(shape):
    nd = len(shape)
    return pl.BlockSpec(shape, lambda *_: (0,) * nd, pipeline_mode=pl.Buffered(1))


def _row_spec(width, n_tiles, off):
    return pl.BlockSpec((TM, width), lambda b, t: (b * n_tiles + t + off, 0))


def _stream_specs(stream, n_t, off):
    if len(stream) == 1:
        return [_row_spec(D_MODEL, n_t, off)]
    assert off == 0
    return [pl.BlockSpec((TM, D_MODEL), lambda b, t: (b * (n_t - 1) + jnp.maximum(t - 1, 0), 0)),
            pl.BlockSpec((TM, D_MODEL), lambda b, t: (b, 0))]


def _stream_tile(refs):
    if len(refs) == 1:
        return refs[0][...]
    lat_ref, ctx_ref = refs
    return jnp.where(pl.program_id(1) == 0, ctx_ref[...], lat_ref[...])


def _mod_spec(n_batch, off, has_ctx):
    if has_ctx:
        return pl.BlockSpec((1, 1, D_MODEL), lambda b, t: (jnp.where(t + off == 0, n_batch, b), 0, 0))
    return pl.BlockSpec((1, 1, D_MODEL), lambda b, t: (b, 0, 0))


def _ada_kernel(c_ref, w_ref, b_ref, o_ref):
    s = _silu(c_ref[...]).astype(BF16)
    o_ref[0] = _dot(s, w_ref[0].astype(BF16)) + b_ref[0]


def _ada(cvec, ada_w, ada_b):
    rows = cvec.shape[0]
    tn = 512
    return pl.pallas_call(
        _ada_kernel,
        grid=(DEPTH, 6 * D_MODEL // tn),
        in_specs=[pl.BlockSpec((rows, D_MODEL), lambda l, j: (0, 0)),
                  pl.BlockSpec((1, D_MODEL, tn), lambda l, j: (l, 0, j)),
                  pl.BlockSpec((1, 1, tn), lambda l, j: (l, 0, j))],
        out_specs=pl.BlockSpec((1, rows, tn), lambda l, j: (l, 0, j)),
        out_shape=jax.ShapeDtypeStruct((DEPTH, rows, 6 * D_MODEL), F32),
        compiler_params=_params(2),
        name="adaln",
    )(cvec, ada_w, ada_b.reshape(DEPTH, 1, 6 * D_MODEL))


def _gated_mix(y_ref, g_ref, ng_ref, w_ref, *, heads, dv):
    acc = jnp.zeros((TM, D_MODEL), F32)
    for hd in range(heads):
        lo = hd * dv
        y = y_ref[:, lo:lo + dv]
        yn = y * lax.rsqrt(jnp.mean(y * y, axis=-1, keepdims=True) + NORM_EPS) * ng_ref[hd:hd + 1, :]
        z = (_silu(g_ref[:, lo:lo + dv].astype(F32)) * yn).astype(BF16)
        acc = acc + _dot(z, w_ref[lo:lo + dv, :])
    return acc


def _ml_mix(hs_ref, op_ref, ac_ref, skip_ref, w_ref):
    cw = 1024
    acc = jnp.zeros((TM, D_MODEL), F32)
    for j in range(MLSTM_INNER // cw):
        sl = slice(j * cw, (j + 1) * cw)
        y = (jax.nn.sigmoid(op_ref[:, sl].astype(F32)) * hs_ref[:, sl].astype(F32)
             + skip_ref[:, sl] * ac_ref[:, sl].astype(F32))
        acc = acc + _dot(y.astype(BF16), w_ref[sl, :])
    return acc


def _mix_ffn_kernel(*refs, n_mix, n_stream, mix, final_norm):
    mix_refs, wo_ref, stream_refs = refs[:n_mix], refs[n_mix], refs[n_mix + 1:n_mix + 1 + n_stream]
    (gate1_ref, g_ref, sh_ref, sc_ref, gate2_ref, fg_ref, w1_ref, w2_ref, o_ref) = refs[n_mix + 1 + n_stream:]
    x = _stream_tile(stream_refs) + gate1_ref[0] * mix(*mix_refs, wo_ref)
    xn = _modnorm(x, g_ref[...], sh_ref[0], sc_ref[0]).astype(BF16)
    fc = 1024
    acc = jnp.zeros((TM, D_MODEL), F32)
    for j in range(D_FF // fc):
        hid = _dot(xn, w1_ref[:, j * fc:(j + 1) * fc])
        hid = jnp.square(jnp.maximum(hid, 0.0)).astype(BF16)
        acc = acc + _dot(hid, w2_ref[j * fc:(j + 1) * fc, :])
    o = x + gate2_ref[0] * acc
    if final_norm:
        o = o * lax.rsqrt(jnp.mean(o * o, axis=-1, keepdims=True) + NORM_EPS) * fg_ref[...]
    o_ref[...] = o


def _mix_ffn(mix, mix_rows, mix_consts, w_out, stream, n_batch, n_t, off, gate1, gain, shift, scale, gate2,
             final_g, w1, w2, *, final_norm):
    n_out = n_t - off
    ms = _mod_spec(n_batch, off, off == 0)
    in_specs = ([_row_spec(a.shape[1], n_t, off) for a in mix_rows] + [_const_spec(a.shape) for a in mix_consts]
                + [_const_spec(w_out.shape)] + _stream_specs(stream, n_t, off)
                + [ms, _const_spec((1, D_MODEL)), ms, ms, ms,
                   _const_spec((1, D_MODEL)), _const_spec((D_MODEL, D_FF)), _const_spec((D_FF, D_MODEL))])
    return pl.pallas_call(
        functools.partial(_mix_ffn_kernel, n_mix=len(mix_rows) + len(mix_consts), n_stream=len(stream), mix=mix,
                          final_norm=final_norm),
        grid=(n_batch, n_out),
        in_specs=in_specs,
        out_specs=_row_spec(D_MODEL, n_out, 0),
        out_shape=jax.ShapeDtypeStruct((n_batch * n_out * TM, D_MODEL), F32),
        compiler_params=_params(2),
        name="mix_ffn",
    )(*mix_rows, *mix_consts, w_out, *stream, gate1, gain, shift, scale, gate2, final_g, w1, w2)


def _ret_proj_kernel(*refs, n_stream):
    (g_ref, sh_ref, sc_ref, cos_ref, sin_ref, cost_ref, sint_ref, w_ref, wkt_ref,
     q_ref, kt_ref, v_ref, gg_ref) = refs[n_stream:]
    xn = _modnorm(_stream_tile(refs[:n_stream]), g_ref[...], sh_ref[0], sc_ref[0]).astype(BF16)
    cos, sin = cos_ref[...], sin_ref[...]
    cos_t, sin_t = cost_ref[...], sint_ref[...]
    half = RET_DK // 2
    qk_w, v_w = RET_HEADS * RET_DK, RET_HEADS * RET_DV
    for hd in range(RET_HEADS):
        lo = hd * RET_DK
        u = _dot(xn, w_ref[:, lo:lo + RET_DK])
        u1, u2 = u[:, :half], u[:, half:]
        q_ref[:, lo:lo + RET_DK] = jnp.concatenate([u1 * cos - u2 * sin, u1 * sin + u2 * cos], axis=1).astype(BF16)
        ut = lax.dot_general(wkt_ref[lo:lo + RET_DK, :], xn, _NT, preferred_element_type=F32) * RET_DK ** -0.5
        u1, u2 = ut[:half], ut[half:]
        kt_ref[lo:lo + RET_DK, :] = jnp.concatenate([u1 * cos_t - u2 * sin_t, u1 * sin_t + u2 * cos_t],
                                                    axis=0).astype(BF16)
    for hd in range(RET_HEADS):
        lo = hd * RET_DV
        v_ref[:, lo:lo + RET_DV] = _dot(xn, w_ref[:, qk_w + lo:qk_w + lo + RET_DV]).astype(BF16)
        gg_ref[:, lo:lo + RET_DV] = _dot(xn, w_ref[:, qk_w + v_w + lo:qk_w + v_w + lo + RET_DV]).astype(BF16)


def _ret_proj(stream, n_batch, n_t, gain, shift, scale, cos, sin, w_qvg, wkt):
    ms = _mod_spec(n_batch, 0, True)
    rows = n_batch * n_t * TM
    qk_w, v_w = RET_HEADS * RET_DK, RET_HEADS * RET_DV
    tab = pl.BlockSpec((TM, RET_DK // 2), lambda b, t: (t, 0))
    tab_t = pl.BlockSpec((RET_DK // 2, TM), lambda b, t: (0, t))
    return pl.pallas_call(
        functools.partial(_ret_proj_kernel, n_stream=len(stream)),
        grid=(n_batch, n_t),
        in_specs=_stream_specs(stream, n_t, 0) + [_const_spec((1, D_MODEL)), ms, ms, tab, tab, tab_t, tab_t,
                                                  _const_spec(w_qvg.shape), _const_spec(wkt.shape)],
        out_specs=[_row_spec(qk_w, n_t, 0), pl.BlockSpec((qk_w, TM), lambda b, t: (0, b * n_t + t)),
                   _row_spec(v_w, n_t, 0), _row_spec(v_w, n_t, 0)],
        out_shape=[jax.ShapeDtypeStruct((rows, qk_w), BF16), jax.ShapeDtypeStruct((qk_w, rows), BF16),
                   jax.ShapeDtypeStruct((rows, v_w), BF16), jax.ShapeDtypeStruct((rows, v_w), BF16)],
        compiler_params=_params(2),
        name="ret_proj",
    )(*stream, gain, shift, scale, cos, sin, cos.T, sin.T, w_qvg, wkt)


def _block_starts(step, n_blocks):
    return step, jnp.where(step == 0, 0, n_blocks - step)


def _ret_scan_kernel(ld_ref, q_ref, kt_ref, v_ref, o_ref, sf_ref, sb_ref, dmf_ref, dmb_ref, *, chunk, seq):
    L = chunk
    hd = pl.program_id(1)
    n_blocks = seq // L
    li = lax.broadcasted_iota(jnp.int32, (L, L), 0)
    si = lax.broadcasted_iota(jnp.int32, (L, L), 1)
    row = lax.broadcasted_iota(jnp.int32, (L, 1), 0).astype(F32)
    col = lax.broadcasted_iota(jnp.int32, (1, L), 1).astype(F32)
    consts = []
    for direction, dm_ref in ((0, dmf_ref), (1, dmb_ref)):
        lg = -jnp.exp(jnp.full((1, 1), ld_ref[direction, hd], F32))
        if direction == 0:
            dist = li - si
            q_dec = jnp.exp((row + 1.0) * lg)
            k_dec = jnp.exp((L - 1.0 - col) * lg)
        else:
            dist = si - li
            q_dec = jnp.exp((L - row) * lg)
            k_dec = jnp.exp(col * lg)
        dm_ref[...] = jnp.where(dist >= 0, jnp.exp(jnp.maximum(dist, 0).astype(F32) * lg), 0.0)
        consts.append((q_dec, k_dec, jnp.exp(float(L) * lg)))
    sf_ref[...] = jnp.zeros_like(sf_ref)
    sb_ref[...] = jnp.zeros_like(sb_ref)
    o_ref[...] = jnp.zeros_like(o_ref)

    def chunk_step(r0, s_ref, dm_ref, q_dec, k_dec, s_dec):
        qc = q_ref[pl.ds(r0, L), :]
        ktc = kt_ref[:, pl.ds(r0, L)]
        vc = v_ref[pl.ds(r0, L), :]
        state = s_ref[...]
        carried = _dot(qc, state.astype(BF16))
        yield
        att = _dot(qc, ktc) * dm_ref[...]
        yield
        kd = (ktc.astype(F32) * k_dec).astype(BF16)
        s_ref[...] = s_dec * state + _dot(kd, vc)
        yield
        o_ref[pl.ds(r0, L), :] += _dot(att.astype(BF16), vc) + q_dec * carried

    def body(step, carry):
        blk_f, blk_b = _block_starts(step, n_blocks)
        for _ in itertools.zip_longest(
                chunk_step(pl.multiple_of(blk_f * L, L), sf_ref, dmf_ref, *consts[0]),
                chunk_step(pl.multiple_of(blk_b * L, L), sb_ref, dmb_ref, *consts[1])):
            pass
        return carry

    lax.fori_loop(0, n_blocks, body, 0)


def _ret_scan(log_decay, q, kt, v, n_batch, seq):
    assert RET_CHUNK == CTX_LEN
    return pl.pallas_call(
        functools.partial(_ret_scan_kernel, chunk=RET_CHUNK, seq=seq),
        grid=(n_batch, RET_HEADS),
        in_specs=[pl.BlockSpec(memory_space=pltpu.SMEM),
                  pl.BlockSpec((seq, RET_DK), lambda b, h: (b, h)),
                  pl.BlockSpec((RET_DK, seq), lambda b, h: (h, b)),
                  pl.BlockSpec((seq, RET_DV), lambda b, h: (b, h))],
        out_specs=pl.BlockSpec((seq, RET_DV), lambda b, h: (b, h)),
        out_shape=jax.ShapeDtypeStruct((n_batch * seq, RET_HEADS * RET_DV), F32),
        scratch_shapes=[pltpu.VMEM((RET_DK, RET_DV), F32), pltpu.VMEM((RET_DK, RET_DV), F32),
                        pltpu.VMEM((RET_CHUNK, RET_CHUNK), F32), pltpu.VMEM((RET_CHUNK, RET_CHUNK), F32)],
        compiler_params=_params(2),
        name="ret_scan",
    )(log_decay, q, kt, v)


def _gla_proj_kernel(x_ref, g_ref, sh_ref, sc_ref, w_ref, wa1_ref, wa2_ref, ba_ref,
                     q_ref, k_ref, v_ref, r_ref, ga_ref):
    xn = _modnorm(x_ref[...], g_ref[...], sh_ref[0], sc_ref[0]).astype(BF16)
    kw, vw = GLA_HEADS * GLA_DK, GLA_HEADS * GLA_DV
    q_ref[...] = (_dot(xn, w_ref[:, :kw]) * GLA_DK ** -0.5).astype(BF16)
    k_ref[...] = _dot(xn, w_ref[:, kw:2 * kw]).astype(BF16)
    v_ref[...] = _dot(xn, w_ref[:, 2 * kw:2 * kw + vw]).astype(BF16)
    r_ref[...] = _dot(xn, w_ref[:, 2 * kw + vw:]).astype(BF16)
    low = _dot(xn, wa1_ref[...]).astype(BF16)
    z = _dot(low, wa2_ref[...]) + ba_ref[...]
    ga_ref[...] = _log_sigmoid(z) / GLA_TAU


def _gla_proj(h, n_batch, n_t, gain, shift, scale, w_in, wa1, wa2, ba):
    ms = _mod_spec(n_batch, 0, True)
    rows = h.shape[0]
    kw, vw = GLA_HEADS * GLA_DK, GLA_HEADS * GLA_DV
    return pl.pallas_call(
        _gla_proj_kernel,
        grid=(n_batch, n_t),
        in_specs=[_row_spec(D_MODEL, n_t, 0), _const_spec((1, D_MODEL)), ms, ms, _const_spec(w_in.shape),
                  _const_spec(wa1.shape), _const_spec(wa2.shape), _const_spec(ba.shape)],
        out_specs=[_row_spec(kw, n_t, 0), _row_spec(kw, n_t, 0), _row_spec(vw, n_t, 0), _row_spec(vw, n_t, 0),
                   _row_spec(2 * kw, n_t, 0)],
        out_shape=[jax.ShapeDtypeStruct((rows, kw), BF16), jax.ShapeDtypeStruct((rows, kw), BF16),
                   jax.ShapeDtypeStruct((rows, vw), BF16), jax.ShapeDtypeStruct((rows, vw), BF16),
                   jax.ShapeDtypeStruct((rows, 2 * kw), F32)],
        compiler_params=_params(2),
        name="gla_proj",
    )(h, gain, shift, scale, w_in, wa1, wa2, ba)


def _gla_scan_kernel(q_ref, k_ref, v_ref, gf_ref, gb_ref, o_ref, sf_ref, sb_ref, trif_ref, trib_ref,
                     *, sub, block, seq):
    C, blk = sub, block
    n_sub, n_blocks = blk // C, seq // blk
    li = lax.broadcasted_iota(jnp.int32, (blk, blk), 0)
    si = lax.broadcasted_iota(jnp.int32, (blk, blk), 1)
    trif_ref[...] = (li >= si).astype(BF16)
    trib_ref[...] = (si >= li).astype(BF16)
    dl = lax.broadcasted_iota(jnp.int32, (C, C), 0)
    ds = lax.broadcasted_iota(jnp.int32, (C, C), 1)
    diag_masks = (dl >= ds, ds >= dl)
    eye = (lax.broadcasted_iota(jnp.int32, (GLA_DK, GLA_DK), 0)
           == lax.broadcasted_iota(jnp.int32, (GLA_DK, GLA_DK), 1))
    sf_ref[...] = jnp.zeros_like(sf_ref)
    sb_ref[...] = jnp.zeros_like(sb_ref)
    o_ref[...] = jnp.zeros_like(o_ref)

    def block_step(direction, r0, key, prev_key, pub):
        fwd = direction == 0
        gate_ref, tri_ref, s_ref = (gf_ref, trif_ref, sf_ref) if fwd else (gb_ref, trib_ref, sb_ref)
        g = gate_ref[pl.ds(r0, blk), :]
        g_hi = g.astype(BF16)
        rem = g - g_hi.astype(F32)
        g_mid = rem.astype(BF16)
        g_lo = (rem - g_mid.astype(F32)).astype(BF16)
        parts = _dot(tri_ref[...], jnp.concatenate([g_hi, g_mid, g_lo], axis=1))
        yield
        cum = parts[:, :GLA_DK] + parts[:, GLA_DK:2 * GLA_DK] + parts[:, 2 * GLA_DK:]
        qf = q_ref[pl.ds(r0, blk), :].astype(F32)
        kf = k_ref[pl.ds(r0, blk), :].astype(F32)
        vb = v_ref[pl.ds(r0, blk), :]
        end_row = blk - 1 if fwd else 0
        c_end = cum[end_row:end_row + 1]
        state = s_ref[...] if prev_key is None else pub[prev_key]["state"]
        q_carry = (qf * jnp.exp(cum)).astype(BF16)
        kd = (kf * jnp.exp(c_end - cum)).astype(BF16)
        e_col = jnp.sum(jnp.where(eye, jnp.exp(c_end), 0.0), axis=1, keepdims=True)
        pub[key] = {"state": e_col * state + lax.dot_general(kd, vb, _TN, preferred_element_type=F32)}
        att_rows = {}
        for i in range(n_sub):
            yield
            lo = i * C if fwd else blk - (i + 1) * C
            rows = slice(lo, lo + C)
            mid = lo + (C // 2 if fwd else C - 1 - C // 2)
            cq = cum[rows]
            c_mid = cum[mid:mid + 1]
            qd = (qf[rows] * jnp.exp(cq - c_mid)).astype(BF16)
            kg = (kf[rows] * jnp.exp(c_mid - cq)).astype(BF16)
            diag = jnp.where(diag_masks[direction], lax.dot_general(qd, kg, _NT, preferred_element_type=F32), 0.0)
            earlier = None
            if i > 0:
                prev = slice(0, lo) if fwd else slice(lo + C, blk)
                edge = lo - 1 if fwd else lo + C
                c_edge = cum[edge:edge + 1]
                ql = (qf[rows] * jnp.exp(cq - c_edge)).astype(BF16)
                kk = (kf[prev] * jnp.exp(c_edge - cum[prev])).astype(BF16)
                earlier = lax.dot_general(ql, kk, _NT, preferred_element_type=F32)
            unseen = jnp.zeros((C, blk - (i + 1) * C), F32) if i < n_sub - 1 else None
            cols = [earlier, diag, unseen] if fwd else [unseen, diag, earlier]
            att_rows[lo] = jnp.concatenate([c for c in cols if c is not None], axis=1).astype(BF16)
        yield
        att = jnp.concatenate([att_rows[lo] for lo in sorted(att_rows)], axis=0)
        out = _dot(jnp.concatenate([att, q_carry], axis=1), jnp.concatenate([vb, state.astype(BF16)], axis=0))
        pub[key]["out"] = out

    def run(blocks_f, blocks_b):
        pub, gens, starts = {}, [], {}
        for n in range(len(blocks_f)):
            for direction, blocks in ((0, blocks_f), (1, blocks_b)):
                starts[direction, n] = pl.multiple_of(blocks[n] * blk, blk)
                gens.append(block_step(direction, starts[direction, n], (direction, n),
                                       (direction, n - 1) if n else None, pub))
        for _ in itertools.zip_longest(*gens):
            pass
        last = len(blocks_f) - 1
        sf_ref[...] = pub[0, last]["state"]
        sb_ref[...] = pub[1, last]["state"]
        for key in sorted(pub, key=lambda kn: (kn[1], kn[0])):
            o_ref[pl.ds(starts[key], blk), :] += pub[key]["out"]

    n_lat = n_blocks - 1
    assert n_lat % 2 == 0
    run([0], [0])

    def body(j, carry):
        run([1 + 2 * j, 2 + 2 * j], [n_blocks - 1 - 2 * j, n_blocks - 2 - 2 * j])
        return carry

    lax.fori_loop(0, n_lat // 2, body, 0)


def _gla_scan(q, k, v, ga, n_batch, seq):
    return pl.pallas_call(
        functools.partial(_gla_scan_kernel, sub=GLA_CHUNK, block=CTX_LEN, seq=seq),
        grid=(n_batch, GLA_HEADS),
        in_specs=[pl.BlockSpec((seq, GLA_DK), lambda b, h: (b, h)),
                  pl.BlockSpec((seq, GLA_DK), lambda b, h: (b, h)),
                  pl.BlockSpec((seq, GLA_DV), lambda b, h: (b, h)),
                  pl.BlockSpec((seq, GLA_DK), lambda b, h: (b, h)),
                  pl.BlockSpec((seq, GLA_DK), lambda b, h: (b, GLA_HEADS + h))],
        out_specs=pl.BlockSpec((seq, GLA_DV), lambda b, h: (b, h)),
        out_shape=jax.ShapeDtypeStruct((n_batch * seq, GLA_HEADS * GLA_DV), F32),
        scratch_shapes=[pltpu.VMEM((GLA_DK, GLA_DV), F32), pltpu.VMEM((GLA_DK, GLA_DV), F32),
                        pltpu.VMEM((CTX_LEN, CTX_LEN), BF16), pltpu.VMEM((CTX_LEN, CTX_LEN), BF16)],
        compiler_params=_params(2),
        name="gla_scan",
    )(q, k, v, ga, ga)


def _ml_proj_kernel(x_ref, g_ref, sh_ref, sc_ref, w_ref, a_ref, o_ref):
    xn = _modnorm(x_ref[...], g_ref[...], sh_ref[0], sc_ref[0]).astype(BF16)
    cw = 1024
    for j in range(MLSTM_INNER // cw):
        a_ref[:, j * cw:(j + 1) * cw] = _dot(xn, w_ref[:, j * cw:(j + 1) * cw]).astype(BF16)
        o_ref[:, j * cw:(j + 1) * cw] = _dot(
            xn, w_ref[:, MLSTM_INNER + j * cw:MLSTM_INNER + (j + 1) * cw]).astype(BF16)


def _ml_proj(h, n_batch, n_t, gain, shift, scale, w_in):
    ms = _mod_spec(n_batch, 0, True)
    rows = h.shape[0]
    return pl.pallas_call(
        _ml_proj_kernel,
        grid=(n_batch, n_t),
        in_specs=[_row_spec(D_MODEL, n_t, 0), _const_spec((1, D_MODEL)), ms, ms, _const_spec(w_in.shape)],
        out_specs=[_row_spec(MLSTM_INNER, n_t, 0), _row_spec(MLSTM_INNER, n_t, 0)],
        out_shape=[jax.ShapeDtypeStruct((rows, MLSTM_INNER), BF16)] * 2,
        compiler_params=_params(2),
        name="ml_proj",
    )(h, gain, shift, scale, w_in)


def _ml_qkv_kernel(a_ref, prev_ref, next_ref, cw_ref, cb_ref, wq_ref, wkt_ref, wv_ref, wif_ref, wifkt_ref, bif_ref,
                   ac_ref, q_ref, kt_ref, v_ref, gc_ref, gr_ref, *, n_t):
    t = pl.program_id(1)
    has_prev = jnp.where(t >= 2, 1.0, 0.0).astype(F32)
    has_next = jnp.where(jnp.logical_and(t >= 1, t <= n_t - 2), 1.0, 0.0).astype(F32)
    row = lax.broadcasted_iota(jnp.int32, (TM, 1), 0)
    dh = MLSTM_DH
    pre = jnp.zeros((TM, LANES), F32)
    pre_t = jnp.zeros((LANES, TM), F32)
    for hd in range(MLSTM_HEADS):
        lo = hd * dh
        a_bf = a_ref[:, lo:lo + dh]
        a = a_bf.astype(F32)
        before = prev_ref[:, lo:lo + dh].astype(F32)[BF16_ROWS - 1:BF16_ROWS] * has_prev
        after = next_ref[:, lo:lo + dh].astype(F32)[0:1] * has_next
        up = jnp.where(row == 0, before, pltpu.roll(a, 1, axis=0))
        dn = jnp.where(row == TM - 1, after, pltpu.roll(a, TM - 1, axis=0))
        conv = (cw_ref[0:1, lo:lo + dh] * up + cw_ref[1:2, lo:lo + dh] * a + cw_ref[2:3, lo:lo + dh] * dn
                + cb_ref[:, lo:lo + dh])
        ac = _silu(conv).astype(BF16)
        ac_ref[:, lo:lo + dh] = ac
        q = _dot(ac, wq_ref[hd]).astype(BF16)
        kt = (lax.dot_general(wkt_ref[hd], ac, _NT, preferred_element_type=F32) * dh ** -0.5).astype(BF16)
        v = _dot(a_bf, wv_ref[hd]).astype(BF16)
        q_ref[:, lo:lo + dh] = q
        kt_ref[lo:lo + dh, :] = kt
        v_ref[:, lo:lo + dh] = v
        pre = pre + _dot(q, wif_ref[0, hd]) + _dot(v, wif_ref[1, hd])
        pre_t = pre_t + _dot(wifkt_ref[hd], kt)
    pre = pre + pre_t.T + bif_ref[...]
    lane = lax.broadcasted_iota(jnp.int32, (TM, LANES), 1)
    is_forget = (lane % (2 * MLSTM_HEADS)) >= MLSTM_HEADS
    gates = jnp.where(is_forget, _log_sigmoid(pre), pre)
    n_g = 4 * MLSTM_HEADS
    gc_ref[...] = gates[:, :n_g]
    gr_ref[...] = gates.T[:n_g, :]


def _ml_qkv(a, n_batch, n_t, conv_w, conv_b, wq, wkt, wv, wif, wifkt, bif):
    rows = a.shape[0]
    per_tile = TM // BF16_ROWS
    n_halo = rows // BF16_ROWS
    n_g = 4 * MLSTM_HEADS
    wide = _row_spec(MLSTM_INNER, n_t, 0)
    prev_spec = pl.BlockSpec((BF16_ROWS, MLSTM_INNER),
                             lambda b, t: (jnp.maximum((b * n_t + t) * per_tile - 1, 0), 0))
    next_spec = pl.BlockSpec((BF16_ROWS, MLSTM_INNER),
                             lambda b, t: (jnp.minimum((b * n_t + t + 1) * per_tile, n_halo - 1), 0))
    return pl.pallas_call(
        functools.partial(_ml_qkv_kernel, n_t=n_t),
        grid=(n_batch, n_t),
        in_specs=[wide, prev_spec, next_spec, _const_spec(conv_w.shape), _const_spec(conv_b.shape),
                  _const_spec(wq.shape), _const_spec(wkt.shape), _const_spec(wv.shape),
                  _const_spec(wif.shape), _const_spec(wifkt.shape), _const_spec(bif.shape)],
        out_specs=[wide, wide, pl.BlockSpec((MLSTM_INNER, TM), lambda b, t: (0, b * n_t + t)), wide,
                   _row_spec(n_g, n_t, 0), pl.BlockSpec((n_g, TM), lambda b, t: (0, b * n_t + t))],
        out_shape=[jax.ShapeDtypeStruct((rows, MLSTM_INNER), BF16)] * 2
        + [jax.ShapeDtypeStruct((MLSTM_INNER, rows), BF16), jax.ShapeDtypeStruct((rows, MLSTM_INNER), BF16),
           jax.ShapeDtypeStruct((rows, n_g), F32), jax.ShapeDtypeStruct((n_g, rows), F32)],
        compiler_params=_params(2),
        name="ml_qkv",
    )(a, a, a, conv_w, conv_b, wq, wkt, wv, wif, wifkt, bif)


def _ml_scan_kernel(q_ref, kt_ref, v_ref, gc_ref, gr_ref, o_ref, ctf_ref, ctb_ref, *, chunk, seq):
    L = chunk
    dv = v_ref.shape[1]
    n_blocks = seq // L
    li = lax.broadcasted_iota(jnp.int32, (L, L), 0)
    si = lax.broadcasted_iota(jnp.int32, (L, L), 1)
    for ref in (ctf_ref, ctb_ref, o_ref):
        ref[...] = jnp.zeros_like(ref)

    def chunk_step(direction, r0, m_prev, result):
        ct_ref = ctf_ref if direction == 0 else ctb_ref
        col_i, col_f = (0, 1) if direction == 0 else (2, 3)
        seen = (li >= si) if direction == 0 else (si >= li)
        seen_t = (li <= si) if direction == 0 else (si <= li)
        end_row = L - 1 if direction == 0 else 0
        qc = q_ref[pl.ds(r0, L), :]
        ktc = kt_ref[:, pl.ds(r0, L)]
        vc = v_ref[pl.ds(r0, L), :]
        qk = _dot(qc, ktc)
        yield
        f_col = gc_ref[pl.ds(r0, L), col_f:col_f + 1]
        f_row = gr_ref[col_f:col_f + 1, pl.ds(r0, L)]
        i_row = gr_ref[col_i:col_i + 1, pl.ds(r0, L)]
        b_col = jnp.sum(jnp.where(seen, f_row, 0.0), axis=1, keepdims=True)
        b_row = jnp.sum(jnp.where(seen_t, f_col, 0.0), axis=0, keepdims=True)
        d_in = jnp.where(seen, b_col - b_row + i_row, -jnp.inf)
        g = b_col + m_prev
        m_t = jnp.maximum(g, jnp.max(d_in, axis=1, keepdims=True))
        w_prev = jnp.exp(g - m_t)
        s = qk * jnp.exp(d_in - m_t)
        den_intra = jnp.sum(s, axis=1, keepdims=True)
        num_intra = _dot(s.astype(BF16), vc)
        yield
        carried = _dot(qc, ct_ref[...].astype(BF16))
        yield
        den = den_intra + w_prev * carried[:, dv:dv + 1]
        out = (num_intra + w_prev * carried[:, :dv]) * (1.0 / jnp.maximum(jnp.abs(den), jnp.exp(-m_t)))
        o_ref[pl.ds(r0, L), :] = (o_ref[pl.ds(r0, L), :].astype(F32) + out).astype(o_ref.dtype)
        b_end = b_col[end_row:end_row + 1]
        d_end = b_end - b_row + i_row
        m_new = jnp.maximum(b_end + m_prev, jnp.max(d_end, axis=1, keepdims=True))
        a_prev = jnp.exp(b_end + m_prev - m_new)
        ktw = ktc.astype(F32) * jnp.exp(d_end - m_new)
        ct_ref[:, :dv] = a_prev * ct_ref[:, :dv] + _dot(ktw.astype(BF16), vc)
        ct_ref[:, dv:] = a_prev * ct_ref[:, dv:] + jnp.sum(ktw, axis=1, keepdims=True)
        result[direction] = m_new

    def body(step, carry):
        blk_f, blk_b = _block_starts(step, n_blocks)
        result = {}
        for _ in itertools.zip_longest(
                chunk_step(0, pl.multiple_of(blk_f * L, L), carry[0], result),
                chunk_step(1, pl.multiple_of(blk_b * L, L), carry[1], result)):
            pass
        return result[0], result[1]

    lax.fori_loop(0, n_blocks, body, (jnp.zeros((1, 1), F32), jnp.zeros((1, 1), F32)))


def _ml_scan(q, kt, v, gc, gr, n_batch, seq):
    assert MLSTM_CHUNK == CTX_LEN
    dh = MLSTM_DH
    blk = pl.BlockSpec((seq, dh), lambda b, h: (b, h))
    return pl.pallas_call(
        functools.partial(_ml_scan_kernel, chunk=MLSTM_CHUNK, seq=seq),
        grid=(n_batch, MLSTM_HEADS),
        in_specs=[blk, pl.BlockSpec((dh, seq), lambda b, h: (h, b)), blk,
                  pl.BlockSpec((None, None, seq, 4), lambda b, h: (b, h, 0, 0)),
                  pl.BlockSpec((None, None, 4, seq), lambda b, h: (b, h, 0, 0))],
        out_specs=blk,
        out_shape=jax.ShapeDtypeStruct((n_batch * seq, MLSTM_INNER), BF16),
        scratch_shapes=[pltpu.VMEM((dh, dh + LANES), F32), pltpu.VMEM((dh, dh + LANES), F32)],
        compiler_params=_params(2),
        name="ml_scan",
    )(q, kt, v, gc, gr)


def _rope_tables(t_lat):
    n_f = RET_DK // 4
    pos = jnp.arange(t_lat)
    inv = ROPE_BASE ** (-jnp.arange(n_f, dtype=F32) / n_f)
    ang = jnp.concatenate([(pos // GRID_W).astype(F32)[:, None] * inv,
                           (pos % GRID_W).astype(F32)[:, None] * inv], axis=-1)
    cos = jnp.concatenate([jnp.ones((CTX_LEN, 2 * n_f), F32), jnp.cos(ang)], axis=0)
    sin = jnp.concatenate([jnp.zeros((CTX_LEN, 2 * n_f), F32), jnp.sin(ang)], axis=0)
    return cos, sin


def _ret_weights(w_in):
    qk_w = RET_HEADS * RET_DK
    perm = jnp.concatenate([jnp.arange(0, RET_DK, 2), jnp.arange(1, RET_DK, 2)])
    head_cols = (jnp.arange(RET_HEADS)[:, None] * RET_DK + perm[None, :]).reshape(-1)
    w_qvg = jnp.concatenate([w_in[:, head_cols], w_in[:, 2 * qk_w:]], axis=1).astype(BF16)
    wkt = w_in[:, qk_w + head_cols].T.astype(BF16)
    return w_qvg, wkt


def kernel(x, c, ctx, c_ctx, ada_w, ada_b, norm1_g, norm2_g, ffn_w1, ffn_w2, final_g, ret_w_in, ret_log_decay_f, ret_log_decay_b, ret_norm_g, ret_w_out, gla_w_in, gla_wa1_f, gla_wa2_f, gla_ba_f, gla_wa1_b, gla_wa2_b, gla_ba_b, gla_norm_g, gla_w_out, ml_w_in, ml_conv_w, ml_conv_b, ml_w_q, ml_w_k, ml_w_v, ml_w_if_f, ml_b_if_f, ml_w_if_b, ml_b_if_b, ml_skip, ml_w_out):
    n_batch, t_lat, _ = x.shape
    assert ctx.shape[1] == CTX_LEN == TM and t_lat % TM == 0
    seq = CTX_LEN + t_lat
    n_t = seq // TM

    stream = (x.reshape(n_batch * t_lat, D_MODEL), ctx.reshape(n_batch * CTX_LEN, D_MODEL))

    n_rows = n_batch + 1
    pad = -n_rows % 8
    cvec = jnp.concatenate([c, c_ctx[None, :], jnp.zeros((pad, D_MODEL), F32)], axis=0)
    mod = _ada(cvec, ada_w, ada_b)[:, :n_rows]
    mod = mod.reshape(DEPTH, n_rows, 6, 1, D_MODEL).transpose(0, 2, 1, 3, 4)

    cos, sin = _rope_tables(t_lat)
    final_g2 = final_g.reshape(1, D_MODEL)

    for i in range(DEPTH):
        last = i == DEPTH - 1
        off = 1 if last else 0
        kind, j = i % 3, i // 3
        sh1, sc1, g1, sh2, sc2, g2 = (mod[i, p] for p in range(6))
        gain1 = norm1_g[i].reshape(1, D_MODEL)
        gain2 = norm2_g[i].reshape(1, D_MODEL)
        if kind == 0:
            q, kt, v, g = _ret_proj(stream, n_batch, n_t, gain1, sh1, sc1, cos, sin, *_ret_weights(ret_w_in[j]))
            decay = jnp.stack([ret_log_decay_f[j], ret_log_decay_b[j]])
            y = _ret_scan(decay, q, kt, v, n_batch, seq)
            mix = functools.partial(_gated_mix, heads=RET_HEADS, dv=RET_DV)
            mix_rows, mix_consts, w_out = [y, g], [ret_norm_g[j]], ret_w_out[j]
        elif kind == 1:
            kw = GLA_HEADS * GLA_DK
            wa1 = jnp.zeros((D_MODEL, LANES), F32)
            wa1 = wa1.at[:, :GLA_RANK].set(gla_wa1_f[j]).at[:, GLA_RANK:2 * GLA_RANK].set(gla_wa1_b[j])
            wa2 = jnp.zeros((LANES, 2 * kw), F32)
            wa2 = wa2.at[:GLA_RANK, :kw].set(gla_wa2_f[j]).at[GLA_RANK:2 * GLA_RANK, kw:].set(gla_wa2_b[j])
            ba = jnp.concatenate([gla_ba_f[j], gla_ba_b[j]]).reshape(1, 2 * kw)
            q, k, v, r, ga = _gla_proj(h, n_batch, n_t, gain1, sh1, sc1, gla_w_in[j].astype(BF16),
                                       wa1.astype(BF16), wa2.astype(BF16), ba)
            y = _gla_scan(q, k, v, ga, n_batch, seq)
            mix = functools.partial(_gated_mix, heads=GLA_HEADS, dv=GLA_DV)
            mix_rows, mix_consts, w_out = [y, r], [gla_norm_g[j]], gla_w_out[j]
        else:
            a, o_pre = _ml_proj(h, n_batch, n_t, gain1, sh1, sc1, ml_w_in[j].astype(BF16))
            n_g = 4 * MLSTM_HEADS
            wif = jnp.concatenate([ml_w_if_f[j], ml_w_if_b[j]], axis=-1)
            wif = jnp.pad(wif, ((0, 0), (0, 0), (0, 0), (0, LANES - n_g))).astype(BF16)
            bif = jnp.pad(jnp.concatenate([ml_b_if_f[j], ml_b_if_b[j]]), (0, LANES - n_g)).reshape(1, LANES)
            ac, q, kt, v, gc, gr = _ml_qkv(a, n_batch, n_t, ml_conv_w[j], ml_conv_b[j].reshape(1, MLSTM_INNER),
                                           ml_w_q[j].astype(BF16), ml_w_k[j].transpose(0, 2, 1).astype(BF16),
                                           ml_w_v[j].astype(BF16), wif[jnp.array([0, 2])],
                                           wif[1].transpose(0, 2, 1), bif)
            gc = gc.reshape(n_batch, seq, 4, MLSTM_HEADS).transpose(0, 3, 1, 2)
            gr = gr.reshape(4, MLSTM_HEADS, n_batch, seq).transpose(2, 1, 0, 3)
            hs = _ml_scan(q, kt, v, gc, gr, n_batch, seq)
            mix = _ml_mix
            mix_rows, mix_consts, w_out = [hs, o_pre, ac], [ml_skip[j].reshape(1, MLSTM_INNER)], ml_w_out[j]
        h = _mix_ffn(mix, mix_rows, mix_consts, w_out.astype(BF16), stream, n_batch, n_t, off, g1, gain2, sh2, sc2,
                     g2, final_g2, ffn_w1[i].astype(BF16), ffn_w2[i].astype(BF16), final_norm=last)
        stream = (h,)
    return h.reshape(n_batch, t_lat, D_MODEL)
```

```python
import functools
import itertools

import jax
import jax.numpy as jnp
from jax import lax
from jax.experimental import pallas as pl
from jax.experimental.pallas import tpu as pltpu

F32 = jnp.float32
BF16 = jnp.bfloat16

D_MODEL = 1024
DEPTH = 4
GRID_W = 64
CTX_LEN = 256
NORM_EPS = 1e-6
D_FF = 4 * D_MODEL
RET_HEADS = 4
RET_DK = D_MODEL // RET_HEADS
RET_DV = 2 * D_MODEL // RET_HEADS
ROPE_BASE = 10000.0
GLA_HEADS = 4
GLA_DK = D_MODEL // 2 // GLA_HEADS
GLA_DV = D_MODEL // GLA_HEADS
GLA_RANK = 16
GLA_TAU = 16.0
MLSTM_INNER = 2 * D_MODEL
MLSTM_HEADS = 4
MLSTM_DH = MLSTM_INNER // MLSTM_HEADS

TM = CTX_LEN
LANES = 128
BF16_ROWS = 16
RET_CHUNK = 256
GLA_CHUNK = 64
MLSTM_CHUNK = 256
VMEM_LIMIT = 56 * 1024 * 1024

_NT = (((1,), (1,)), ((), ()))
_TN = (((0,), (0,)), ((), ()))


def _dot(a, b):
    return jnp.dot(a, b, preferred_element_type=F32)


def _silu(x):
    return x * jax.nn.sigmoid(x)


def _log_sigmoid(z):
    return jnp.minimum(z, 0.0) - jnp.log1p(jnp.exp(-jnp.abs(z)))


def _modnorm(x, g, shift, scale):
    y = x * lax.rsqrt(jnp.mean(x * x, axis=-1, keepdims=True) + NORM_EPS) * g
    return y * (1.0 + scale) + shift


def _params(n_axes):
    return pltpu.CompilerParams(dimension_semantics=("arbitrary",) * n_axes,
                                vmem_limit_bytes=VMEM_LIMIT)


def _const_spec(shape):
    nd = len(shape)
    return pl.BlockSpec(shape, lambda *_: (0,) * nd, pipeline_mode=pl.Buffered(1))


def _row_spec(width, n_tiles, off):
    return pl.BlockSpec((TM, width), lambda b, t: (b * n_tiles + t + off, 0))


def _stream_specs(stream, n_t, off):
    if len(stream) == 1:
        return [_row_spec(D_MODEL, n_t, off)]
    assert off == 0
    return [pl.BlockSpec((TM, D_MODEL), lambda b, t: (b * (n_t - 1) + jnp.maximum(t - 1, 0), 0)),
            pl.BlockSpec((TM, D_MODEL), lambda b, t: (b, 0))]


def _stream_tile(refs):
    if len(refs) == 1:
        return refs[0][...]
    lat_ref, ctx_ref = refs
    return jnp.where(pl.program_id(1) == 0, ctx_ref[...], lat_ref[...])


def _mod_spec(n_batch, off, has_ctx):
    if has_ctx:
        return pl.BlockSpec((1, 1, D_MODEL), lambda b, t: (jnp.where(t + off == 0, n_batch, b), 0, 0))
    return pl.BlockSpec((1, 1, D_MODEL), lambda b, t: (b, 0, 0))


def _ada_kernel(c_ref, w_ref, b_ref, o_ref):
    s = _silu(c_ref[...]).astype(BF16)
    o_ref[0] = _dot(s, w_ref[0].astype(BF16)) + b_ref[0]


def _ada(cvec, ada_w, ada_b):
    rows = cvec.shape[0]
    tn = 512
    return pl.pallas_call(
        _ada_kernel,
        grid=(DEPTH, 6 * D_MODEL // tn),
        in_specs=[pl.BlockSpec((rows, D_MODEL), lambda l, j: (0, 0)),
                  pl.BlockSpec((1, D_MODEL, tn), lambda l, j: (l, 0, j)),
                  pl.BlockSpec((1, 1, tn), lambda l, j: (l, 0, j))],
        out_specs=pl.BlockSpec((1, rows, tn), lambda l, j: (l, 0, j)),
        out_shape=jax.ShapeDtypeStruct((DEPTH, rows, 6 * D_MODEL), F32),
        compiler_params=_params(2),
        name="adaln",
    )(cvec, ada_w, ada_b.reshape(DEPTH, 1, 6 * D_MODEL))


def _gated_mix(y_ref, g_ref, ng_ref, w_ref, *, heads, dv):
    acc = jnp.zeros((TM, D_MODEL), F32)
    for hd in range(heads):
        lo = hd * dv
        y = y_ref[:, lo:lo + dv]
        yn = y * lax.rsqrt(jnp.mean(y * y, axis=-1, keepdims=True) + NORM_EPS) * ng_ref[hd:hd + 1, :]
        z = (_silu(g_ref[:, lo:lo + dv].astype(F32)) * yn).astype(BF16)
        acc = acc + _dot(z, w_ref[lo:lo + dv, :])
    return acc


def _ml_mix(hs_ref, op_ref, ac_ref, skip_ref, w_ref):
    cw = 1024
    acc = jnp.zeros((TM, D_MODEL), F32)
    for j in range(MLSTM_INNER // cw):
        sl = slice(j * cw, (j + 1) * cw)
        y = (jax.nn.sigmoid(op_ref[:, sl].astype(F32)) * hs_ref[:, sl].astype(F32)
             + skip_ref[:, sl] * ac_ref[:, sl].astype(F32))
        acc = acc + _dot(y.astype(BF16), w_ref[sl, :])
    return acc


def _mix_ffn_kernel(*refs, n_mix, n_stream, mix, final_norm):
    mix_refs, wo_ref, stream_refs = refs[:n_mix], refs[n_mix], refs[n_mix + 1:n_mix + 1 + n_stream]
    (gate1_ref, g_ref, sh_ref, sc_ref, gate2_ref, fg_ref, w1_ref, w2_ref, o_ref) = refs[n_mix + 1 + n_stream:]
    x = _stream_tile(stream_refs) + gate1_ref[0] * mix(*mix_refs, wo_ref)
    xn = _modnorm(x, g_ref[...], sh_ref[0], sc_ref[0]).astype(BF16)
    fc = 1024
    acc = jnp.zeros((TM, D_MODEL), F32)
    for j in range(D_FF // fc):
        hid = _dot(xn, w1_ref[:, j * fc:(j + 1) * fc])
        hid = jnp.square(jnp.maximum(hid, 0.0)).astype(BF16)
        acc = acc + _dot(hid, w2_ref[j * fc:(j + 1) * fc, :])
    o = x + gate2_ref[0] * acc
    if final_norm:
        o = o * lax.rsqrt(jnp.mean(o * o, axis=-1, keepdims=True) + NORM_EPS) * fg_ref[...]
    o_ref[...] = o


def _mix_ffn(mix, mix_rows, mix_consts, w_out, stream, n_batch, n_t, off, gate1, gain, shift, scale, gate2,
             final_g, w1, w2, *, final_norm):
    n_out = n_t - off
    ms = _mod_spec(n_batch, off, off == 0)
    in_specs = ([_row_spec(a.shape[1], n_t, off) for a in mix_rows] + [_const_spec(a.shape) for a in mix_consts]
                + [_const_spec(w_out.shape)] + _stream_specs(stream, n_t, off)
                + [ms, _const_spec((1, D_MODEL)), ms, ms, ms,
                   _const_spec((1, D_MODEL)), _const_spec((D_MODEL, D_FF)), _const_spec((D_FF, D_MODEL))])
    return pl.pallas_call(
        functools.partial(_mix_ffn_kernel, n_mix=len(mix_rows) + len(mix_consts), n_stream=len(stream), mix=mix,
                          final_norm=final_norm),
        grid=(n_batch, n_out),
        in_specs=in_specs,
        out_specs=_row_spec(D_MODEL, n_out, 0),
        out_shape=jax.ShapeDtypeStruct((n_batch * n_out * TM, D_MODEL), F32),
        compiler_params=_params(2),
        name="mix_ffn",
    )(*mix_rows, *mix_consts, w_out, *stream, gate1, gain, shift, scale, gate2, final_g, w1, w2)


def _ret_proj_kernel(*refs, n_stream):
    (g_ref, sh_ref, sc_ref, cos_ref, sin_ref, cost_ref, sint_ref, w_ref, wkt_ref,
     q_ref, kt_ref, v_ref, gg_ref) = refs[n_stream:]
    xn = _modnorm(_stream_tile(refs[:n_stream]), g_ref[...], sh_ref[0], sc_ref[0]).astype(BF16)
    cos, sin = cos_ref[...], sin_ref[...]
    cos_t, sin_t = cost_ref[...], sint_ref[...]
    half = RET_DK // 2
    qk_w, v_w = RET_HEADS * RET_DK, RET_HEADS * RET_DV
    for hd in range(RET_HEADS):
        lo = hd * RET_DK
        u = _dot(xn, w_ref[:, lo:lo + RET_DK])
        u1, u2 = u[:, :half], u[:, half:]
        q_ref[:, lo:lo + RET_DK] = jnp.concatenate([u1 * cos - u2 * sin, u1 * sin + u2 * cos], axis=1).astype(BF16)
        ut = lax.dot_general(wkt_ref[lo:lo + RET_DK, :], xn, _NT, preferred_element_type=F32) * RET_DK ** -0.5
        u1, u2 = ut[:half], ut[half:]
        kt_ref[lo:lo + RET_DK, :] = jnp.concatenate([u1 * cos_t - u2 * sin_t, u1 * sin_t + u2 * cos_t],
                                                    axis=0).astype(BF16)
    for hd in range(RET_HEADS):
        lo = hd * RET_DV
        v_ref[:, lo:lo + RET_DV] = _dot(xn, w_ref[:, qk_w + lo:qk_w + lo + RET_DV]).astype(BF16)
        gg_ref[:, lo:lo + RET_DV] = _dot(xn, w_ref[:, qk_w + v_w + lo:qk_w + v_w + lo + RET_DV]).astype(BF16)


def _ret_proj(stream, n_batch, n_t, gain, shift, scale, cos, sin, w_qvg, wkt):
    ms = _mod_spec(n_batch, 0, True)
    rows = n_batch * n_t * TM
    qk_w, v_w = RET_HEADS * RET_DK, RET_HEADS * RET_DV
    tab = pl.BlockSpec((TM, RET_DK // 2), lambda b, t: (t, 0))
    tab_t = pl.BlockSpec((RET_DK // 2, TM), lambda b, t: (0, t))
    return pl.pallas_call(
        functools.partial(_ret_proj_kernel, n_stream=len(stream)),
        grid=(n_batch, n_t),
        in_specs=_stream_specs(stream, n_t, 0) + [_const_spec((1, D_MODEL)), ms, ms, tab, tab, tab_t, tab_t,
                                                  _const_spec(w_qvg.shape), _const_spec(wkt.shape)],
        out_specs=[_row_spec(qk_w, n_t, 0), pl.BlockSpec((qk_w, TM), lambda b, t: (0, b * n_t + t)),
                   _row_spec(v_w, n_t, 0), _row_spec(v_w, n_t, 0)],
        out_shape=[jax.ShapeDtypeStruct((rows, qk_w), BF16), jax.ShapeDtypeStruct((qk_w, rows), BF16),
                   jax.ShapeDtypeStruct((rows, v_w), BF16), jax.ShapeDtypeStruct((rows, v_w), BF16)],
        compiler_params=_params(2),
        name="ret_proj",
    )(*stream, gain, shift, scale, cos, sin, cos.T, sin.T, w_qvg, wkt)


def _block_starts(step, n_blocks):
    return step, jnp.where(step == 0, 0, n_blocks - step)


def _ret_scan_kernel(ld_ref, q_ref, kt_ref, v_ref, o_ref, sf_ref, sb_ref, dmf_ref, dmb_ref, *, chunk, seq):
    L = chunk
    hd = pl.program_id(1)
    n_blocks = seq // L
    li = lax.broadcasted_iota(jnp.int32, (L, L), 0)
    si = lax.broadcasted_iota(jnp.int32, (L, L), 1)
    row = lax.broadcasted_iota(jnp.int32, (L, 1), 0).astype(F32)
    col = lax.broadcasted_iota(jnp.int32, (1, L), 1).astype(F32)
    consts = []
    for direction, dm_ref in ((0, dmf_ref), (1, dmb_ref)):
        lg = -jnp.exp(jnp.full((1, 1), ld_ref[direction, hd], F32))
        if direction == 0:
            dist = li - si
            q_dec = jnp.exp((row + 1.0) * lg)
            k_dec = jnp.exp((L - 1.0 - col) * lg)
        else:
            dist = si - li
            q_dec = jnp.exp((L - row) * lg)
            k_dec = jnp.exp(col * lg)
        dm_ref[...] = jnp.where(dist >= 0, jnp.exp(jnp.maximum(dist, 0).astype(F32) * lg), 0.0)
        consts.append((q_dec, k_dec, jnp.exp(float(L) * lg)))
    sf_ref[...] = jnp.zeros_like(sf_ref)
    sb_ref[...] = jnp.zeros_like(sb_ref)
    o_ref[...] = jnp.zeros_like(o_ref)

    def chunk_step(direction, r0, key, prev_key, pub):
        s_ref, dm_ref = (sf_ref, dmf_ref) if direction == 0 else (sb_ref, dmb_ref)
        q_dec, k_dec, s_dec = consts[direction]
        qc = q_ref[pl.ds(r0, L), :]
        ktc = kt_ref[:, pl.ds(r0, L)]
        vc = v_ref[pl.ds(r0, L), :]
        att = _dot(qc, ktc) * dm_ref[...]
        yield
        update = _dot((ktc.astype(F32) * k_dec).astype(BF16), vc)
        yield
        state = s_ref[...] if prev_key is None else pub[prev_key]["state"]
        pub[key] = {"state": s_dec * state + update}
        carried = _dot(qc, state.astype(BF16))
        yield
        pub[key]["out"] = _dot(att.astype(BF16), vc) + q_dec * carried

    def run(blocks_f, blocks_b):
        pub, gens, starts = {}, [], {}
        for n in range(len(blocks_f)):
            for direction, blocks in ((0, blocks_f), (1, blocks_b)):
                starts[direction, n] = pl.multiple_of(blocks[n] * L, L)
                gens.append(chunk_step(direction, starts[direction, n], (direction, n),
                                       (direction, n - 1) if n else None, pub))
        for _ in itertools.zip_longest(*gens):
            pass
        last = len(blocks_f) - 1
        sf_ref[...] = pub[0, last]["state"]
        sb_ref[...] = pub[1, last]["state"]
        for key in sorted(pub, key=lambda kn: (kn[1], kn[0])):
            o_ref[pl.ds(starts[key], L), :] += pub[key]["out"]

    n_lat = n_blocks - 1
    assert n_lat % 2 == 0
    run([0], [0])

    def body(j, carry):
        run([1 + 2 * j, 2 + 2 * j], [n_blocks - 1 - 2 * j, n_blocks - 2 - 2 * j])
        return carry

    lax.fori_loop(0, n_lat // 2, body, 0)


def _ret_scan(log_decay, q, kt, v, n_batch, seq):
    assert RET_CHUNK == CTX_LEN
    return pl.pallas_call(
        functools.partial(_ret_scan_kernel, chunk=RET_CHUNK, seq=seq),
        grid=(n_batch, RET_HEADS),
        in_specs=[pl.BlockSpec(memory_space=pltpu.SMEM),
                  pl.BlockSpec((seq, RET_DK), lambda b, h: (b, h)),
                  pl.BlockSpec((RET_DK, seq), lambda b, h: (h, b)),
                  pl.BlockSpec((seq, RET_DV), lambda b, h: (b, h))],
        out_specs=pl.BlockSpec((seq, RET_DV), lambda b, h: (b, h)),
        out_shape=jax.ShapeDtypeStruct((n_batch * seq, RET_HEADS * RET_DV), F32),
        scratch_shapes=[pltpu.VMEM((RET_DK, RET_DV), F32), pltpu.VMEM((RET_DK, RET_DV), F32),
                        pltpu.VMEM((RET_CHUNK, RET_CHUNK), F32), pltpu.VMEM((RET_CHUNK, RET_CHUNK), F32)],
        compiler_params=_params(2),
        name="ret_scan",
    )(log_decay, q, kt, v)


def _gla_proj_kernel(x_ref, g_ref, sh_ref, sc_ref, w_ref, wa1_ref, wa2_ref, ba_ref,
                     q_ref, k_ref, v_ref, r_ref, ga_ref):
    xn = _modnorm(x_ref[...], g_ref[...], sh_ref[0], sc_ref[0]).astype(BF16)
    kw, vw = GLA_HEADS * GLA_DK, GLA_HEADS * GLA_DV
    q_ref[...] = (_dot(xn, w_ref[:, :kw]) * GLA_DK ** -0.5).astype(BF16)
    k_ref[...] = _dot(xn, w_ref[:, kw:2 * kw]).astype(BF16)
    v_ref[...] = _dot(xn, w_ref[:, 2 * kw:2 * kw + vw]).astype(BF16)
    r_ref[...] = _dot(xn, w_ref[:, 2 * kw + vw:]).astype(BF16)
    low = _dot(xn, wa1_ref[...]).astype(BF16)
    z = _dot(low, wa2_ref[...]) + ba_ref[...]
    ga_ref[...] = _log_sigmoid(z) / GLA_TAU


def _gla_proj(h, n_batch, n_t, gain, shift, scale, w_in, wa1, wa2, ba):
    ms = _mod_spec(n_batch, 0, True)
    rows = h.shape[0]
    kw, vw = GLA_HEADS * GLA_DK, GLA_HEADS * GLA_DV
    return pl.pallas_call(
        _gla_proj_kernel,
        grid=(n_batch, n_t),
        in_specs=[_row_spec(D_MODEL, n_t, 0), _const_spec((1, D_MODEL)), ms, ms, _const_spec(w_in.shape),
                  _const_spec(wa1.shape), _const_spec(wa2.shape), _const_spec(ba.shape)],
        out_specs=[_row_spec(kw, n_t, 0), _row_spec(kw, n_t, 0), _row_spec(vw, n_t, 0), _row_spec(vw, n_t, 0),
                   _row_spec(2 * kw, n_t, 0)],
        out_shape=[jax.ShapeDtypeStruct((rows, kw), BF16), jax.ShapeDtypeStruct((rows, kw), BF16),
                   jax.ShapeDtypeStruct((rows, vw), BF16), jax.ShapeDtypeStruct((rows, vw), BF16),
                   jax.ShapeDtypeStruct((rows, 2 * kw), F32)],
        compiler_params=_params(2),
        name="gla_proj",
    )(h, gain, shift, scale, w_in, wa1, wa2, ba)


def _gla_scan_kernel(q_ref, k_ref, v_ref, gf_ref, gb_ref, o_ref, sf_ref, sb_ref, trif_ref, trib_ref,
                     *, sub, block, seq):
    C, blk = sub, block
    n_sub, n_blocks = blk // C, seq // blk
    li = lax.broadcasted_iota(jnp.int32, (blk, blk), 0)
    si = lax.broadcasted_iota(jnp.int32, (blk, blk), 1)
    trif_ref[...] = (li >= si).astype(BF16)
    trib_ref[...] = (si >= li).astype(BF16)
    dl = lax.broadcasted_iota(jnp.int32, (C, C), 0)
    ds = lax.broadcasted_iota(jnp.int32, (C, C), 1)
    diag_masks = (dl >= ds, ds >= dl)
    eye = (lax.broadcasted_iota(jnp.int32, (GLA_DK, GLA_DK), 0)
           == lax.broadcasted_iota(jnp.int32, (GLA_DK, GLA_DK), 1))
    sf_ref[...] = jnp.zeros_like(sf_ref)
    sb_ref[...] = jnp.zeros_like(sb_ref)
    o_ref[...] = jnp.zeros_like(o_ref)

    def block_step(direction, r0, key, prev_key, pub):
        fwd = direction == 0
        gate_ref, tri_ref, s_ref = (gf_ref, trif_ref, sf_ref) if fwd else (gb_ref, trib_ref, sb_ref)
        g = gate_ref[pl.ds(r0, blk), :]
        g_hi = g.astype(BF16)
        rem = g - g_hi.astype(F32)
        g_mid = rem.astype(BF16)
        g_lo = (rem - g_mid.astype(F32)).astype(BF16)
        parts = _dot(tri_ref[...], jnp.concatenate([g_hi, g_mid, g_lo], axis=1))
        yield
        cum = parts[:, :GLA_DK] + parts[:, GLA_DK:2 * GLA_DK] + parts[:, 2 * GLA_DK:]
        qf = q_ref[pl.ds(r0, blk), :].astype(F32)
        kf = k_ref[pl.ds(r0, blk), :].astype(F32)
        vb = v_ref[pl.ds(r0, blk), :]
        end_row = blk - 1 if fwd else 0
        c_end = cum[end_row:end_row + 1]
        state = s_ref[...] if prev_key is None else pub[prev_key]["state"]
        q_carry = (qf * jnp.exp(cum)).astype(BF16)
        kd = (kf * jnp.exp(c_end - cum)).astype(BF16)
        e_col = jnp.sum(jnp.where(eye, jnp.exp(c_end), 0.0), axis=1, keepdims=True)
        pub[key] = {"state": e_col * state + lax.dot_general(kd, vb, _TN, preferred_element_type=F32)}
        att_rows = {}
        for i in range(n_sub):
            yield
            lo = i * C if fwd else blk - (i + 1) * C
            rows = slice(lo, lo + C)
            mid = lo + (C // 2 if fwd else C - 1 - C // 2)
            cq = cum[rows]
            c_mid = cum[mid:mid + 1]
            qd = (qf[rows] * jnp.exp(cq - c_mid)).astype(BF16)
            kg = (kf[rows] * jnp.exp(c_mid - cq)).astype(BF16)
            diag = jnp.where(diag_masks[direction], lax.dot_general(qd, kg, _NT, preferred_element_type=F32), 0.0)
            earlier = None
            if i > 0:
                prev = slice(0, lo) if fwd else slice(lo + C, blk)
                edge = lo - 1 if fwd else lo + C
                c_edge = cum[edge:edge + 1]
                ql = (qf[rows] * jnp.exp(cq - c_edge)).astype(BF16)
                kk = (kf[prev] * jnp.exp(c_edge - cum[prev])).astype(BF16)
                earlier = lax.dot_general(ql, kk, _NT, preferred_element_type=F32)
            unseen = jnp.zeros((C, blk - (i + 1) * C), F32) if i < n_sub - 1 else None
            cols = [earlier, diag, unseen] if fwd else [unseen, diag, earlier]
            att_rows[lo] = jnp.concatenate([c for c in cols if c is not None], axis=1).astype(BF16)
        yield
        att = jnp.concatenate([att_rows[lo] for lo in sorted(att_rows)], axis=0)
        out = _dot(jnp.concatenate([att, q_carry], axis=1), jnp.concatenate([vb, state.astype(BF16)], axis=0))
        pub[key]["out"] = out

    def run(blocks_f, blocks_b):
        pub, gens, starts = {}, [], {}
        for n in range(len(blocks_f)):
            for direction, blocks in ((0, blocks_f), (1, blocks_b)):
                starts[direction, n] = pl.multiple_of(blocks[n] * blk, blk)
                gens.append(block_step(direction, starts[direction, n], (direction, n),
                                       (direction, n - 1) if n else None, pub))
        for _ in itertools.zip_longest(*gens):
            pass
        last = len(blocks_f) - 1
        sf_ref[...] = pub[0, last]["state"]
        sb_ref[...] = pub[1, last]["state"]
        for key in sorted(pub, key=lambda kn: (kn[1], kn[0])):
            o_ref[pl.ds(starts[key], blk), :] += pub[key]["out"]

    n_lat = n_blocks - 1
    assert n_lat % 2 == 0
    run([0], [0])

    def body(j, carry):
        run([1 + 2 * j, 2 + 2 * j], [n_blocks - 1 - 2 * j, n_blocks - 2 - 2 * j])
        return carry

    lax.fori_loop(0, n_lat // 2, body, 0)


def _gla_scan(q, k, v, ga, n_batch, seq):
    return pl.pallas_call(
        functools.partial(_gla_scan_kernel, sub=GLA_CHUNK, block=CTX_LEN, seq=seq),
        grid=(n_batch, GLA_HEADS),
        in_specs=[pl.BlockSpec((seq, GLA_DK), lambda b, h: (b, h)),
                  pl.BlockSpec((seq, GLA_DK), lambda b, h: (b, h)),
                  pl.BlockSpec((seq, GLA_DV), lambda b, h: (b, h)),
                  pl.BlockSpec((seq, GLA_DK), lambda b, h: (b, h)),
                  pl.BlockSpec((seq, GLA_DK), lambda b, h: (b, GLA_HEADS + h))],
        out_specs=pl.BlockSpec((seq, GLA_DV), lambda b, h: (b, h)),
        out_shape=jax.ShapeDtypeStruct((n_batch * seq, GLA_HEADS * GLA_DV), F32),
        scratch_shapes=[pltpu.VMEM((GLA_DK, GLA_DV), F32), pltpu.VMEM((GLA_DK, GLA_DV), F32),
                        pltpu.VMEM((CTX_LEN, CTX_LEN), BF16), pltpu.VMEM((CTX_LEN, CTX_LEN), BF16)],
        compiler_params=_params(2),
        name="gla_scan",
    )(q, k, v, ga, ga)


def _ml_proj_kernel(x_ref, g_ref, sh_ref, sc_ref, w_ref, a_ref, o_ref):
    xn = _modnorm(x_ref[...], g_ref[...], sh_ref[0], sc_ref[0]).astype(BF16)
    cw = 1024
    for j in range(MLSTM_INNER // cw):
        a_ref[:, j * cw:(j + 1) * cw] = _dot(xn, w_ref[:, j * cw:(j + 1) * cw]).astype(BF16)
        o_ref[:, j * cw:(j + 1) * cw] = _dot(
            xn, w_ref[:, MLSTM_INNER + j * cw:MLSTM_INNER + (j + 1) * cw]).astype(BF16)


def _ml_proj(h, n_batch, n_t, gain, shift, scale, w_in):
    ms = _mod_spec(n_batch, 0, True)
    rows = h.shape[0]
    return pl.pallas_call(
        _ml_proj_kernel,
        grid=(n_batch, n_t),
        in_specs=[_row_spec(D_MODEL, n_t, 0), _const_spec((1, D_MODEL)), ms, ms, _const_spec(w_in.shape)],
        out_specs=[_row_spec(MLSTM_INNER, n_t, 0), _row_spec(MLSTM_INNER, n_t, 0)],
        out_shape=[jax.ShapeDtypeStruct((rows, MLSTM_INNER), BF16)] * 2,
        compiler_params=_params(2),
        name="ml_proj",
    )(h, gain, shift, scale, w_in)


def _ml_qkv_kernel(a_ref, prev_ref, next_ref, cw_ref, cb_ref, wq_ref, wkt_ref, wv_ref, wif_ref, wifkt_ref, bif_ref,
                   ac_ref, q_ref, kt_ref, v_ref, gc_ref, gr_ref, *, n_t):
    t = pl.program_id(1)
    has_prev = jnp.where(t >= 2, 1.0, 0.0).astype(F32)
    has_next = jnp.where(jnp.logical_and(t >= 1, t <= n_t - 2), 1.0, 0.0).astype(F32)
    row = lax.broadcasted_iota(jnp.int32, (TM, 1), 0)
    dh = MLSTM_DH
    pre = jnp.zeros((TM, LANES), F32)
    pre_t = jnp.zeros((LANES, TM), F32)
    for hd in range(MLSTM_HEADS):
        lo = hd * dh
        a_bf = a_ref[:, lo:lo + dh]
        a = a_bf.astype(F32)
        before = prev_ref[:, lo:lo + dh].astype(F32)[BF16_ROWS - 1:BF16_ROWS] * has_prev
        after = next_ref[:, lo:lo + dh].astype(F32)[0:1] * has_next
        up = jnp.where(row == 0, before, pltpu.roll(a, 1, axis=0))
        dn = jnp.where(row == TM - 1, after, pltpu.roll(a, TM - 1, axis=0))
        conv = (cw_ref[0:1, lo:lo + dh] * up + cw_ref[1:2, lo:lo + dh] * a + cw_ref[2:3, lo:lo + dh] * dn
                + cb_ref[:, lo:lo + dh])
        ac = _silu(conv).astype(BF16)
        ac_ref[:, lo:lo + dh] = ac
        q = _dot(ac, wq_ref[hd]).astype(BF16)
        kt = (lax.dot_general(wkt_ref[hd], ac, _NT, preferred_element_type=F32) * dh ** -0.5).astype(BF16)
        v = _dot(a_bf, wv_ref[hd]).astype(BF16)
        q_ref[:, lo:lo + dh] = q
        kt_ref[lo:lo + dh, :] = kt
        v_ref[:, lo:lo + dh] = v
        pre = pre + _dot(q, wif_ref[0, hd]) + _dot(v, wif_ref[1, hd])
        pre_t = pre_t + _dot(wifkt_ref[hd], kt)
    pre = pre + pre_t.T + bif_ref[...]
    lane = lax.broadcasted_iota(jnp.int32, (TM, LANES), 1)
    is_forget = (lane % (2 * MLSTM_HEADS)) >= MLSTM_HEADS
    gates = jnp.where(is_forget, _log_sigmoid(pre), pre)
    n_g = 4 * MLSTM_HEADS
    gc_ref[...] = gates[:, :n_g]
    gr_ref[...] = gates.T[:n_g, :]


def _ml_qkv(a, n_batch, n_t, conv_w, conv_b, wq, wkt, wv, wif, wifkt, bif):
    rows = a.shape[0]
    per_tile = TM // BF16_ROWS
    n_halo = rows // BF16_ROWS
    n_g = 4 * MLSTM_HEADS
    wide = _row_spec(MLSTM_INNER, n_t, 0)
    prev_spec = pl.BlockSpec((BF16_ROWS, MLSTM_INNER),
                             lambda b, t: (jnp.maximum((b * n_t + t) * per_tile - 1, 0), 0))
    next_spec = pl.BlockSpec((BF16_ROWS, MLSTM_INNER),
                             lambda b, t: (jnp.minimum((b * n_t + t + 1) * per_tile, n_halo - 1), 0))
    return pl.pallas_call(
        functools.partial(_ml_qkv_kernel, n_t=n_t),
        grid=(n_batch, n_t),
        in_specs=[wide, prev_spec, next_spec, _const_spec(conv_w.shape), _const_spec(conv_b.shape),
                  _const_spec(wq.shape), _const_spec(wkt.shape), _const_spec(wv.shape),
                  _const_spec(wif.shape), _const_spec(wifkt.shape), _const_spec(bif.shape)],
        out_specs=[wide, wide, pl.BlockSpec((MLSTM_INNER, TM), lambda b, t: (0, b * n_t + t)), wide,
                   _row_spec(n_g, n_t, 0), pl.BlockSpec((n_g, TM), lambda b, t: (0, b * n_t + t))],
        out_shape=[jax.ShapeDtypeStruct((rows, MLSTM_INNER), BF16)] * 2
        + [jax.ShapeDtypeStruct((MLSTM_INNER, rows), BF16), jax.ShapeDtypeStruct((rows, MLSTM_INNER), BF16),
           jax.ShapeDtypeStruct((rows, n_g), F32), jax.ShapeDtypeStruct((n_g, rows), F32)],
        compiler_params=_params(2),
        name="ml_qkv",
    )(a, a, a, conv_w, conv_b, wq, wkt, wv, wif, wifkt, bif)


def _ml_scan_kernel(q_ref, kt_ref, v_ref, gc_ref, gr_ref, o_ref, ctf_ref, ctb_ref, *, chunk, seq):
    L = chunk
    dv = v_ref.shape[1]
    n_blocks = seq // L
    li = lax.broadcasted_iota(jnp.int32, (L, L), 0)
    si = lax.broadcasted_iota(jnp.int32, (L, L), 1)
    ones = jnp.ones((L, LANES), BF16)
    for ref in (ctf_ref, ctb_ref, o_ref):
        ref[...] = jnp.zeros_like(ref)

    def chunk_step(direction, r0, m_prev, result):
        ct_ref = ctf_ref if direction == 0 else ctb_ref
        col_i, col_f = (0, 1) if direction == 0 else (2, 3)
        seen = (li >= si) if direction == 0 else (si >= li)
        seen_t = (li <= si) if direction == 0 else (si <= li)
        end_row = L - 1 if direction == 0 else 0
        qc = q_ref[pl.ds(r0, L), :]
        ktc = kt_ref[:, pl.ds(r0, L)]
        vc = v_ref[pl.ds(r0, L), :]
        state = ct_ref[...]
        carried = _dot(qc, state.astype(BF16))
        yield
        qk = _dot(qc, ktc)
        yield
        f_col = gc_ref[pl.ds(r0, L), col_f:col_f + 1]
        f_row = gr_ref[col_f:col_f + 1, pl.ds(r0, L)]
        i_row = gr_ref[col_i:col_i + 1, pl.ds(r0, L)]
        b_col = jnp.sum(jnp.where(seen, f_row, 0.0), axis=1, keepdims=True)
        b_row = jnp.sum(jnp.where(seen_t, f_col, 0.0), axis=0, keepdims=True)
        b_end = b_col[end_row:end_row + 1]
        d_end = b_end - b_row + i_row
        m_new = jnp.maximum(b_end + m_prev, jnp.max(d_end, axis=1, keepdims=True))
        a_prev = jnp.exp(b_end + m_prev - m_new)
        ktw = (ktc.astype(F32) * jnp.exp(d_end - m_new)).astype(BF16)
        ct_ref[:, :dv] = a_prev * state[:, :dv] + _dot(ktw, vc)
        ct_ref[:, dv:] = a_prev * state[:, dv:] + _dot(ktw, ones)
        yield
        d_in = jnp.where(seen, b_col - b_row + i_row, -jnp.inf)
        g = b_col + m_prev
        m_t = jnp.maximum(g, jnp.max(d_in, axis=1, keepdims=True))
        w_prev = jnp.exp(g - m_t)
        s = qk * jnp.exp(d_in - m_t)
        num = _dot(s.astype(BF16), vc) + w_prev * carried[:, :dv]
        den = jnp.sum(s, axis=1, keepdims=True) + w_prev * carried[:, dv:dv + 1]
        out = num * (1.0 / jnp.maximum(jnp.abs(den), jnp.exp(-m_t)))
        o_ref[pl.ds(r0, L), :] = (o_ref[pl.ds(r0, L), :].astype(F32) + out).astype(o_ref.dtype)
        result[direction] = m_new

    def body(step, carry):
        blk_f, blk_b = _block_starts(step, n_blocks)
        result = {}
        for _ in itertools.zip_longest(
                chunk_step(0, pl.multiple_of(blk_f * L, L), carry[0], result),
                chunk_step(1, pl.multiple_of(blk_b * L, L), carry[1], result)):
            pass
        return result[0], result[1]

    lax.fori_loop(0, n_blocks, body, (jnp.zeros((1, 1), F32), jnp.zeros((1, 1), F32)))


def _ml_scan(q, kt, v, gc, gr, n_batch, seq):
    assert MLSTM_CHUNK == CTX_LEN
    dh = MLSTM_DH
    blk = pl.BlockSpec((seq, dh), lambda b, h: (b, h))
    return pl.pallas_call(
        functools.partial(_ml_scan_kernel, chunk=MLSTM_CHUNK, seq=seq),
        grid=(n_batch, MLSTM_HEADS),
        in_specs=[blk, pl.BlockSpec((dh, seq), lambda b, h: (h, b)), blk,
                  pl.BlockSpec((None, None, seq, 4), lambda b, h: (b, h, 0, 0)),
                  pl.BlockSpec((None, None, 4, seq), lambda b, h: (b, h, 0, 0))],
        out_specs=blk,
        out_shape=jax.ShapeDtypeStruct((n_batch * seq, MLSTM_INNER), BF16),
        scratch_shapes=[pltpu.VMEM((dh, dh + LANES), F32), pltpu.VMEM((dh, dh + LANES), F32)],
        compiler_params=_params(2),
        name="ml_scan",
    )(q, kt, v, gc, gr)


def _rope_tables(t_lat):
    n_f = RET_DK // 4
    pos = jnp.arange(t_lat)
    inv = ROPE_BASE ** (-jnp.arange(n_f, dtype=F32) / n_f)
    ang = jnp.concatenate([(pos // GRID_W).astype(F32)[:, None] * inv,
                           (pos % GRID_W).astype(F32)[:, None] * inv], axis=-1)
    cos = jnp.concatenate([jnp.ones((CTX_LEN, 2 * n_f), F32), jnp.cos(ang)], axis=0)
    sin = jnp.concatenate([jnp.zeros((CTX_LEN, 2 * n_f), F32), jnp.sin(ang)], axis=0)
    return cos, sin


def _ret_weights(w_in):
    qk_w = RET_HEADS * RET_DK
    perm = jnp.concatenate([jnp.arange(0, RET_DK, 2), jnp.arange(1, RET_DK, 2)])
    head_cols = (jnp.arange(RET_HEADS)[:, None] * RET_DK + perm[None, :]).reshape(-1)
    w_qvg = jnp.concatenate([w_in[:, head_cols], w_in[:, 2 * qk_w:]], axis=1).astype(BF16)
    wkt = w_in[:, qk_w + head_cols].T.astype(BF16)
    return w_qvg, wkt


def kernel(x, c, ctx, c_ctx, ada_w, ada_b, norm1_g, norm2_g, ffn_w1, ffn_w2, final_g, ret_w_in, ret_log_decay_f, ret_log_decay_b, ret_norm_g, ret_w_out, gla_w_in, gla_wa1_f, gla_wa2_f, gla_ba_f, gla_wa1_b, gla_wa2_b, gla_ba_b, gla_norm_g, gla_w_out, ml_w_in, ml_conv_w, ml_conv_b, ml_w_q, ml_w_k, ml_w_v, ml_w_if_f, ml_b_if_f, ml_w_if_b, ml_b_if_b, ml_skip, ml_w_out):
    n_batch, t_lat, _ = x.shape
    assert ctx.shape[1] == CTX_LEN == TM and t_lat % TM == 0
    seq = CTX_LEN + t_lat
    n_t = seq // TM

    stream = (x.reshape(n_batch * t_lat, D_MODEL), ctx.reshape(n_batch * CTX_LEN, D_MODEL))

    n_rows = n_batch + 1
    pad = -n_rows % 8
    cvec = jnp.concatenate([c, c_ctx[None, :], jnp.zeros((pad, D_MODEL), F32)], axis=0)
    mod = _ada(cvec, ada_w, ada_b)[:, :n_rows]
    mod = mod.reshape(DEPTH, n_rows, 6, 1, D_MODEL).transpose(0, 2, 1, 3, 4)

    cos, sin = _rope_tables(t_lat)
    final_g2 = final_g.reshape(1, D_MODEL)

    for i in range(DEPTH):
        last = i == DEPTH - 1
        off = 1 if last else 0
        kind, j = i % 3, i // 3
        sh1, sc1, g1, sh2, sc2, g2 = (mod[i, p] for p in range(6))
        gain1 = norm1_g[i].reshape(1, D_MODEL)
        gain2 = norm2_g[i].reshape(1, D_MODEL)
        if kind == 0:
            q, kt, v, g = _ret_proj(stream, n_batch, n_t, gain1, sh1, sc1, cos, sin, *_ret_weights(ret_w_in[j]))
            decay = jnp.stack([ret_log_decay_f[j], ret_log_decay_b[j]])
            y = _ret_scan(decay, q, kt, v, n_batch, seq)
            mix = functools.partial(_gated_mix, heads=RET_HEADS, dv=RET_DV)
            mix_rows, mix_consts, w_out = [y, g], [ret_norm_g[j]], ret_w_out[j]
        elif kind == 1:
            kw = GLA_HEADS * GLA_DK
            wa1 = jnp.zeros((D_MODEL, LANES), F32)
            wa1 = wa1.at[:, :GLA_RANK].set(gla_wa1_f[j]).at[:, GLA_RANK:2 * GLA_RANK].set(gla_wa1_b[j])
            wa2 = jnp.zeros((LANES, 2 * kw), F32)
            wa2 = wa2.at[:GLA_RANK, :kw].set(gla_wa2_f[j]).at[GLA_RANK:2 * GLA_RANK, kw:].set(gla_wa2_b[j])
            ba = jnp.concatenate([gla_ba_f[j], gla_ba_b[j]]).reshape(1, 2 * kw)
            q, k, v, r, ga = _gla_proj(h, n_batch, n_t, gain1, sh1, sc1, gla_w_in[j].astype(BF16),
                                       wa1.astype(BF16), wa2.astype(BF16), ba)
            y = _gla_scan(q, k, v, ga, n_batch, seq)
            mix = functools.partial(_gated_mix, heads=GLA_HEADS, dv=GLA_DV)
            mix_rows, mix_consts, w_out = [y, r], [gla_norm_g[j]], gla_w_out[j]
        else:
            a, o_pre = _ml_proj(h, n_batch, n_t, gain1, sh1, sc1, ml_w_in[j].astype(BF16))
            n_g = 4 * MLSTM_HEADS
            wif = jnp.concatenate([ml_w_if_f[j], ml_w_if_b[j]], axis=-1)
            wif = jnp.pad(wif, ((0, 0), (0, 0), (0, 0), (0, LANES - n_g))).astype(BF16)
            bif = jnp.pad(jnp.concatenate([ml_b_if_f[j], ml_b_if_b[j]]), (0, LANES - n_g)).reshape(1, LANES)
            ac, q, kt, v, gc, gr = _ml_qkv(a, n_batch, n_t, ml_conv_w[j], ml_conv_b[j].reshape(1, MLSTM_INNER),
                                           ml_w_q[j].astype(BF16), ml_w_k[j].transpose(0, 2, 1).astype(BF16),
                                           ml_w_v[j].astype(BF16), wif[jnp.array([0, 2])],
                                           wif[1].transpose(0, 2, 1), bif)
            gc = gc.reshape(n_batch, seq, 4, MLSTM_HEADS).transpose(0, 3, 1, 2)
            gr = gr.reshape(4, MLSTM_HEADS, n_batch, seq).transpose(2, 1, 0, 3)
            hs = _ml_scan(q, kt, v, gc, gr, n_batch, seq)
            mix = _ml_mix
            mix_rows, mix_consts, w_out = [hs, o_pre, ac], [ml_skip[j].reshape(1, MLSTM_INNER)], ml_w_out[j]
        h = _mix_ffn(mix, mix_rows, mix_consts, w_out.astype(BF16), stream, n_batch, n_t, off, g1, gain2, sh2, sc2,
                     g2, final_g2, ffn_w1[i].astype(BF16), ffn_w2[i].astype(BF16), final_norm=last)
        stream = (h,)
    return h.reshape(n_batch, t_lat, D_MODEL)
```

```python
import functools
import itertools

import jax
import jax.numpy as jnp
from jax import lax
from jax.experimental import pallas as pl
from jax.experimental.pallas import tpu as pltpu

F32 = jnp.float32
BF16 = jnp.bfloat16

D_MODEL = 1024
DEPTH = 4
GRID_W = 64
CTX_LEN = 256
NORM_EPS = 1e-6
D_FF = 4 * D_MODEL
RET_HEADS = 4
RET_DK = D_MODEL // RET_HEADS
RET_DV = 2 * D_MODEL // RET_HEADS
ROPE_BASE = 10000.0
GLA_HEADS = 4
GLA_DK = D_MODEL // 2 // GLA_HEADS
GLA_DV = D_MODEL // GLA_HEADS
GLA_RANK = 16
GLA_TAU = 16.0
MLSTM_INNER = 2 * D_MODEL
MLSTM_HEADS = 4
MLSTM_DH = MLSTM_INNER // MLSTM_HEADS

TM = CTX_LEN
LANES = 128
BF16_ROWS = 16
RET_CHUNK = 256
GLA_CHUNK = 64
MLSTM_CHUNK = 256
VMEM_LIMIT = 56 * 1024 * 1024

_NT = (((1,), (1,)), ((), ()))
_TN = (((0,), (0,)), ((), ()))


def _dot(a, b):
    return jnp.dot(a, b, preferred_element_type=F32)


def _silu(x):
    return x * jax.nn.sigmoid(x)


def _log_sigmoid(z):
    return jnp.minimum(z, 0.0) - jnp.log1p(jnp.exp(-jnp.abs(z)))


def _modnorm(x, g, shift, scale):
    y = x * lax.rsqrt(jnp.mean(x * x, axis=-1, keepdims=True) + NORM_EPS) * g
    return y * (1.0 + scale) + shift


def _params(n_axes):
    return pltpu.CompilerParams(dimension_semantics=("arbitrary",) * n_axes,
                                vmem_limit_bytes=VMEM_LIMIT)


def _const_spec(shape):
    nd = len(shape)
    return pl.BlockSpec(shape, lambda *_: (0,) * nd, pipeline_mode=pl.Buffered(1))


def _row_spec(width, n_tiles, off):
    return pl.BlockSpec((TM, width), lambda b, t: (b * n_tiles + t + off, 0))


def _stream_specs(stream, n_t, off):
    if len(stream) == 1:
        return [_row_spec(D_MODEL, n_t, off)]
    assert off == 0
    return [pl.BlockSpec((TM, D_MODEL), lambda b, t: (b * (n_t - 1) + jnp.maximum(t - 1, 0), 0)),
            pl.BlockSpec((TM, D_MODEL), lambda b, t: (b, 0))]


def _stream_tile(refs):
    if len(refs) == 1:
        return refs[0][...]
    lat_ref, ctx_ref = refs
    return jnp.where(pl.program_id(1) == 0, ctx_ref[...], lat_ref[...])


def _mod_spec(n_batch, off, has_ctx):
    if has_ctx:
        return pl.BlockSpec((1, 1, D_MODEL), lambda b, t: (jnp.where(t + off == 0, n_batch, b), 0, 0))
    return pl.BlockSpec((1, 1, D_MODEL), lambda b, t: (b, 0, 0))


def _ada_kernel(c_ref, w_ref, b_ref, o_ref):
    s = _silu(c_ref[...]).astype(BF16)
    o_ref[0] = _dot(s, w_ref[0].astype(BF16)) + b_ref[0]


def _ada(cvec, ada_w, ada_b):
    rows = cvec.shape[0]
    tn = 512
    return pl.pallas_call(
        _ada_kernel,
        grid=(DEPTH, 6 * D_MODEL // tn),
        in_specs=[pl.BlockSpec((rows, D_MODEL), lambda l, j: (0, 0)),
                  pl.BlockSpec((1, D_MODEL, tn), lambda l, j: (l, 0, j)),
                  pl.BlockSpec((1, 1, tn), lambda l, j: (l, 0, j))],
        out_specs=pl.BlockSpec((1, rows, tn), lambda l, j: (l, 0, j)),
        out_shape=jax.ShapeDtypeStruct((DEPTH, rows, 6 * D_MODEL), F32),
        compiler_params=_params(2),
        name="adaln",
    )(cvec, ada_w, ada_b.reshape(DEPTH, 1, 6 * D_MODEL))


def _gated_mix(y_ref, g_ref, ng_ref, w_ref, *, heads, dv):
    acc = jnp.zeros((TM, D_MODEL), F32)
    for hd in range(heads):
        lo = hd * dv
        y = y_ref[:, lo:lo + dv].astype(F32)
        yn = y * lax.rsqrt(jnp.mean(y * y, axis=-1, keepdims=True) + NORM_EPS) * ng_ref[hd:hd + 1, :]
        z = (_silu(g_ref[:, lo:lo + dv].astype(F32)) * yn).astype(BF16)
        acc = acc + _dot(z, w_ref[lo:lo + dv, :])
    return acc


def _ml_mix(hs_ref, op_ref, ac_ref, skip_ref, w_ref):
    cw = 1024
    acc = jnp.zeros((TM, D_MODEL), F32)
    for j in range(MLSTM_INNER // cw):
        sl = slice(j * cw, (j + 1) * cw)
        y = (jax.nn.sigmoid(op_ref[:, sl].astype(F32)) * hs_ref[:, sl].astype(F32)
             + skip_ref[:, sl] * ac_ref[:, sl].astype(F32))
        acc = acc + _dot(y.astype(BF16), w_ref[sl, :])
    return acc


def _mix_ffn_kernel(*refs, n_mix, n_stream, mix, final_norm):
    mix_refs, wo_ref, stream_refs = refs[:n_mix], refs[n_mix], refs[n_mix + 1:n_mix + 1 + n_stream]
    (gate1_ref, g_ref, sh_ref, sc_ref, gate2_ref, fg_ref, w1_ref, w2_ref, o_ref) = refs[n_mix + 1 + n_stream:]
    x = _stream_tile(stream_refs) + gate1_ref[0] * mix(*mix_refs, wo_ref)
    xn = _modnorm(x, g_ref[...], sh_ref[0], sc_ref[0]).astype(BF16)
    fc = 1024
    acc = jnp.zeros((TM, D_MODEL), F32)
    for j in range(D_FF // fc):
        hid = _dot(xn, w1_ref[:, j * fc:(j + 1) * fc])
        hid = jnp.square(jnp.maximum(hid, 0.0)).astype(BF16)
        acc = acc + _dot(hid, w2_ref[j * fc:(j + 1) * fc, :])
    o = x + gate2_ref[0] * acc
    if final_norm:
        o = o * lax.rsqrt(jnp.mean(o * o, axis=-1, keepdims=True) + NORM_EPS) * fg_ref[...]
    o_ref[...] = o


def _mix_ffn(mix, mix_rows, mix_consts, w_out, stream, n_batch, n_t, off, gate1, gain, shift, scale, gate2,
             final_g, w1, w2, *, final_norm):
    n_out = n_t - off
    ms = _mod_spec(n_batch, off, off == 0)
    in_specs = ([_row_spec(a.shape[1], n_t, off) for a in mix_rows] + [_const_spec(a.shape) for a in mix_consts]
                + [_const_spec(w_out.shape)] + _stream_specs(stream, n_t, off)
                + [ms, _const_spec((1, D_MODEL)), ms, ms, ms,
                   _const_spec((1, D_MODEL)), _const_spec((D_MODEL, D_FF)), _const_spec((D_FF, D_MODEL))])
    return pl.pallas_call(
        functools.partial(_mix_ffn_kernel, n_mix=len(mix_rows) + len(mix_consts), n_stream=len(stream), mix=mix,
                          final_norm=final_norm),
        grid=(n_batch, n_out),
        in_specs=in_specs,
        out_specs=_row_spec(D_MODEL, n_out, 0),
        out_shape=jax.ShapeDtypeStruct((n_batch * n_out * TM, D_MODEL), F32),
        compiler_params=_params(2),
        name="mix_ffn",
    )(*mix_rows, *mix_consts, w_out, *stream, gate1, gain, shift, scale, gate2, final_g, w1, w2)


def _ret_proj_kernel(*refs, n_stream):
    (g_ref, sh_ref, sc_ref, cos_ref, sin_ref, cost_ref, sint_ref, w_ref, wkt_ref,
     q_ref, kt_ref, v_ref, gg_ref) = refs[n_stream:]
    xn = _modnorm(_stream_tile(refs[:n_stream]), g_ref[...], sh_ref[0], sc_ref[0]).astype(BF16)
    cos, sin = cos_ref[...], sin_ref[...]
    cos_t, sin_t = cost_ref[...], sint_ref[...]
    half = RET_DK // 2
    qk_w, v_w = RET_HEADS * RET_DK, RET_HEADS * RET_DV
    for hd in range(RET_HEADS):
        lo = hd * RET_DK
        u = _dot(xn, w_ref[:, lo:lo + RET_DK])
        u1, u2 = u[:, :half], u[:, half:]
        q_ref[:, lo:lo + RET_DK] = jnp.concatenate([u1 * cos - u2 * sin, u1 * sin + u2 * cos], axis=1).astype(BF16)
        ut = lax.dot_general(wkt_ref[lo:lo + RET_DK, :], xn, _NT, preferred_element_type=F32) * RET_DK ** -0.5
        u1, u2 = ut[:half], ut[half:]
        kt_ref[lo:lo + RET_DK, :] = jnp.concatenate([u1 * cos_t - u2 * sin_t, u1 * sin_t + u2 * cos_t],
                                                    axis=0).astype(BF16)
    for hd in range(RET_HEADS):
        lo = hd * RET_DV
        v_ref[:, lo:lo + RET_DV] = _dot(xn, w_ref[:, qk_w + lo:qk_w + lo + RET_DV]).astype(BF16)
        gg_ref[:, lo:lo + RET_DV] = _dot(xn, w_ref[:, qk_w + v_w + lo:qk_w + v_w + lo + RET_DV]).astype(BF16)


def _ret_proj(stream, n_batch, n_t, gain, shift, scale, cos, sin, w_qvg, wkt):
    ms = _mod_spec(n_batch, 0, True)
    rows = n_batch * n_t * TM
    qk_w, v_w = RET_HEADS * RET_DK, RET_HEADS * RET_DV
    tab = pl.BlockSpec((TM, RET_DK // 2), lambda b, t: (t, 0))
    tab_t = pl.BlockSpec((RET_DK // 2, TM), lambda b, t: (0, t))
    return pl.pallas_call(
        functools.partial(_ret_proj_kernel, n_stream=len(stream)),
        grid=(n_batch, n_t),
        in_specs=_stream_specs(stream, n_t, 0) + [_const_spec((1, D_MODEL)), ms, ms, tab, tab, tab_t, tab_t,
                                                  _const_spec(w_qvg.shape), _const_spec(wkt.shape)],
        out_specs=[_row_spec(qk_w, n_t, 0), pl.BlockSpec((qk_w, TM), lambda b, t: (0, b * n_t + t)),
                   _row_spec(v_w, n_t, 0), _row_spec(v_w, n_t, 0)],
        out_shape=[jax.ShapeDtypeStruct((rows, qk_w), BF16), jax.ShapeDtypeStruct((qk_w, rows), BF16),
                   jax.ShapeDtypeStruct((rows, v_w), BF16), jax.ShapeDtypeStruct((rows, v_w), BF16)],
        compiler_params=_params(2),
        name="ret_proj",
    )(*stream, gain, shift, scale, cos, sin, cos.T, sin.T, w_qvg, wkt)


def _block_starts(step, n_blocks):
    return step, jnp.where(step == 0, 0, n_blocks - step)


def _walk_blocks(run, n_blocks):
    n_lat = n_blocks - 1
    assert n_lat % 4 == 0
    run([0], [0], (True, False))

    def body(j, carry, first_visit):
        run([1 + 2 * j, 2 + 2 * j], [n_blocks - 1 - 2 * j, n_blocks - 2 - 2 * j], first_visit)
        return carry

    lax.fori_loop(0, n_lat // 4, functools.partial(body, first_visit=(True, True)), 0)
    lax.fori_loop(n_lat // 4, n_lat // 2, functools.partial(body, first_visit=(False, False)), 0)


def _ret_scan_kernel(ld_ref, q_ref, kt_ref, v_ref, o_ref, sf_ref, sb_ref, dmf_ref, dmb_ref, *, chunk, seq):
    L = chunk
    hd = pl.program_id(1)
    n_blocks = seq // L
    li = lax.broadcasted_iota(jnp.int32, (L, L), 0)
    si = lax.broadcasted_iota(jnp.int32, (L, L), 1)
    row = lax.broadcasted_iota(jnp.int32, (L, 1), 0).astype(F32)
    col = lax.broadcasted_iota(jnp.int32, (1, L), 1).astype(F32)
    consts = []
    for direction, dm_ref in ((0, dmf_ref), (1, dmb_ref)):
        lg = -jnp.exp(jnp.full((1, 1), ld_ref[direction, hd], F32))
        if direction == 0:
            dist = li - si
            q_dec = jnp.exp((row + 1.0) * lg)
            k_dec = jnp.exp((L - 1.0 - col) * lg)
        else:
            dist = si - li
            q_dec = jnp.exp((L - row) * lg)
            k_dec = jnp.exp(col * lg)
        dm_ref[...] = jnp.where(dist >= 0, jnp.exp(jnp.maximum(dist, 0).astype(F32) * lg), 0.0)
        consts.append((q_dec, k_dec, jnp.exp(float(L) * lg)))
    sf_ref[...] = jnp.zeros_like(sf_ref)
    sb_ref[...] = jnp.zeros_like(sb_ref)

    def chunk_step(direction, r0, key, prev_key, pub):
        s_ref, dm_ref = (sf_ref, dmf_ref) if direction == 0 else (sb_ref, dmb_ref)
        q_dec, k_dec, s_dec = consts[direction]
        qc = q_ref[pl.ds(r0, L), :]
        ktc = kt_ref[:, pl.ds(r0, L)]
        vc = v_ref[pl.ds(r0, L), :]
        att = _dot(qc, ktc) * dm_ref[...]
        yield
        update = _dot((ktc.astype(F32) * k_dec).astype(BF16), vc)
        yield
        state = s_ref[...] if prev_key is None else pub[prev_key]["state"]
        pub[key] = {"state": s_dec * state + update}
        carried = _dot(qc, state.astype(BF16))
        yield
        pub[key]["out"] = _dot(att.astype(BF16), vc) + q_dec * carried

    def run(blocks_f, blocks_b, first_visit):
        pub, gens, starts = {}, [], {}
        for n in range(len(blocks_f)):
            for direction, blocks in ((0, blocks_f), (1, blocks_b)):
                starts[direction, n] = pl.multiple_of(blocks[n] * L, L)
                gens.append(chunk_step(direction, starts[direction, n], (direction, n),
                                       (direction, n - 1) if n else None, pub))
        for _ in itertools.zip_longest(*gens):
            pass
        last = len(blocks_f) - 1
        sf_ref[...] = pub[0, last]["state"]
        sb_ref[...] = pub[1, last]["state"]
        for key in sorted(pub, key=lambda kn: (kn[1], kn[0])):
            rows, out = pl.ds(starts[key], L), pub[key]["out"]
            if not first_visit[key[0]]:
                out = o_ref[rows, :].astype(F32) + out
            o_ref[rows, :] = out.astype(o_ref.dtype)

    _walk_blocks(run, n_blocks)


def _ret_scan(log_decay, q, kt, v, n_batch, seq):
    assert RET_CHUNK == CTX_LEN
    return pl.pallas_call(
        functools.partial(_ret_scan_kernel, chunk=RET_CHUNK, seq=seq),
        grid=(n_batch, RET_HEADS),
        in_specs=[pl.BlockSpec(memory_space=pltpu.SMEM),
                  pl.BlockSpec((seq, RET_DK), lambda b, h: (b, h)),
                  pl.BlockSpec((RET_DK, seq), lambda b, h: (h, b)),
                  pl.BlockSpec((seq, RET_DV), lambda b, h: (b, h))],
        out_specs=pl.BlockSpec((seq, RET_DV), lambda b, h: (b, h)),
        out_shape=jax.ShapeDtypeStruct((n_batch * seq, RET_HEADS * RET_DV), BF16),
        scratch_shapes=[pltpu.VMEM((RET_DK, RET_DV), F32), pltpu.VMEM((RET_DK, RET_DV), F32),
                        pltpu.VMEM((RET_CHUNK, RET_CHUNK), F32), pltpu.VMEM((RET_CHUNK, RET_CHUNK), F32)],
        compiler_params=_params(2),
        name="ret_scan",
    )(log_decay, q, kt, v)


def _gla_proj_kernel(x_ref, g_ref, sh_ref, sc_ref, w_ref, wa1_ref, wa2_ref, ba_ref,
                     q_ref, k_ref, v_ref, r_ref, ga_ref):
    xn = _modnorm(x_ref[...], g_ref[...], sh_ref[0], sc_ref[0]).astype(BF16)
    kw, vw = GLA_HEADS * GLA_DK, GLA_HEADS * GLA_DV
    q_ref[...] = (_dot(xn, w_ref[:, :kw]) * GLA_DK ** -0.5).astype(BF16)
    k_ref[...] = _dot(xn, w_ref[:, kw:2 * kw]).astype(BF16)
    v_ref[...] = _dot(xn, w_ref[:, 2 * kw:2 * kw + vw]).astype(BF16)
    r_ref[...] = _dot(xn, w_ref[:, 2 * kw + vw:]).astype(BF16)
    low = _dot(xn, wa1_ref[...]).astype(BF16)
    z = _dot(low, wa2_ref[...]) + ba_ref[...]
    ga_ref[...] = _log_sigmoid(z) / GLA_TAU


def _gla_proj(h, n_batch, n_t, gain, shift, scale, w_in, wa1, wa2, ba):
    ms = _mod_spec(n_batch, 0, True)
    rows = h.shape[0]
    kw, vw = GLA_HEADS * GLA_DK, GLA_HEADS * GLA_DV
    return pl.pallas_call(
        _gla_proj_kernel,
        grid=(n_batch, n_t),
        in_specs=[_row_spec(D_MODEL, n_t, 0), _const_spec((1, D_MODEL)), ms, ms, _const_spec(w_in.shape),
                  _const_spec(wa1.shape), _const_spec(wa2.shape), _const_spec(ba.shape)],
        out_specs=[_row_spec(kw, n_t, 0), _row_spec(kw, n_t, 0), _row_spec(vw, n_t, 0), _row_spec(vw, n_t, 0),
                   _row_spec(2 * kw, n_t, 0)],
        out_shape=[jax.ShapeDtypeStruct((rows, kw), BF16), jax.ShapeDtypeStruct((rows, kw), BF16),
                   jax.ShapeDtypeStruct((rows, vw), BF16), jax.ShapeDtypeStruct((rows, vw), BF16),
                   jax.ShapeDtypeStruct((rows, 2 * kw), F32)],
        compiler_params=_params(2),
        name="gla_proj",
    )(h, gain, shift, scale, w_in, wa1, wa2, ba)


def _gla_scan_kernel(q_ref, k_ref, v_ref, gf_ref, gb_ref, o_ref, sf_ref, sb_ref, trif_ref, trib_ref,
                     *, sub, block, seq):
    C, blk = sub, block
    n_sub, n_blocks = blk // C, seq // blk
    li = lax.broadcasted_iota(jnp.int32, (blk, blk), 0)
    si = lax.broadcasted_iota(jnp.int32, (blk, blk), 1)
    trif_ref[...] = (li >= si).astype(BF16)
    trib_ref[...] = (si >= li).astype(BF16)
    dl = lax.broadcasted_iota(jnp.int32, (C, C), 0)
    ds = lax.broadcasted_iota(jnp.int32, (C, C), 1)
    diag_masks = (dl >= ds, ds >= dl)
    eye = (lax.broadcasted_iota(jnp.int32, (GLA_DK, GLA_DK), 0)
           == lax.broadcasted_iota(jnp.int32, (GLA_DK, GLA_DK), 1))
    sf_ref[...] = jnp.zeros_like(sf_ref)
    sb_ref[...] = jnp.zeros_like(sb_ref)

    def block_step(direction, r0, key, prev_key, pub):
        fwd = direction == 0
        gate_ref, tri_ref, s_ref = (gf_ref, trif_ref, sf_ref) if fwd else (gb_ref, trib_ref, sb_ref)
        g = gate_ref[pl.ds(r0, blk), :]
        g_hi = g.astype(BF16)
        rem = g - g_hi.astype(F32)
        g_mid = rem.astype(BF16)
        g_lo = (rem - g_mid.astype(F32)).astype(BF16)
        parts = _dot(tri_ref[...], jnp.concatenate([g_hi, g_mid, g_lo], axis=1))
        yield
        cum = parts[:, :GLA_DK] + parts[:, GLA_DK:2 * GLA_DK] + parts[:, 2 * GLA_DK:]
        qf = q_ref[pl.ds(r0, blk), :].astype(F32)
        kf = k_ref[pl.ds(r0, blk), :].astype(F32)
        vb = v_ref[pl.ds(r0, blk), :]
        end_row = blk - 1 if fwd else 0
        c_end = cum[end_row:end_row + 1]
        state = s_ref[...] if prev_key is None else pub[prev_key]["state"]
        q_carry = (qf * jnp.exp(cum)).astype(BF16)
        kd = (kf * jnp.exp(c_end - cum)).astype(BF16)
        e_col = jnp.sum(jnp.where(eye, jnp.exp(c_end), 0.0), axis=1, keepdims=True)
        pub[key] = {"state": e_col * state + lax.dot_general(kd, vb, _TN, preferred_element_type=F32)}
        att_rows = {}
        for i in range(n_sub):
            yield
            lo = i * C if fwd else blk - (i + 1) * C
            rows = slice(lo, lo + C)
            mid = lo + (C // 2 if fwd else C - 1 - C // 2)
            cq = cum[rows]
            c_mid = cum[mid:mid + 1]
            qd = (qf[rows] * jnp.exp(cq - c_mid)).astype(BF16)
            kg = (kf[rows] * jnp.exp(c_mid - cq)).astype(BF16)
            diag = jnp.where(diag_masks[direction], lax.dot_general(qd, kg, _NT, preferred_element_type=F32), 0.0)
            earlier = None
            if i > 0:
                prev = slice(0, lo) if fwd else slice(lo + C, blk)
                edge = lo - 1 if fwd else lo + C
                c_edge = cum[edge:edge + 1]
                ql = (qf[rows] * jnp.exp(cq - c_edge)).astype(BF16)
                kk = (kf[prev] * jnp.exp(c_edge - cum[prev])).astype(BF16)
                earlier = lax.dot_general(ql, kk, _NT, preferred_element_type=F32)
            unseen = jnp.zeros((C, blk - (i + 1) * C), F32) if i < n_sub - 1 else None
            cols = [earlier, diag, unseen] if fwd else [unseen, diag, earlier]
            att_rows[lo] = jnp.concatenate([c for c in cols if c is not None], axis=1).astype(BF16)
        yield
        att = jnp.concatenate([att_rows[lo] for lo in sorted(att_rows)], axis=0)
        out = _dot(jnp.concatenate([att, q_carry], axis=1), jnp.concatenate([vb, state.astype(BF16)], axis=0))
        pub[key]["out"] = out

    def run(blocks_f, blocks_b, first_visit):
        pub, gens, starts = {}, [], {}
        for n in range(len(blocks_f)):
            for direction, blocks in ((0, blocks_f), (1, blocks_b)):
                starts[direction, n] = pl.multiple_of(blocks[n] * blk, blk)
                gens.append(block_step(direction, starts[direction, n], (direction, n),
                                       (direction, n - 1) if n else None, pub))
        for _ in itertools.zip_longest(*gens):
            pass
        last = len(blocks_f) - 1
        sf_ref[...] = pub[0, last]["state"]
        sb_ref[...] = pub[1, last]["state"]
        for key in sorted(pub, key=lambda kn: (kn[1], kn[0])):
            rows, out = pl.ds(starts[key], blk), pub[key]["out"]
            if not first_visit[key[0]]:
                out = o_ref[rows, :].astype(F32) + out
            o_ref[rows, :] = out.astype(o_ref.dtype)

    _walk_blocks(run, n_blocks)


def _gla_scan(q, k, v, ga, n_batch, seq):
    return pl.pallas_call(
        functools.partial(_gla_scan_kernel, sub=GLA_CHUNK, block=CTX_LEN, seq=seq),
        grid=(n_batch, GLA_HEADS),
        in_specs=[pl.BlockSpec((seq, GLA_DK), lambda b, h: (b, h)),
                  pl.BlockSpec((seq, GLA_DK), lambda b, h: (b, h)),
                  pl.BlockSpec((seq, GLA_DV), lambda b, h: (b, h)),
                  pl.BlockSpec((seq, GLA_DK), lambda b, h: (b, h)),
                  pl.BlockSpec((seq, GLA_DK), lambda b, h: (b, GLA_HEADS + h))],
        out_specs=pl.BlockSpec((seq, GLA_DV), lambda b, h: (b, h)),
        out_shape=jax.ShapeDtypeStruct((n_batch * seq, GLA_HEADS * GLA_DV), BF16),
        scratch_shapes=[pltpu.VMEM((GLA_DK, GLA_DV), F32), pltpu.VMEM((GLA_DK, GLA_DV), F32),
                        pltpu.VMEM((CTX_LEN, CTX_LEN), BF16), pltpu.VMEM((CTX_LEN, CTX_LEN), BF16)],
        compiler_params=_params(2),
        name="gla_scan",
    )(q, k, v, ga, ga)


def _ml_proj_kernel(x_ref, g_ref, sh_ref, sc_ref, w_ref, a_ref, o_ref):
    xn = _modnorm(x_ref[...], g_ref[...], sh_ref[0], sc_ref[0]).astype(BF16)
    cw = 1024
    for j in range(MLSTM_INNER // cw):
        a_ref[:, j * cw:(j + 1) * cw] = _dot(xn, w_ref[:, j * cw:(j + 1) * cw]).astype(BF16)
        o_ref[:, j * cw:(j + 1) * cw] = _dot(
            xn, w_ref[:, MLSTM_INNER + j * cw:MLSTM_INNER + (j + 1) * cw]).astype(BF16)


def _ml_proj(h, n_batch, n_t, gain, shift, scale, w_in):
    ms = _mod_spec(n_batch, 0, True)
    rows = h.shape[0]
    return pl.pallas_call(
        _ml_proj_kernel,
        grid=(n_batch, n_t),
        in_specs=[_row_spec(D_MODEL, n_t, 0), _const_spec((1, D_MODEL)), ms, ms, _const_spec(w_in.shape)],
        out_specs=[_row_spec(MLSTM_INNER, n_t, 0), _row_spec(MLSTM_INNER, n_t, 0)],
        out_shape=[jax.ShapeDtypeStruct((rows, MLSTM_INNER), BF16)] * 2,
        compiler_params=_params(2),
        name="ml_proj",
    )(h, gain, shift, scale, w_in)


def _ml_qkv_kernel(a_ref, prev_ref, next_ref, cw_ref, cb_ref, wq_ref, wkt_ref, wv_ref, bif_ref,
                   ac_ref, q_ref, kt_ref, v_ref, gc_ref, gr_ref, *, n_t):
    t = pl.program_id(1)
    has_prev = jnp.where(t >= 2, 1.0, 0.0).astype(F32)
    has_next = jnp.where(jnp.logical_and(t >= 1, t <= n_t - 2), 1.0, 0.0).astype(F32)
    row = lax.broadcasted_iota(jnp.int32, (TM, 1), 0)
    dh = MLSTM_DH
    pre = jnp.zeros((TM, LANES), F32)
    for hd in range(MLSTM_HEADS):
        lo = hd * dh
        a_bf = a_ref[:, lo:lo + dh]
        a = a_bf.astype(F32)
        before = prev_ref[:, lo:lo + dh].astype(F32)[BF16_ROWS - 1:BF16_ROWS] * has_prev
        after = next_ref[:, lo:lo + dh].astype(F32)[0:1] * has_next
        up = jnp.where(row == 0, before, pltpu.roll(a, 1, axis=0))
        dn = jnp.where(row == TM - 1, after, pltpu.roll(a, TM - 1, axis=0))
        conv = (cw_ref[0:1, lo:lo + dh] * up + cw_ref[1:2, lo:lo + dh] * a + cw_ref[2:3, lo:lo + dh] * dn
                + cb_ref[:, lo:lo + dh])
        ac = _silu(conv).astype(BF16)
        ac_ref[:, lo:lo + dh] = ac
        q_and_gates = _dot(ac, wq_ref[hd])
        kt = (lax.dot_general(wkt_ref[hd], ac, _NT, preferred_element_type=F32) * dh ** -0.5).astype(BF16)
        v_and_gates = _dot(a_bf, wv_ref[hd])
        q_ref[:, lo:lo + dh] = q_and_gates[:, :dh].astype(BF16)
        kt_ref[lo:lo + dh, :] = kt
        v_ref[:, lo:lo + dh] = v_and_gates[:, :dh].astype(BF16)
        pre = pre + q_and_gates[:, dh:] + v_and_gates[:, dh:]
    pre = pre + bif_ref[...]
    lane = lax.broadcasted_iota(jnp.int32, (TM, LANES), 1)
    is_forget = (lane % (2 * MLSTM_HEADS)) >= MLSTM_HEADS
    gates = jnp.where(is_forget, _log_sigmoid(pre), pre)
    n_g = 4 * MLSTM_HEADS
    gc_ref[...] = gates[:, :n_g]
    gr_ref[...] = gates.T[:n_g, :]


def _ml_qkv(a, n_batch, n_t, conv_w, conv_b, wq, wkt, wv, bif):
    rows = a.shape[0]
    per_tile = TM // BF16_ROWS
    n_halo = rows // BF16_ROWS
    n_g = 4 * MLSTM_HEADS
    wide = _row_spec(MLSTM_INNER, n_t, 0)
    prev_spec = pl.BlockSpec((BF16_ROWS, MLSTM_INNER),
                             lambda b, t: (jnp.maximum((b * n_t + t) * per_tile - 1, 0), 0))
    next_spec = pl.BlockSpec((BF16_ROWS, MLSTM_INNER),
                             lambda b, t: (jnp.minimum((b * n_t + t + 1) * per_tile, n_halo - 1), 0))
    return pl.pallas_call(
        functools.partial(_ml_qkv_kernel, n_t=n_t),
        grid=(n_batch, n_t),
        in_specs=[wide, prev_spec, next_spec, _const_spec(conv_w.shape), _const_spec(conv_b.shape),
                  _const_spec(wq.shape), _const_spec(wkt.shape), _const_spec(wv.shape), _const_spec(bif.shape)],
        out_specs=[wide, wide, pl.BlockSpec((MLSTM_INNER, TM), lambda b, t: (0, b * n_t + t)), wide,
                   _row_spec(n_g, n_t, 0), pl.BlockSpec((n_g, TM), lambda b, t: (0, b * n_t + t))],
        out_shape=[jax.ShapeDtypeStruct((rows, MLSTM_INNER), BF16)] * 2
        + [jax.ShapeDtypeStruct((MLSTM_INNER, rows), BF16), jax.ShapeDtypeStruct((rows, MLSTM_INNER), BF16),
           jax.ShapeDtypeStruct((rows, n_g), F32), jax.ShapeDtypeStruct((n_g, rows), F32)],
        compiler_params=_params(2),
        name="ml_qkv",
    )(a, a, a, conv_w, conv_b, wq, wkt, wv, bif)


def _ml_scan_kernel(q_ref, kt_ref, v_ref, gc_ref, gr_ref, o_ref, ctf_ref, ctb_ref, *, chunk, seq):
    L = chunk
    dv = v_ref.shape[1]
    n_blocks = seq // L
    li = lax.broadcasted_iota(jnp.int32, (L, L), 0)
    si = lax.broadcasted_iota(jnp.int32, (L, L), 1)
    ones = jnp.ones((L, LANES), BF16)
    for ref in (ctf_ref, ctb_ref):
        ref[...] = jnp.zeros_like(ref)

    def chunk_step(direction, r0, m_prev, result, first_visit):
        ct_ref = ctf_ref if direction == 0 else ctb_ref
        col_i, col_f = (0, 1) if direction == 0 else (2, 3)
        seen = (li >= si) if direction == 0 else (si >= li)
        seen_t = (li <= si) if direction == 0 else (si <= li)
        end_row = L - 1 if direction == 0 else 0
        qc = q_ref[pl.ds(r0, L), :]
        ktc = kt_ref[:, pl.ds(r0, L)]
        vc = v_ref[pl.ds(r0, L), :]
        state = ct_ref[...]
        carried = _dot(qc, state.astype(BF16))
        yield
        qk = _dot(qc, ktc)
        yield
        f_col = gc_ref[pl.ds(r0, L), col_f:col_f + 1]
        f_row = gr_ref[col_f:col_f + 1, pl.ds(r0, L)]
        i_row = gr_ref[col_i:col_i + 1, pl.ds(r0, L)]
        b_col = jnp.sum(jnp.where(seen, f_row, 0.0), axis=1, keepdims=True)
        b_row = jnp.sum(jnp.where(seen_t, f_col, 0.0), axis=0, keepdims=True)
        b_end = b_col[end_row:end_row + 1]
        d_end = b_end - b_row + i_row
        m_new = jnp.maximum(b_end + m_prev, jnp.max(d_end, axis=1, keepdims=True))
        a_prev = jnp.exp(b_end + m_prev - m_new)
        ktw = (ktc.astype(F32) * jnp.exp(d_end - m_new)).astype(BF16)
        ct_ref[:, :dv] = a_prev * state[:, :dv] + _dot(ktw, vc)
        ct_ref[:, dv:] = a_prev * state[:, dv:] + _dot(ktw, ones)
        yield
        d_in = jnp.where(seen, b_col - b_row + i_row, -jnp.inf)
        g = b_col + m_prev
        m_t = jnp.maximum(g, jnp.max(d_in, axis=1, keepdims=True))
        w_prev = jnp.exp(g - m_t)
        s = qk * jnp.exp(d_in - m_t)
        num = _dot(s.astype(BF16), vc) + w_prev * carried[:, :dv]
        den = jnp.sum(s, axis=1, keepdims=True) + w_prev * carried[:, dv:dv + 1]
        out = num * (1.0 / jnp.maximum(jnp.abs(den), jnp.exp(-m_t)))
        if not first_visit:
            out = o_ref[pl.ds(r0, L), :].astype(F32) + out
        o_ref[pl.ds(r0, L), :] = out.astype(o_ref.dtype)
        result[direction] = m_new

    def body(step, carry, first_visit):
        blk_f, blk_b = _block_starts(step, n_blocks)
        result = {}
        for _ in itertools.zip_longest(
                chunk_step(0, pl.multiple_of(blk_f * L, L), carry[0], result, first_visit[0]),
                chunk_step(1, pl.multiple_of(blk_b * L, L), carry[1], result, first_visit[1])):
            pass
        return result[0], result[1]

    assert n_blocks % 2 == 1
    half = (n_blocks + 1) // 2
    m = body(0, (jnp.zeros((1, 1), F32), jnp.zeros((1, 1), F32)), (True, False))
    m = lax.fori_loop(1, half, functools.partial(body, first_visit=(True, True)), m)
    lax.fori_loop(half, n_blocks, functools.partial(body, first_visit=(False, False)), m)


def _ml_scan(q, kt, v, gc, gr, n_batch, seq):
    assert MLSTM_CHUNK == CTX_LEN
    dh = MLSTM_DH
    blk = pl.BlockSpec((seq, dh), lambda b, h: (b, h))
    return pl.pallas_call(
        functools.partial(_ml_scan_kernel, chunk=MLSTM_CHUNK, seq=seq),
        grid=(n_batch, MLSTM_HEADS),
        in_specs=[blk, pl.BlockSpec((dh, seq), lambda b, h: (h, b)), blk,
                  pl.BlockSpec((None, None, seq, 4), lambda b, h: (b, h, 0, 0)),
                  pl.BlockSpec((None, None, 4, seq), lambda b, h: (b, h, 0, 0))],
        out_specs=blk,
        out_shape=jax.ShapeDtypeStruct((n_batch * seq, MLSTM_INNER), BF16),
        scratch_shapes=[pltpu.VMEM((dh, dh + LANES), F32), pltpu.VMEM((dh, dh + LANES), F32)],
        compiler_params=_params(2),
        name="ml_scan",
    )(q, kt, v, gc, gr)


def _rope_tables(t_lat):
    n_f = RET_DK // 4
    pos = jnp.arange(t_lat)
    inv = ROPE_BASE ** (-jnp.arange(n_f, dtype=F32) / n_f)
    ang = jnp.concatenate([(pos // GRID_W).astype(F32)[:, None] * inv,
                           (pos % GRID_W).astype(F32)[:, None] * inv], axis=-1)
    cos = jnp.concatenate([jnp.ones((CTX_LEN, 2 * n_f), F32), jnp.cos(ang)], axis=0)
    sin = jnp.concatenate([jnp.zeros((CTX_LEN, 2 * n_f), F32), jnp.sin(ang)], axis=0)
    return cos, sin


def _ret_weights(w_in):
    qk_w = RET_HEADS * RET_DK
    perm = jnp.concatenate([jnp.arange(0, RET_DK, 2), jnp.arange(1, RET_DK, 2)])
    head_cols = (jnp.arange(RET_HEADS)[:, None] * RET_DK + perm[None, :]).reshape(-1)
    w_qvg = jnp.concatenate([w_in[:, head_cols], w_in[:, 2 * qk_w:]], axis=1).astype(BF16)
    wkt = w_in[:, qk_w + head_cols].T.astype(BF16)
    return w_qvg, wkt


def _ml_gate_weights(w_q, w_k, w_v, w_if_f, w_if_b, b_if_f, b_if_b):
    n_g = 4 * MLSTM_HEADS
    w_if = jnp.concatenate([w_if_f, w_if_b], axis=-1)
    fold = functools.partial(jnp.einsum, "hde,heg->hdg", precision=lax.Precision.HIGHEST)
    gates_u = fold(w_q, w_if[0]) + MLSTM_DH ** -0.5 * fold(w_k, w_if[1])
    gates_a = fold(w_v, w_if[2])
    pad = ((0, 0), (0, 0), (0, LANES - n_g))
    wq_aug = jnp.concatenate([w_q, jnp.pad(gates_u, pad)], axis=-1).astype(BF16)
    wv_aug = jnp.concatenate([w_v, jnp.pad(gates_a, pad)], axis=-1).astype(BF16)
    bif = jnp.pad(jnp.concatenate([b_if_f, b_if_b]), (0, LANES - n_g)).reshape(1, LANES)
    return wq_aug, wv_aug, bif


def kernel(x, c, ctx, c_ctx, ada_w, ada_b, norm1_g, norm2_g, ffn_w1, ffn_w2, final_g, ret_w_in, ret_log_decay_f, ret_log_decay_b, ret_norm_g, ret_w_out, gla_w_in, gla_wa1_f, gla_wa2_f, gla_ba_f, gla_wa1_b, gla_wa2_b, gla_ba_b, gla_norm_g, gla_w_out, ml_w_in, ml_conv_w, ml_conv_b, ml_w_q, ml_w_k, ml_w_v, ml_w_if_f, ml_b_if_f, ml_w_if_b, ml_b_if_b, ml_skip, ml_w_out):
    n_batch, t_lat, _ = x.shape
    assert ctx.shape[1] == CTX_LEN == TM and t_lat % TM == 0
    seq = CTX_LEN + t_lat
    n_t = seq // TM

    stream = (x.reshape(n_batch * t_lat, D_MODEL), ctx.reshape(n_batch * CTX_LEN, D_MODEL))

    n_rows = n_batch + 1
    pad = -n_rows % 8
    cvec = jnp.concatenate([c, c_ctx[None, :], jnp.zeros((pad, D_MODEL), F32)], axis=0)
    mod = _ada(cvec, ada_w, ada_b)[:, :n_rows]
    mod = mod.reshape(DEPTH, n_rows, 6, 1, D_MODEL).transpose(0, 2, 1, 3, 4)

    cos, sin = _rope_tables(t_lat)
    final_g2 = final_g.reshape(1, D_MODEL)

    for i in range(DEPTH):
        last = i == DEPTH - 1
        off = 1 if last else 0
        kind, j = i % 3, i // 3
        sh1, sc1, g1, sh2, sc2, g2 = (mod[i, p] for p in range(6))
        gain1 = norm1_g[i].reshape(1, D_MODEL)
        gain2 = norm2_g[i].reshape(1, D_MODEL)
        if kind == 0:
            q, kt, v, g = _ret_proj(stream, n_batch, n_t, gain1, sh1, sc1, cos, sin, *_ret_weights(ret_w_in[j]))
            decay = jnp.stack([ret_log_decay_f[j], ret_log_decay_b[j]])
            y = _ret_scan(decay, q, kt, v, n_batch, seq)
            mix = functools.partial(_gated_mix, heads=RET_HEADS, dv=RET_DV)
            mix_rows, mix_consts, w_out = [y, g], [ret_norm_g[j]], ret_w_out[j]
        elif kind == 1:
            kw = GLA_HEADS * GLA_DK
            wa1 = jnp.zeros((D_MODEL, LANES), F32)
            wa1 = wa1.at[:, :GLA_RANK].set(gla_wa1_f[j]).at[:, GLA_RANK:2 * GLA_RANK].set(gla_wa1_b[j])
            wa2 = jnp.zeros((LANES, 2 * kw), F32)
            wa2 = wa2.at[:GLA_RANK, :kw].set(gla_wa2_f[j]).at[GLA_RANK:2 * GLA_RANK, kw:].set(gla_wa2_b[j])
            ba = jnp.concatenate([gla_ba_f[j], gla_ba_b[j]]).reshape(1, 2 * kw)
            q, k, v, r, ga = _gla_proj(h, n_batch, n_t, gain1, sh1, sc1, gla_w_in[j].astype(BF16),
                                       wa1.astype(BF16), wa2.astype(BF16), ba)
            y = _gla_scan(q, k, v, ga, n_batch, seq)
            mix = functools.partial(_gated_mix, heads=GLA_HEADS, dv=GLA_DV)
            mix_rows, mix_consts, w_out = [y, r], [gla_norm_g[j]], gla_w_out[j]
        else:
            a, o_pre = _ml_proj(h, n_batch, n_t, gain1, sh1, sc1, ml_w_in[j].astype(BF16))
            wq_aug, wv_aug, bif = _ml_gate_weights(ml_w_q[j], ml_w_k[j], ml_w_v[j], ml_w_if_f[j], ml_w_if_b[j],
                                                   ml_b_if_f[j], ml_b_if_b[j])
            ac, q, kt, v, gc, gr = _ml_qkv(a, n_batch, n_t, ml_conv_w[j], ml_conv_b[j].reshape(1, MLSTM_INNER),
                                           wq_aug, ml_w_k[j].transpose(0, 2, 1).astype(BF16), wv_aug, bif)
            gc = gc.reshape(n_batch, seq, 4, MLSTM_HEADS).transpose(0, 3, 1, 2)
            gr = gr.reshape(4, MLSTM_HEADS, n_batch, seq).transpose(2, 1, 0, 3)
            hs = _ml_scan(q, kt, v, gc, gr, n_batch, seq)
            mix = _ml_mix
            mix_rows, mix_consts, w_out = [hs, o_pre, ac], [ml_skip[j].reshape(1, MLSTM_INNER)], ml_w_out[j]
        h = _mix_ffn(mix, mix_rows, mix_consts, w_out.astype(BF16), stream, n_batch, n_t, off, g1, gain2, sh2, sc2,
                     g2, final_g2, ffn_w1[i].astype(BF16), ffn_w2[i].astype(BF16), final_norm=last)
        stream = (h,)
    return h.reshape(n_batch, t_lat, D_MODEL)
```

```python
import functools
import itertools

import jax
import jax.numpy as jnp
from jax import lax
from jax.experimental import pallas as pl
from jax.experimental.pallas import tpu as pltpu

F32 = jnp.float32
BF16 = jnp.bfloat16

D_MODEL = 1024
DEPTH = 4
GRID_W = 64
CTX_LEN = 256
NORM_EPS = 1e-6
D_FF = 4 * D_MODEL
RET_HEADS = 4
RET_DK = D_MODEL // RET_HEADS
RET_DV = 2 * D_MODEL // RET_HEADS
ROPE_BASE = 10000.0
GLA_HEADS = 4
GLA_DK = D_MODEL // 2 // GLA_HEADS
GLA_DV = D_MODEL // GLA_HEADS
GLA_RANK = 16
GLA_TAU = 16.0
MLSTM_INNER = 2 * D_MODEL
MLSTM_HEADS = 4
MLSTM_DH = MLSTM_INNER // MLSTM_HEADS

TM = CTX_LEN
LANES = 128
BF16_ROWS = 16
RET_CHUNK = 256
GLA_CHUNK = 64
MLSTM_CHUNK = 256
VMEM_LIMIT = 56 * 1024 * 1024

_NT = (((1,), (1,)), ((), ()))
_TN = (((0,), (0,)), ((), ()))


def _dot(a, b):
    return jnp.dot(a, b, preferred_element_type=F32)


def _silu(x):
    return x * jax.nn.sigmoid(x)


def _log_sigmoid(z):
    return jnp.minimum(z, 0.0) - jnp.log1p(jnp.exp(-jnp.abs(z)))


def _modnorm(x, g, shift, scale):
    y = x * lax.rsqrt(jnp.mean(x * x, axis=-1, keepdims=True) + NORM_EPS) * g
    return y * (1.0 + scale) + shift


def _params(n_axes):
    return pltpu.CompilerParams(dimension_semantics=("arbitrary",) * n_axes,
                                vmem_limit_bytes=VMEM_LIMIT)


def _const_spec(shape):
    nd = len(shape)
    return pl.BlockSpec(shape, lambda *_: (0,) * nd, pipeline_mode=pl.Buffered(1))


def _row_spec(width, n_tiles, off):
    return pl.BlockSpec((TM, width), lambda b, t: (b * n_tiles + t + off, 0))


def _stream_specs(stream, n_t, off):
    if len(stream) == 1:
        return [_row_spec(D_MODEL, n_t, off)]
    assert off == 0
    return [pl.BlockSpec((TM, D_MODEL), lambda b, t: (b * (n_t - 1) + jnp.maximum(t - 1, 0), 0)),
            pl.BlockSpec((TM, D_MODEL), lambda b, t: (b, 0))]


def _stream_tile(refs):
    if len(refs) == 1:
        return refs[0][...]
    lat_ref, ctx_ref = refs
    return jnp.where(pl.program_id(1) == 0, ctx_ref[...], lat_ref[...])


def _mod_spec(n_batch, off, has_ctx):
    if has_ctx:
        return pl.BlockSpec((1, 1, D_MODEL), lambda b, t: (jnp.where(t + off == 0, n_batch, b), 0, 0))
    return pl.BlockSpec((1, 1, D_MODEL), lambda b, t: (b, 0, 0))


def _ada_kernel(c_ref, w_ref, b_ref, o_ref):
    s = _silu(c_ref[...]).astype(BF16)
    o_ref[0] = _dot(s, w_ref[0].astype(BF16)) + b_ref[0]


def _ada(cvec, ada_w, ada_b):
    rows = cvec.shape[0]
    tn = 512
    return pl.pallas_call(
        _ada_kernel,
        grid=(DEPTH, 6 * D_MODEL // tn),
        in_specs=[pl.BlockSpec((rows, D_MODEL), lambda l, j: (0, 0)),
                  pl.BlockSpec((1, D_MODEL, tn), lambda l, j: (l, 0, j)),
                  pl.BlockSpec((1, 1, tn), lambda l, j: (l, 0, j))],
        out_specs=pl.BlockSpec((1, rows, tn), lambda l, j: (l, 0, j)),
        out_shape=jax.ShapeDtypeStruct((DEPTH, rows, 6 * D_MODEL), F32),
        compiler_params=_params(2),
        name="adaln",
    )(cvec, ada_w, ada_b.reshape(DEPTH, 1, 6 * D_MODEL))


def _gated_mix(y_ref, g_ref, ng_ref, w_ref, *, heads, dv):
    acc = jnp.zeros((TM, D_MODEL), F32)
    for hd in range(heads):
        lo = hd * dv
        y = y_ref[:, lo:lo + dv]
        yn = y * lax.rsqrt(jnp.mean(y * y, axis=-1, keepdims=True) + NORM_EPS) * ng_ref[hd:hd + 1, :]
        z = (_silu(g_ref[:, lo:lo + dv].astype(F32)) * yn).astype(BF16)
        acc = acc + _dot(z, w_ref[lo:lo + dv, :])
    return acc


def _ml_mix(hs_ref, op_ref, ac_ref, skip_ref, w_ref):
    cw = 1024
    acc = jnp.zeros((TM, D_MODEL), F32)
    for j in range(MLSTM_INNER // cw):
        sl = slice(j * cw, (j + 1) * cw)
        y = (jax.nn.sigmoid(op_ref[:, sl].astype(F32)) * hs_ref[:, sl].astype(F32)
             + skip_ref[:, sl] * ac_ref[:, sl].astype(F32))
        acc = acc + _dot(y.astype(BF16), w_ref[sl, :])
    return acc


def _mix_ffn_kernel(*refs, n_mix, n_stream, mix, final_norm):
    mix_refs, wo_ref, stream_refs = refs[:n_mix], refs[n_mix], refs[n_mix + 1:n_mix + 1 + n_stream]
    (gate1_ref, g_ref, sh_ref, sc_ref, gate2_ref, fg_ref, w1_ref, w2_ref, o_ref) = refs[n_mix + 1 + n_stream:]
    x = _stream_tile(stream_refs) + gate1_ref[0] * mix(*mix_refs, wo_ref)
    xn = _modnorm(x, g_ref[...], sh_ref[0], sc_ref[0]).astype(BF16)
    fc = 1024
    acc = jnp.zeros((TM, D_MODEL), F32)
    for j in range(D_FF // fc):
        hid = _dot(xn, w1_ref[:, j * fc:(j + 1) * fc])
        hid = jnp.square(jnp.maximum(hid, 0.0)).astype(BF16)
        acc = acc + _dot(hid, w2_ref[j * fc:(j + 1) * fc, :])
    o = x + gate2_ref[0] * acc
    if final_norm:
        o = o * lax.rsqrt(jnp.mean(o * o, axis=-1, keepdims=True) + NORM_EPS) * fg_ref[...]
    o_ref[...] = o


def _mix_ffn(mix, mix_rows, mix_consts, w_out, stream, n_batch, n_t, off, gate1, gain, shift, scale, gate2,
             final_g, w1, w2, *, final_norm):
    n_out = n_t - off
    ms = _mod_spec(n_batch, off, off == 0)
    in_specs = ([_row_spec(a.shape[1], n_t, off) for a in mix_rows] + [_const_spec(a.shape) for a in mix_consts]
                + [_const_spec(w_out.shape)] + _stream_specs(stream, n_t, off)
                + [ms, _const_spec((1, D_MODEL)), ms, ms, ms,
                   _const_spec((1, D_MODEL)), _const_spec((D_MODEL, D_FF)), _const_spec((D_FF, D_MODEL))])
    return pl.pallas_call(
        functools.partial(_mix_ffn_kernel, n_mix=len(mix_rows) + len(mix_consts), n_stream=len(stream), mix=mix,
                          final_norm=final_norm),
        grid=(n_batch, n_out),
        in_specs=in_specs,
        out_specs=_row_spec(D_MODEL, n_out, 0),
        out_shape=jax.ShapeDtypeStruct((n_batch * n_out * TM, D_MODEL), F32),
        compiler_params=_params(2),
        name="mix_ffn",
    )(*mix_rows, *mix_consts, w_out, *stream, gate1, gain, shift, scale, gate2, final_g, w1, w2)


def _ret_proj_kernel(*refs, n_stream):
    (g_ref, sh_ref, sc_ref, cos_ref, sin_ref, cost_ref, sint_ref, w_ref, wkt_ref,
     q_ref, kt_ref, v_ref, gg_ref) = refs[n_stream:]
    xn = _modnorm(_stream_tile(refs[:n_stream]), g_ref[...], sh_ref[0], sc_ref[0]).astype(BF16)
    cos, sin = cos_ref[...], sin_ref[...]
    cos_t, sin_t = cost_ref[...], sint_ref[...]
    half = RET_DK // 2
    qk_w, v_w = RET_HEADS * RET_DK, RET_HEADS * RET_DV
    for hd in range(RET_HEADS):
        lo = hd * RET_DK
        u = _dot(xn, w_ref[:, lo:lo + RET_DK])
        u1, u2 = u[:, :half], u[:, half:]
        q_ref[:, lo:lo + RET_DK] = jnp.concatenate([u1 * cos - u2 * sin, u1 * sin + u2 * cos], axis=1).astype(BF16)
        ut = lax.dot_general(wkt_ref[lo:lo + RET_DK, :], xn, _NT, preferred_element_type=F32) * RET_DK ** -0.5
        u1, u2 = ut[:half], ut[half:]
        kt_ref[lo:lo + RET_DK, :] = jnp.concatenate([u1 * cos_t - u2 * sin_t, u1 * sin_t + u2 * cos_t],
                                                    axis=0).astype(BF16)
    for hd in range(RET_HEADS):
        lo = hd * RET_DV
        v_ref[:, lo:lo + RET_DV] = _dot(xn, w_ref[:, qk_w + lo:qk_w + lo + RET_DV]).astype(BF16)
        gg_ref[:, lo:lo + RET_DV] = _dot(xn, w_ref[:, qk_w + v_w + lo:qk_w + v_w + lo + RET_DV]).astype(BF16)


def _ret_proj(stream, n_batch, n_t, gain, shift, scale, cos, sin, w_qvg, wkt):
    ms = _mod_spec(n_batch, 0, True)
    rows = n_batch * n_t * TM
    qk_w, v_w = RET_HEADS * RET_DK, RET_HEADS * RET_DV
    tab = pl.BlockSpec((TM, RET_DK // 2), lambda b, t: (t, 0))
    tab_t = pl.BlockSpec((RET_DK // 2, TM), lambda b, t: (0, t))
    return pl.pallas_call(
        functools.partial(_ret_proj_kernel, n_stream=len(stream)),
        grid=(n_batch, n_t),
        in_specs=_stream_specs(stream, n_t, 0) + [_const_spec((1, D_MODEL)), ms, ms, tab, tab, tab_t, tab_t,
                                                  _const_spec(w_qvg.shape), _const_spec(wkt.shape)],
        out_specs=[_row_spec(qk_w, n_t, 0), pl.BlockSpec((qk_w, TM), lambda b, t: (0, b * n_t + t)),
                   _row_spec(v_w, n_t, 0), _row_spec(v_w, n_t, 0)],
        out_shape=[jax.ShapeDtypeStruct((rows, qk_w), BF16), jax.ShapeDtypeStruct((qk_w, rows), BF16),
                   jax.ShapeDtypeStruct((rows, v_w), BF16), jax.ShapeDtypeStruct((rows, v_w), BF16)],
        compiler_params=_params(2),
        name="ret_proj",
    )(*stream, gain, shift, scale, cos, sin, cos.T, sin.T, w_qvg, wkt)


def _block_starts(step, n_blocks):
    return step, jnp.where(step == 0, 0, n_blocks - step)


def _walk_blocks(run, n_blocks, per_dir=2):
    n_lat = n_blocks - 1
    assert n_lat % (2 * per_dir) == 0
    run([0], [0], (True, False))

    def body(j, carry, first_visit):
        run([1 + per_dir * j + i for i in range(per_dir)],
            [n_blocks - 1 - per_dir * j - i for i in range(per_dir)], first_visit)
        return carry

    n_iter = n_lat // per_dir
    lax.fori_loop(0, n_iter // 2, functools.partial(body, first_visit=(True, True)), 0)
    lax.fori_loop(n_iter // 2, n_iter, functools.partial(body, first_visit=(False, False)), 0)


def _ret_scan_kernel(ld_ref, q_ref, kt_ref, v_ref, o_ref, sf_ref, sb_ref, dmf_ref, dmb_ref, *, chunk, seq):
    L = chunk
    hd = pl.program_id(1)
    n_blocks = seq // L
    li = lax.broadcasted_iota(jnp.int32, (L, L), 0)
    si = lax.broadcasted_iota(jnp.int32, (L, L), 1)
    row = lax.broadcasted_iota(jnp.int32, (L, 1), 0).astype(F32)
    col = lax.broadcasted_iota(jnp.int32, (1, L), 1).astype(F32)
    consts = []
    for direction, dm_ref in ((0, dmf_ref), (1, dmb_ref)):
        lg = -jnp.exp(jnp.full((1, 1), ld_ref[direction, hd], F32))
        if direction == 0:
            dist = li - si
            q_dec = jnp.exp((row + 1.0) * lg)
            k_dec = jnp.exp((L - 1.0 - col) * lg)
        else:
            dist = si - li
            q_dec = jnp.exp((L - row) * lg)
            k_dec = jnp.exp(col * lg)
        dm_ref[...] = jnp.where(dist >= 0, jnp.exp(jnp.maximum(dist, 0).astype(F32) * lg), 0.0)
        consts.append((q_dec, k_dec, jnp.exp(float(L) * lg)))
    sf_ref[...] = jnp.zeros_like(sf_ref)
    sb_ref[...] = jnp.zeros_like(sb_ref)

    def chunk_step(direction, r0, key, prev_key, pub):
        s_ref, dm_ref = (sf_ref, dmf_ref) if direction == 0 else (sb_ref, dmb_ref)
        q_dec, k_dec, s_dec = consts[direction]
        qc = q_ref[pl.ds(r0, L), :]
        ktc = kt_ref[:, pl.ds(r0, L)]
        vc = v_ref[pl.ds(r0, L), :]
        att = _dot(qc, ktc) * dm_ref[...]
        yield
        update = _dot((ktc.astype(F32) * k_dec).astype(BF16), vc)
        yield
        state = s_ref[...] if prev_key is None else pub[prev_key]["state"]
        pub[key] = {"state": s_dec * state + update}
        carried = _dot(qc, state.astype(BF16))
        yield
        pub[key]["out"] = _dot(att.astype(BF16), vc) + q_dec * carried

    def run(blocks_f, blocks_b, first_visit):
        pub, gens, starts = {}, [], {}
        for n in range(len(blocks_f)):
            for direction, blocks in ((0, blocks_f), (1, blocks_b)):
                starts[direction, n] = pl.multiple_of(blocks[n] * L, L)
                gens.append(chunk_step(direction, starts[direction, n], (direction, n),
                                       (direction, n - 1) if n else None, pub))
        for _ in itertools.zip_longest(*gens):
            pass
        last = len(blocks_f) - 1
        sf_ref[...] = pub[0, last]["state"]
        sb_ref[...] = pub[1, last]["state"]
        for key in sorted(pub, key=lambda kn: (kn[1], kn[0])):
            if first_visit[key[0]]:
                o_ref[pl.ds(starts[key], L), :] = pub[key]["out"]
            else:
                o_ref[pl.ds(starts[key], L), :] += pub[key]["out"]

    _walk_blocks(run, n_blocks, per_dir=4)


def _ret_scan(log_decay, q, kt, v, n_batch, seq):
    assert RET_CHUNK == CTX_LEN
    return pl.pallas_call(
        functools.partial(_ret_scan_kernel, chunk=RET_CHUNK, seq=seq),
        grid=(n_batch, RET_HEADS),
        in_specs=[pl.BlockSpec(memory_space=pltpu.SMEM),
                  pl.BlockSpec((seq, RET_DK), lambda b, h: (b, h)),
                  pl.BlockSpec((RET_DK, seq), lambda b, h: (h, b)),
                  pl.BlockSpec((seq, RET_DV), lambda b, h: (b, h))],
        out_specs=pl.BlockSpec((seq, RET_DV), lambda b, h: (b, h)),
        out_shape=jax.ShapeDtypeStruct((n_batch * seq, RET_HEADS * RET_DV), F32),
        scratch_shapes=[pltpu.VMEM((RET_DK, RET_DV), F32), pltpu.VMEM((RET_DK, RET_DV), F32),
                        pltpu.VMEM((RET_CHUNK, RET_CHUNK), F32), pltpu.VMEM((RET_CHUNK, RET_CHUNK), F32)],
        compiler_params=_params(2),
        name="ret_scan",
    )(log_decay, q, kt, v)


def _gla_proj_kernel(x_ref, g_ref, sh_ref, sc_ref, w_ref, wa1_ref, wa2_ref, ba_ref,
                     q_ref, k_ref, v_ref, r_ref, ga_ref):
    xn = _modnorm(x_ref[...], g_ref[...], sh_ref[0], sc_ref[0]).astype(BF16)
    kw, vw = GLA_HEADS * GLA_DK, GLA_HEADS * GLA_DV
    q_ref[...] = (_dot(xn, w_ref[:, :kw]) * GLA_DK ** -0.5).astype(BF16)
    k_ref[...] = _dot(xn, w_ref[:, kw:2 * kw]).astype(BF16)
    v_ref[...] = _dot(xn, w_ref[:, 2 * kw:2 * kw + vw]).astype(BF16)
    r_ref[...] = _dot(xn, w_ref[:, 2 * kw + vw:]).astype(BF16)
    low = _dot(xn, wa1_ref[...]).astype(BF16)
    z = _dot(low, wa2_ref[...]) + ba_ref[...]
    ga_ref[...] = _log_sigmoid(z) / GLA_TAU


def _gla_proj(h, n_batch, n_t, gain, shift, scale, w_in, wa1, wa2, ba):
    ms = _mod_spec(n_batch, 0, True)
    rows = h.shape[0]
    kw, vw = GLA_HEADS * GLA_DK, GLA_HEADS * GLA_DV
    return pl.pallas_call(
        _gla_proj_kernel,
        grid=(n_batch, n_t),
        in_specs=[_row_spec(D_MODEL, n_t, 0), _const_spec((1, D_MODEL)), ms, ms, _const_spec(w_in.shape),
                  _const_spec(wa1.shape), _const_spec(wa2.shape), _const_spec(ba.shape)],
        out_specs=[_row_spec(kw, n_t, 0), _row_spec(kw, n_t, 0), _row_spec(vw, n_t, 0), _row_spec(vw, n_t, 0),
                   _row_spec(2 * kw, n_t, 0)],
        out_shape=[jax.ShapeDtypeStruct((rows, kw), BF16), jax.ShapeDtypeStruct((rows, kw), BF16),
                   jax.ShapeDtypeStruct((rows, vw), BF16), jax.ShapeDtypeStruct((rows, vw), BF16),
                   jax.ShapeDtypeStruct((rows, 2 * kw), F32)],
        compiler_params=_params(2),
        name="gla_proj",
    )(h, gain, shift, scale, w_in, wa1, wa2, ba)


def _gla_scan_kernel(q_ref, k_ref, v_ref, gf_ref, gb_ref, o_ref, sf_ref, sb_ref, trif_ref, trib_ref,
                     *, sub, block, seq):
    C, blk = sub, block
    n_sub, n_blocks = blk // C, seq // blk
    li = lax.broadcasted_iota(jnp.int32, (blk, blk), 0)
    si = lax.broadcasted_iota(jnp.int32, (blk, blk), 1)
    trif_ref[...] = (li >= si).astype(BF16)
    trib_ref[...] = (si >= li).astype(BF16)
    dl = lax.broadcasted_iota(jnp.int32, (C, C), 0)
    ds = lax.broadcasted_iota(jnp.int32, (C, C), 1)
    diag_masks = (dl >= ds, ds >= dl)
    eye = (lax.broadcasted_iota(jnp.int32, (GLA_DK, GLA_DK), 0)
           == lax.broadcasted_iota(jnp.int32, (GLA_DK, GLA_DK), 1))
    sf_ref[...] = jnp.zeros_like(sf_ref)
    sb_ref[...] = jnp.zeros_like(sb_ref)

    def block_step(direction, r0, key, prev_key, pub):
        fwd = direction == 0
        gate_ref, tri_ref, s_ref = (gf_ref, trif_ref, sf_ref) if fwd else (gb_ref, trib_ref, sb_ref)
        g = gate_ref[pl.ds(r0, blk), :]
        g_hi = g.astype(BF16)
        rem = g - g_hi.astype(F32)
        g_mid = rem.astype(BF16)
        g_lo = (rem - g_mid.astype(F32)).astype(BF16)
        parts = _dot(tri_ref[...], jnp.concatenate([g_hi, g_mid, g_lo], axis=1))
        yield
        cum = parts[:, :GLA_DK] + parts[:, GLA_DK:2 * GLA_DK] + parts[:, 2 * GLA_DK:]
        qf = q_ref[pl.ds(r0, blk), :].astype(F32)
        kf = k_ref[pl.ds(r0, blk), :].astype(F32)
        vb = v_ref[pl.ds(r0, blk), :]
        end_row = blk - 1 if fwd else 0
        c_end = cum[end_row:end_row + 1]
        state = s_ref[...] if prev_key is None else pub[prev_key]["state"]
        q_carry = (qf * jnp.exp(cum)).astype(BF16)
        kd = (kf * jnp.exp(c_end - cum)).astype(BF16)
        e_col = jnp.sum(jnp.where(eye, jnp.exp(c_end), 0.0), axis=1, keepdims=True)
        pub[key] = {"state": e_col * state + lax.dot_general(kd, vb, _TN, preferred_element_type=F32)}
        att_rows = {}
        for i in range(n_sub):
            yield
            lo = i * C if fwd else blk - (i + 1) * C
            rows = slice(lo, lo + C)
            mid = lo + (C // 2 if fwd else C - 1 - C // 2)
            cq = cum[rows]
            c_mid = cum[mid:mid + 1]
            qd = (qf[rows] * jnp.exp(cq - c_mid)).astype(BF16)
            kg = (kf[rows] * jnp.exp(c_mid - cq)).astype(BF16)
            diag = jnp.where(diag_masks[direction], lax.dot_general(qd, kg, _NT, preferred_element_type=F32), 0.0)
            earlier = None
            if i > 0:
                prev = slice(0, lo) if fwd else slice(lo + C, blk)
                edge = lo - 1 if fwd else lo + C
                c_edge = cum[edge:edge + 1]
                ql = (qf[rows] * jnp.exp(cq - c_edge)).astype(BF16)
                kk = (kf[prev] * jnp.exp(c_edge - cum[prev])).astype(BF16)
                earlier = lax.dot_general(ql, kk, _NT, preferred_element_type=F32)
            unseen = jnp.zeros((C, blk - (i + 1) * C), F32) if i < n_sub - 1 else None
            cols = [earlier, diag, unseen] if fwd else [unseen, diag, earlier]
            att_rows[lo] = jnp.concatenate([c for c in cols if c is not None], axis=1).astype(BF16)
        yield
        att = jnp.concatenate([att_rows[lo] for lo in sorted(att_rows)], axis=0)
        out = _dot(jnp.concatenate([att, q_carry], axis=1), jnp.concatenate([vb, state.astype(BF16)], axis=0))
        pub[key]["out"] = out

    def run(blocks_f, blocks_b, first_visit):
        pub, gens, starts = {}, [], {}
        for n in range(len(blocks_f)):
            for direction, blocks in ((0, blocks_f), (1, blocks_b)):
                starts[direction, n] = pl.multiple_of(blocks[n] * blk, blk)
                gens.append(block_step(direction, starts[direction, n], (direction, n),
                                       (direction, n - 1) if n else None, pub))
        for _ in itertools.zip_longest(*gens):
            pass
        last = len(blocks_f) - 1
        sf_ref[...] = pub[0, last]["state"]
        sb_ref[...] = pub[1, last]["state"]
        for key in sorted(pub, key=lambda kn: (kn[1], kn[0])):
            if first_visit[key[0]]:
                o_ref[pl.ds(starts[key], blk), :] = pub[key]["out"]
            else:
                o_ref[pl.ds(starts[key], blk), :] += pub[key]["out"]

    _walk_blocks(run, n_blocks, per_dir=4)


def _gla_scan(q, k, v, ga, n_batch, seq):
    return pl.pallas_call(
        functools.partial(_gla_scan_kernel, sub=GLA_CHUNK, block=CTX_LEN, seq=seq),
        grid=(n_batch, GLA_HEADS),
        in_specs=[pl.BlockSpec((seq, GLA_DK), lambda b, h: (b, h)),
                  pl.BlockSpec((seq, GLA_DK), lambda b, h: (b, h)),
                  pl.BlockSpec((seq, GLA_DV), lambda b, h: (b, h)),
                  pl.BlockSpec((seq, GLA_DK), lambda b, h: (b, h)),
                  pl.BlockSpec((seq, GLA_DK), lambda b, h: (b, GLA_HEADS + h))],
        out_specs=pl.BlockSpec((seq, GLA_DV), lambda b, h: (b, h)),
        out_shape=jax.ShapeDtypeStruct((n_batch * seq, GLA_HEADS * GLA_DV), F32),
        scratch_shapes=[pltpu.VMEM((GLA_DK, GLA_DV), F32), pltpu.VMEM((GLA_DK, GLA_DV), F32),
                        pltpu.VMEM((CTX_LEN, CTX_LEN), BF16), pltpu.VMEM((CTX_LEN, CTX_LEN), BF16)],
        compiler_params=_params(2),
        name="gla_scan",
    )(q, k, v, ga, ga)


def _ml_proj_kernel(x_ref, g_ref, sh_ref, sc_ref, w_ref, a_ref, o_ref):
    xn = _modnorm(x_ref[...], g_ref[...], sh_ref[0], sc_ref[0]).astype(BF16)
    cw = 1024
    for j in range(MLSTM_INNER // cw):
        a_ref[:, j * cw:(j + 1) * cw] = _dot(xn, w_ref[:, j * cw:(j + 1) * cw]).astype(BF16)
        o_ref[:, j * cw:(j + 1) * cw] = _dot(
            xn, w_ref[:, MLSTM_INNER + j * cw:MLSTM_INNER + (j + 1) * cw]).astype(BF16)


def _ml_proj(h, n_batch, n_t, gain, shift, scale, w_in):
    ms = _mod_spec(n_batch, 0, True)
    rows = h.shape[0]
    return pl.pallas_call(
        _ml_proj_kernel,
        grid=(n_batch, n_t),
        in_specs=[_row_spec(D_MODEL, n_t, 0), _const_spec((1, D_MODEL)), ms, ms, _const_spec(w_in.shape)],
        out_specs=[_row_spec(MLSTM_INNER, n_t, 0), _row_spec(MLSTM_INNER, n_t, 0)],
        out_shape=[jax.ShapeDtypeStruct((rows, MLSTM_INNER), BF16)] * 2,
        compiler_params=_params(2),
        name="ml_proj",
    )(h, gain, shift, scale, w_in)


def _ml_qkv_kernel(a_ref, prev_ref, next_ref, cw_ref, cb_ref, wq_ref, wkt_ref, wv_ref, bif_ref,
                   ac_ref, q_ref, kt_ref, v_ref, gc_ref, gr_ref, *, n_t):
    t = pl.program_id(1)
    has_prev = jnp.where(t >= 2, 1.0, 0.0).astype(F32)
    has_next = jnp.where(jnp.logical_and(t >= 1, t <= n_t - 2), 1.0, 0.0).astype(F32)
    row = lax.broadcasted_iota(jnp.int32, (TM, 1), 0)
    dh = MLSTM_DH
    pre = jnp.zeros((TM, LANES), F32)
    for hd in range(MLSTM_HEADS):
        lo = hd * dh
        a_bf = a_ref[:, lo:lo + dh]
        a = a_bf.astype(F32)
        before = prev_ref[:, lo:lo + dh].astype(F32)[BF16_ROWS - 1:BF16_ROWS] * has_prev
        after = next_ref[:, lo:lo + dh].astype(F32)[0:1] * has_next
        up = jnp.where(row == 0, before, pltpu.roll(a, 1, axis=0))
        dn = jnp.where(row == TM - 1, after, pltpu.roll(a, TM - 1, axis=0))
        conv = (cw_ref[0:1, lo:lo + dh] * up + cw_ref[1:2, lo:lo + dh] * a + cw_ref[2:3, lo:lo + dh] * dn
                + cb_ref[:, lo:lo + dh])
        ac = _silu(conv).astype(BF16)
        ac_ref[:, lo:lo + dh] = ac
        q_and_gates = _dot(ac, wq_ref[hd])
        kt = (lax.dot_general(wkt_ref[hd], ac, _NT, preferred_element_type=F32) * dh ** -0.5).astype(BF16)
        v_and_gates = _dot(a_bf, wv_ref[hd])
        q_ref[:, lo:lo + dh] = q_and_gates[:, :dh].astype(BF16)
        kt_ref[lo:lo + dh, :] = kt
        v_ref[:, lo:lo + dh] = v_and_gates[:, :dh].astype(BF16)
        pre = pre + q_and_gates[:, dh:] + v_and_gates[:, dh:]
    pre = pre + bif_ref[...]
    lane = lax.broadcasted_iota(jnp.int32, (TM, LANES), 1)
    is_forget = (lane % (2 * MLSTM_HEADS)) >= MLSTM_HEADS
    gates = jnp.where(is_forget, _log_sigmoid(pre), pre)
    n_g = 4 * MLSTM_HEADS
    gc_ref[...] = gates[:, :n_g]
    gr_ref[...] = gates.T[:n_g, :]


def _ml_qkv(a, n_batch, n_t, conv_w, conv_b, wq, wkt, wv, bif):
    rows = a.shape[0]
    per_tile = TM // BF16_ROWS
    n_halo = rows // BF16_ROWS
    n_g = 4 * MLSTM_HEADS
    wide = _row_spec(MLSTM_INNER, n_t, 0)
    prev_spec = pl.BlockSpec((BF16_ROWS, MLSTM_INNER),
                             lambda b, t: (jnp.maximum((b * n_t + t) * per_tile - 1, 0), 0))
    next_spec = pl.BlockSpec((BF16_ROWS, MLSTM_INNER),
                             lambda b, t: (jnp.minimum((b * n_t + t + 1) * per_tile, n_halo - 1), 0))
    return pl.pallas_call(
        functools.partial(_ml_qkv_kernel, n_t=n_t),
        grid=(n_batch, n_t),
        in_specs=[wide, prev_spec, next_spec, _const_spec(conv_w.shape), _const_spec(conv_b.shape),
                  _const_spec(wq.shape), _const_spec(wkt.shape), _const_spec(wv.shape), _const_spec(bif.shape)],
        out_specs=[wide, wide, pl.BlockSpec((MLSTM_INNER, TM), lambda b, t: (0, b * n_t + t)), wide,
                   _row_spec(n_g, n_t, 0), pl.BlockSpec((n_g, TM), lambda b, t: (0, b * n_t + t))],
        out_shape=[jax.ShapeDtypeStruct((rows, MLSTM_INNER), BF16)] * 2
        + [jax.ShapeDtypeStruct((MLSTM_INNER, rows), BF16), jax.ShapeDtypeStruct((rows, MLSTM_INNER), BF16),
           jax.ShapeDtypeStruct((rows, n_g), F32), jax.ShapeDtypeStruct((n_g, rows), F32)],
        compiler_params=_params(2),
        name="ml_qkv",
    )(a, a, a, conv_w, conv_b, wq, wkt, wv, bif)


def _ml_scan_kernel(q_ref, kt_ref, v_ref, gc_ref, gr_ref, o_ref, ctf_ref, ctb_ref, *, chunk, seq):
    L = chunk
    dv = v_ref.shape[1]
    n_blocks = seq // L
    li = lax.broadcasted_iota(jnp.int32, (L, L), 0)
    si = lax.broadcasted_iota(jnp.int32, (L, L), 1)
    ones = jnp.ones((L, LANES), BF16)
    for ref in (ctf_ref, ctb_ref):
        ref[...] = jnp.zeros_like(ref)

    def chunk_step(direction, r0, m_prev, result, first_visit):
        ct_ref = ctf_ref if direction == 0 else ctb_ref
        col_i, col_f = (0, 1) if direction == 0 else (2, 3)
        seen = (li >= si) if direction == 0 else (si >= li)
        seen_t = (li <= si) if direction == 0 else (si <= li)
        end_row = L - 1 if direction == 0 else 0
        qc = q_ref[pl.ds(r0, L), :]
        ktc = kt_ref[:, pl.ds(r0, L)]
        vc = v_ref[pl.ds(r0, L), :]
        state = ct_ref[...]
        carried = _dot(qc, state.astype(BF16))
        yield
        qk = _dot(qc, ktc)
        yield
        f_col = gc_ref[pl.ds(r0, L), col_f:col_f + 1]
        f_row = gr_ref[col_f:col_f + 1, pl.ds(r0, L)]
        i_row = gr_ref[col_i:col_i + 1, pl.ds(r0, L)]
        b_col = jnp.sum(jnp.where(seen, f_row, 0.0), axis=1, keepdims=True)
        b_row = jnp.sum(jnp.where(seen_t, f_col, 0.0), axis=0, keepdims=True)
        b_end = b_col[end_row:end_row + 1]
        d_end = b_end - b_row + i_row
        m_new = jnp.maximum(b_end + m_prev, jnp.max(d_end, axis=1, keepdims=True))
        a_prev = jnp.exp(b_end + m_prev - m_new)
        ktw = (ktc.astype(F32) * jnp.exp(d_end - m_new)).astype(BF16)
        ct_ref[:, :dv] = a_prev * state[:, :dv] + _dot(ktw, vc)
        ct_ref[:, dv:] = a_prev * state[:, dv:] + _dot(ktw, ones)
        yield
        d_in = jnp.where(seen, b_col - b_row + i_row, -jnp.inf)
        g = b_col + m_prev
        m_t = jnp.maximum(g, jnp.max(d_in, axis=1, keepdims=True))
        w_prev = jnp.exp(g - m_t)
        s = qk * jnp.exp(d_in - m_t)
        num = _dot(s.astype(BF16), vc) + w_prev * carried[:, :dv]
        den = jnp.sum(s, axis=1, keepdims=True) + w_prev * carried[:, dv:dv + 1]
        out = num * (1.0 / jnp.maximum(jnp.abs(den), jnp.exp(-m_t)))
        if not first_visit:
            out = o_ref[pl.ds(r0, L), :].astype(F32) + out
        o_ref[pl.ds(r0, L), :] = out.astype(o_ref.dtype)
        result[direction] = m_new

    def body(step, carry, first_visit):
        blk_f, blk_b = _block_starts(step, n_blocks)
        result = {}
        for _ in itertools.zip_longest(
                chunk_step(0, pl.multiple_of(blk_f * L, L), carry[0], result, first_visit[0]),
                chunk_step(1, pl.multiple_of(blk_b * L, L), carry[1], result, first_visit[1])):
            pass
        return result[0], result[1]

    assert n_blocks % 2 == 1
    half = (n_blocks + 1) // 2
    m = body(0, (jnp.zeros((1, 1), F32), jnp.zeros((1, 1), F32)), (True, False))
    m = lax.fori_loop(1, half, functools.partial(body, first_visit=(True, True)), m)
    lax.fori_loop(half, n_blocks, functools.partial(body, first_visit=(False, False)), m)


def _ml_scan(q, kt, v, gc, gr, n_batch, seq):
    assert MLSTM_CHUNK == CTX_LEN
    dh = MLSTM_DH
    blk = pl.BlockSpec((seq, dh), lambda b, h: (b, h))
    return pl.pallas_call(
        functools.partial(_ml_scan_kernel, chunk=MLSTM_CHUNK, seq=seq),
        grid=(n_batch, MLSTM_HEADS),
        in_specs=[blk, pl.BlockSpec((dh, seq), lambda b, h: (h, b)), blk,
                  pl.BlockSpec((None, None, seq, 4), lambda b, h: (b, h, 0, 0)),
                  pl.BlockSpec((None, None, 4, seq), lambda b, h: (b, h, 0, 0))],
        out_specs=blk,
        out_shape=jax.ShapeDtypeStruct((n_batch * seq, MLSTM_INNER), BF16),
        scratch_shapes=[pltpu.VMEM((dh, dh + LANES), F32), pltpu.VMEM((dh, dh + LANES), F32)],
        compiler_params=_params(2),
        name="ml_scan",
    )(q, kt, v, gc, gr)


def _rope_tables(t_lat):
    n_f = RET_DK // 4
    pos = jnp.arange(t_lat)
    inv = ROPE_BASE ** (-jnp.arange(n_f, dtype=F32) / n_f)
    ang = jnp.concatenate([(pos // GRID_W).astype(F32)[:, None] * inv,
                           (pos % GRID_W).astype(F32)[:, None] * inv], axis=-1)
    cos = jnp.concatenate([jnp.ones((CTX_LEN, 2 * n_f), F32), jnp.cos(ang)], axis=0)
    sin = jnp.concatenate([jnp.zeros((CTX_LEN, 2 * n_f), F32), jnp.sin(ang)], axis=0)
    return cos, sin


def _ret_weights(w_in):
    qk_w = RET_HEADS * RET_DK
    perm = jnp.concatenate([jnp.arange(0, RET_DK, 2), jnp.arange(1, RET_DK, 2)])
    head_cols = (jnp.arange(RET_HEADS)[:, None] * RET_DK + perm[None, :]).reshape(-1)
    w_qvg = jnp.concatenate([w_in[:, head_cols], w_in[:, 2 * qk_w:]], axis=1).astype(BF16)
    wkt = w_in[:, qk_w + head_cols].T.astype(BF16)
    return w_qvg, wkt


def _ml_gate_weights(w_q, w_k, w_v, w_if_f, w_if_b, b_if_f, b_if_b):
    n_g = 4 * MLSTM_HEADS
    w_if = jnp.concatenate([w_if_f, w_if_b], axis=-1)
    fold = functools.partial(jnp.einsum, "hde,heg->hdg", precision=lax.Precision.HIGHEST)
    gates_u = fold(w_q, w_if[0]) + MLSTM_DH ** -0.5 * fold(w_k, w_if[1])
    gates_a = fold(w_v, w_if[2])
    pad = ((0, 0), (0, 0), (0, LANES - n_g))
    wq_aug = jnp.concatenate([w_q, jnp.pad(gates_u, pad)], axis=-1).astype(BF16)
    wv_aug = jnp.concatenate([w_v, jnp.pad(gates_a, pad)], axis=-1).astype(BF16)
    bif = jnp.pad(jnp.concatenate([b_if_f, b_if_b]), (0, LANES - n_g)).reshape(1, LANES)
    return wq_aug, wv_aug, bif


def kernel(x, c, ctx, c_ctx, ada_w, ada_b, norm1_g, norm2_g, ffn_w1, ffn_w2, final_g, ret_w_in, ret_log_decay_f, ret_log_decay_b, ret_norm_g, ret_w_out, gla_w_in, gla_wa1_f, gla_wa2_f, gla_ba_f, gla_wa1_b, gla_wa2_b, gla_ba_b, gla_norm_g, gla_w_out, ml_w_in, ml_conv_w, ml_conv_b, ml_w_q, ml_w_k, ml_w_v, ml_w_if_f, ml_b_if_f, ml_w_if_b, ml_b_if_b, ml_skip, ml_w_out):
    n_batch, t_lat, _ = x.shape
    assert ctx.shape[1] == CTX_LEN == TM and t_lat % TM == 0
    seq = CTX_LEN + t_lat
    n_t = seq // TM

    stream = (x.reshape(n_batch * t_lat, D_MODEL), ctx.reshape(n_batch * CTX_LEN, D_MODEL))

    n_rows = n_batch + 1
    pad = -n_rows % 8
    cvec = jnp.concatenate([c, c_ctx[None, :], jnp.zeros((pad, D_MODEL), F32)], axis=0)
    mod = _ada(cvec, ada_w, ada_b)[:, :n_rows]
    mod = mod.reshape(DEPTH, n_rows, 6, 1, D_MODEL).transpose(0, 2, 1, 3, 4)

    cos, sin = _rope_tables(t_lat)
    final_g2 = final_g.reshape(1, D_MODEL)

    for i in range(DEPTH):
        last = i == DEPTH - 1
        off = 1 if last else 0
        kind, j = i % 3, i // 3
        sh1, sc1, g1, sh2, sc2, g2 = (mod[i, p] for p in range(6))
        gain1 = norm1_g[i].reshape(1, D_MODEL)
        gain2 = norm2_g[i].reshape(1, D_MODEL)
        if kind == 0:
            q, kt, v, g = _ret_proj(stream, n_batch, n_t, gain1, sh1, sc1, cos, sin, *_ret_weights(ret_w_in[j]))
            decay = jnp.stack([ret_log_decay_f[j], ret_log_decay_b[j]])
            y = _ret_scan(decay, q, kt, v, n_batch, seq)
            mix = functools.partial(_gated_mix, heads=RET_HEADS, dv=RET_DV)
            mix_rows, mix_consts, w_out = [y, g], [ret_norm_g[j]], ret_w_out[j]
        elif kind == 1:
            kw = GLA_HEADS * GLA_DK
            wa1 = jnp.zeros((D_MODEL, LANES), F32)
            wa1 = wa1.at[:, :GLA_RANK].set(gla_wa1_f[j]).at[:, GLA_RANK:2 * GLA_RANK].set(gla_wa1_b[j])
            wa2 = jnp.zeros((LANES, 2 * kw), F32)
            wa2 = wa2.at[:GLA_RANK, :kw].set(gla_wa2_f[j]).at[GLA_RANK:2 * GLA_RANK, kw:].set(gla_wa2_b[j])
            ba = jnp.concatenate([gla_ba_f[j], gla_ba_b[j]]).reshape(1, 2 * kw)
            q, k, v, r, ga = _gla_proj(h, n_batch, n_t, gain1, sh1, sc1, gla_w_in[j].astype(BF16),
                                       wa1.astype(BF16), wa2.astype(BF16), ba)
            y = _gla_scan(q, k, v, ga, n_batch, seq)
            mix = functools.partial(_gated_mix, heads=GLA_HEADS, dv=GLA_DV)
            mix_rows, mix_consts, w_out = [y, r], [gla_norm_g[j]], gla_w_out[j]
        else:
            a, o_pre = _ml_proj(h, n_batch, n_t, gain1, sh1, sc1, ml_w_in[j].astype(BF16))
            wq_aug, wv_aug, bif = _ml_gate_weights(ml_w_q[j], ml_w_k[j], ml_w_v[j], ml_w_if_f[j], ml_w_if_b[j],
                                                   ml_b_if_f[j], ml_b_if_b[j])
            ac, q, kt, v, gc, gr = _ml_qkv(a, n_batch, n_t, ml_conv_w[j], ml_conv_b[j].reshape(1, MLSTM_INNER),
                                           wq_aug, ml_w_k[j].transpose(0, 2, 1).astype(BF16), wv_aug, bif)
            gc = gc.reshape(n_batch, seq, 4, MLSTM_HEADS).transpose(0, 3, 1, 2)
            gr = gr.reshape(4, MLSTM_HEADS, n_batch, seq).transpose(2, 1, 0, 3)
            hs = _ml_scan(q, kt, v, gc, gr, n_batch, seq)
            mix = _ml_mix
            mix_rows, mix_consts, w_out = [hs, o_pre, ac], [ml_skip[j].reshape(1, MLSTM_INNER)], ml_w_out[j]
        h = _mix_ffn(mix, mix_rows, mix_consts, w_out.astype(BF16), stream, n_batch, n_t, off, g1, gain2, sh2, sc2,
                     g2, final_g2, ffn_w1[i].astype(BF16), ffn_w2[i].astype(BF16), final_norm=last)
        stream = (h,)
    return h.reshape(n_batch, t_lat, D_MODEL)
```

```python
import functools
import itertools

import jax
import jax.numpy as jnp
from jax import lax
from jax.experimental import pallas as pl
from jax.experimental.pallas import tpu as pltpu

F32 = jnp.float32
BF16 = jnp.bfloat16

D_MODEL = 1024
DEPTH = 4
GRID_W = 64
CTX_LEN = 256
NORM_EPS = 1e-6
D_FF = 4 * D_MODEL
RET_HEADS = 4
RET_DK = D_MODEL // RET_HEADS
RET_DV = 2 * D_MODEL // RET_HEADS
ROPE_BASE = 10000.0
GLA_HEADS = 4
GLA_DK = D_MODEL // 2 // GLA_HEADS
GLA_DV = D_MODEL // GLA_HEADS
GLA_RANK = 16
GLA_TAU = 16.0
MLSTM_INNER = 2 * D_MODEL
MLSTM_HEADS = 4
MLSTM_DH = MLSTM_INNER // MLSTM_HEADS

TM = CTX_LEN
LANES = 128
BF16_ROWS = 16
RET_CHUNK = 256
GLA_CHUNK = 64
MLSTM_CHUNK = 256
VMEM_LIMIT = 56 * 1024 * 1024

_NT = (((1,), (1,)), ((), ()))
_TN = (((0,), (0,)), ((), ()))


def _dot(a, b):
    return jnp.dot(a, b, preferred_element_type=F32)


def _silu(x):
    return x * jax.nn.sigmoid(x)


def _log_sigmoid(z):
    return jnp.minimum(z, 0.0) - jnp.log1p(jnp.exp(-jnp.abs(z)))


def _modnorm(x, g, shift, scale):
    y = x * lax.rsqrt(jnp.mean(x * x, axis=-1, keepdims=True) + NORM_EPS) * g
    return y * (1.0 + scale) + shift


def _params(n_axes):
    return pltpu.CompilerParams(dimension_semantics=("arbitrary",) * n_axes,
                                vmem_limit_bytes=VMEM_LIMIT)


def _const_spec(shape):
    nd = len(shape)
    return pl.BlockSpec(shape, lambda *_: (0,) * nd, pipeline_mode=pl.Buffered(1))


def _row_spec(width, n_tiles, off):
    return pl.BlockSpec((TM, width), lambda b, t: (b * n_tiles + t + off, 0))


def _stream_specs(stream, n_t, off):
    if len(stream) == 1:
        return [_row_spec(D_MODEL, n_t, off)]
    assert off == 0
    return [pl.BlockSpec((TM, D_MODEL), lambda b, t: (b * (n_t - 1) + jnp.maximum(t - 1, 0), 0)),
            pl.BlockSpec((TM, D_MODEL), lambda b, t: (b, 0))]


def _stream_tile(refs):
    if len(refs) == 1:
        return refs[0][...]
    lat_ref, ctx_ref = refs
    return jnp.where(pl.program_id(1) == 0, ctx_ref[...], lat_ref[...])


def _mod_spec(n_batch, off, has_ctx):
    if has_ctx:
        return pl.BlockSpec((1, 1, D_MODEL), lambda b, t: (jnp.where(t + off == 0, n_batch, b), 0, 0))
    return pl.BlockSpec((1, 1, D_MODEL), lambda b, t: (b, 0, 0))


def _ada_kernel(c_ref, w_ref, b_ref, o_ref):
    s = _silu(c_ref[...]).astype(BF16)
    o_ref[0] = _dot(s, w_ref[0].astype(BF16)) + b_ref[0]


def _ada(cvec, ada_w, ada_b):
    rows = cvec.shape[0]
    tn = 512
    return pl.pallas_call(
        _ada_kernel,
        grid=(DEPTH, 6 * D_MODEL // tn),
        in_specs=[pl.BlockSpec((rows, D_MODEL), lambda l, j: (0, 0)),
                  pl.BlockSpec((1, D_MODEL, tn), lambda l, j: (l, 0, j)),
                  pl.BlockSpec((1, 1, tn), lambda l, j: (l, 0, j))],
        out_specs=pl.BlockSpec((1, rows, tn), lambda l, j: (l, 0, j)),
        out_shape=jax.ShapeDtypeStruct((DEPTH, rows, 6 * D_MODEL), F32),
        compiler_params=_params(2),
        name="adaln",
    )(cvec, ada_w, ada_b.reshape(DEPTH, 1, 6 * D_MODEL))


def _gated_mix(y_ref, g_ref, ng_ref, w_ref, *, heads, dv):
    acc = jnp.zeros((TM, D_MODEL), F32)
    for hd in range(heads):
        lo = hd * dv
        y = y_ref[:, lo:lo + dv]
        yn = y * lax.rsqrt(jnp.mean(y * y, axis=-1, keepdims=True) + NORM_EPS) * ng_ref[hd:hd + 1, :]
        z = (_silu(g_ref[:, lo:lo + dv].astype(F32)) * yn).astype(BF16)
        acc = acc + _dot(z, w_ref[lo:lo + dv, :])
    return acc


def _ml_mix(hs_ref, op_ref, ac_ref, skip_ref, w_ref):
    cw = 1024
    acc = jnp.zeros((TM, D_MODEL), F32)
    for j in range(MLSTM_INNER // cw):
        sl = slice(j * cw, (j + 1) * cw)
        y = (jax.nn.sigmoid(op_ref[:, sl].astype(F32)) * hs_ref[:, sl].astype(F32)
             + skip_ref[:, sl] * ac_ref[:, sl].astype(F32))
        acc = acc + _dot(y.astype(BF16), w_ref[sl, :])
    return acc


def _mix_ffn_kernel(*refs, n_mix, n_stream, mix, final_norm):
    mix_refs, wo_ref, stream_refs = refs[:n_mix], refs[n_mix], refs[n_mix + 1:n_mix + 1 + n_stream]
    (gate1_ref, g_ref, sh_ref, sc_ref, gate2_ref, fg_ref, w1_ref, w2_ref, o_ref) = refs[n_mix + 1 + n_stream:]
    x = _stream_tile(stream_refs) + gate1_ref[0] * mix(*mix_refs, wo_ref)
    xn = _modnorm(x, g_ref[...], sh_ref[0], sc_ref[0]).astype(BF16)
    fc = 1024
    acc = jnp.zeros((TM, D_MODEL), F32)
    for j in range(D_FF // fc):
        hid = _dot(xn, w1_ref[:, j * fc:(j + 1) * fc])
        hid = jnp.square(jnp.maximum(hid, 0.0)).astype(BF16)
        acc = acc + _dot(hid, w2_ref[j * fc:(j + 1) * fc, :])
    o = x + gate2_ref[0] * acc
    if final_norm:
        o = o * lax.rsqrt(jnp.mean(o * o, axis=-1, keepdims=True) + NORM_EPS) * fg_ref[...]
    o_ref[...] = o


def _mix_ffn(mix, mix_rows, mix_consts, w_out, stream, n_batch, n_t, off, gate1, gain, shift, scale, gate2,
             final_g, w1, w2, *, final_norm):
    n_out = n_t - off
    ms = _mod_spec(n_batch, off, off == 0)
    in_specs = ([_row_spec(a.shape[1], n_t, off) for a in mix_rows] + [_const_spec(a.shape) for a in mix_consts]
                + [_const_spec(w_out.shape)] + _stream_specs(stream, n_t, off)
                + [ms, _const_spec((1, D_MODEL)), ms, ms, ms,
                   _const_spec((1, D_MODEL)), _const_spec((D_MODEL, D_FF)), _const_spec((D_FF, D_MODEL))])
    return pl.pallas_call(
        functools.partial(_mix_ffn_kernel, n_mix=len(mix_rows) + len(mix_consts), n_stream=len(stream), mix=mix,
                          final_norm=final_norm),
        grid=(n_batch, n_out),
        in_specs=in_specs,
        out_specs=_row_spec(D_MODEL, n_out, 0),
        out_shape=jax.ShapeDtypeStruct((n_batch * n_out * TM, D_MODEL), F32),
        compiler_params=_params(2),
        name="mix_ffn",
    )(*mix_rows, *mix_consts, w_out, *stream, gate1, gain, shift, scale, gate2, final_g, w1, w2)


def _ret_proj_kernel(*refs, n_stream):
    (g_ref, sh_ref, sc_ref, cos_ref, sin_ref, cost_ref, sint_ref, w_ref, wkt_ref,
     q_ref, kt_ref, v_ref, gg_ref) = refs[n_stream:]
    xn = _modnorm(_stream_tile(refs[:n_stream]), g_ref[...], sh_ref[0], sc_ref[0]).astype(BF16)
    cos, sin = cos_ref[...], sin_ref[...]
    cos_t, sin_t = cost_ref[...], sint_ref[...]
    half = RET_DK // 2
    qk_w, v_w = RET_HEADS * RET_DK, RET_HEADS * RET_DV
    for hd in range(RET_HEADS):
        lo = hd * RET_DK
        u = _dot(xn, w_ref[:, lo:lo + RET_DK])
        u1, u2 = u[:, :half], u[:, half:]
        q_ref[:, lo:lo + RET_DK] = jnp.concatenate([u1 * cos - u2 * sin, u1 * sin + u2 * cos], axis=1).astype(BF16)
        ut = lax.dot_general(wkt_ref[lo:lo + RET_DK, :], xn, _NT, preferred_element_type=F32) * RET_DK ** -0.5
        u1, u2 = ut[:half], ut[half:]
        kt_ref[lo:lo + RET_DK, :] = jnp.concatenate([u1 * cos_t - u2 * sin_t, u1 * sin_t + u2 * cos_t],
                                                    axis=0).astype(BF16)
    for hd in range(RET_HEADS):
        lo = hd * RET_DV
        v_ref[:, lo:lo + RET_DV] = _dot(xn, w_ref[:, qk_w + lo:qk_w + lo + RET_DV]).astype(BF16)
        gg_ref[:, lo:lo + RET_DV] = _dot(xn, w_ref[:, qk_w + v_w + lo:qk_w + v_w + lo + RET_DV]).astype(BF16)


def _ret_proj(stream, n_batch, n_t, gain, shift, scale, cos, sin, w_qvg, wkt):
    ms = _mod_spec(n_batch, 0, True)
    rows = n_batch * n_t * TM
    qk_w, v_w = RET_HEADS * RET_DK, RET_HEADS * RET_DV
    tab = pl.BlockSpec((TM, RET_DK // 2), lambda b, t: (t, 0))
    tab_t = pl.BlockSpec((RET_DK // 2, TM), lambda b, t: (0, t))
    return pl.pallas_call(
        functools.partial(_ret_proj_kernel, n_stream=len(stream)),
        grid=(n_batch, n_t),
        in_specs=_stream_specs(stream, n_t, 0) + [_const_spec((1, D_MODEL)), ms, ms, tab, tab, tab_t, tab_t,
                                                  _const_spec(w_qvg.shape), _const_spec(wkt.shape)],
        out_specs=[_row_spec(qk_w, n_t, 0), pl.BlockSpec((qk_w, TM), lambda b, t: (0, b * n_t + t)),
                   _row_spec(v_w, n_t, 0), _row_spec(v_w, n_t, 0)],
        out_shape=[jax.ShapeDtypeStruct((rows, qk_w), BF16), jax.ShapeDtypeStruct((qk_w, rows), BF16),
                   jax.ShapeDtypeStruct((rows, v_w), BF16), jax.ShapeDtypeStruct((rows, v_w), BF16)],
        compiler_params=_params(2),
        name="ret_proj",
    )(*stream, gain, shift, scale, cos, sin, cos.T, sin.T, w_qvg, wkt)


def _block_starts(step, n_blocks):
    return step, jnp.where(step == 0, 0, n_blocks - step)


def _walk_blocks(run, n_blocks, per_dir=2):
    n_lat = n_blocks - 1
    assert n_lat % (2 * per_dir) == 0
    run([0], [0], (True, False))

    def body(j, carry, first_visit):
        run([1 + per_dir * j + i for i in range(per_dir)],
            [n_blocks - 1 - per_dir * j - i for i in range(per_dir)], first_visit)
        return carry

    n_iter = n_lat // per_dir
    lax.fori_loop(0, n_iter // 2, functools.partial(body, first_visit=(True, True)), 0)
    lax.fori_loop(n_iter // 2, n_iter, functools.partial(body, first_visit=(False, False)), 0)


def _ret_scan_kernel(ld_ref, q_ref, kt_ref, v_ref, o_ref, sf_ref, sb_ref, dmf_ref, dmb_ref, *, chunk, seq):
    L = chunk
    hd = pl.program_id(1)
    n_blocks = seq // L
    li = lax.broadcasted_iota(jnp.int32, (L, L), 0)
    si = lax.broadcasted_iota(jnp.int32, (L, L), 1)
    row = lax.broadcasted_iota(jnp.int32, (L, 1), 0).astype(F32)
    col = lax.broadcasted_iota(jnp.int32, (1, L), 1).astype(F32)
    consts = []
    for direction, dm_ref in ((0, dmf_ref), (1, dmb_ref)):
        lg = -jnp.exp(jnp.full((1, 1), ld_ref[direction, hd], F32))
        if direction == 0:
            dist = li - si
            q_dec = jnp.exp((row + 1.0) * lg)
            k_dec = jnp.exp((L - 1.0 - col) * lg)
        else:
            dist = si - li
            q_dec = jnp.exp((L - row) * lg)
            k_dec = jnp.exp(col * lg)
        dm_ref[...] = jnp.where(dist >= 0, jnp.exp(jnp.maximum(dist, 0).astype(F32) * lg), 0.0)
        consts.append((q_dec, k_dec, jnp.exp(float(L) * lg)))
    sf_ref[...] = jnp.zeros_like(sf_ref)
    sb_ref[...] = jnp.zeros_like(sb_ref)

    def chunk_step(direction, r0, key, prev_key, pub):
        s_ref, dm_ref = (sf_ref, dmf_ref) if direction == 0 else (sb_ref, dmb_ref)
        q_dec, k_dec, s_dec = consts[direction]
        qc = q_ref[pl.ds(r0, L), :]
        ktc = kt_ref[:, pl.ds(r0, L)]
        vc = v_ref[pl.ds(r0, L), :]
        att = _dot(qc, ktc) * dm_ref[...]
        yield
        update = _dot((ktc.astype(F32) * k_dec).astype(BF16), vc)
        yield
        state = s_ref[...] if prev_key is None else pub[prev_key]["state"]
        pub[key] = {"state": s_dec * state + update}
        carried = _dot(qc, state.astype(BF16))
        yield
        pub[key]["out"] = _dot(att.astype(BF16), vc) + q_dec * carried

    def run(blocks_f, blocks_b, first_visit):
        pub, gens, starts = {}, [], {}
        for n in range(len(blocks_f)):
            for direction, blocks in ((0, blocks_f), (1, blocks_b)):
                starts[direction, n] = pl.multiple_of(blocks[n] * L, L)
                gens.append(chunk_step(direction, starts[direction, n], (direction, n),
                                       (direction, n - 1) if n else None, pub))
        for _ in itertools.zip_longest(*gens):
            pass
        last = len(blocks_f) - 1
        sf_ref[...] = pub[0, last]["state"]
        sb_ref[...] = pub[1, last]["state"]
        for key in sorted(pub, key=lambda kn: (kn[1], kn[0])):
            if first_visit[key[0]]:
                o_ref[pl.ds(starts[key], L), :] = pub[key]["out"]
            else:
                o_ref[pl.ds(starts[key], L), :] += pub[key]["out"]

    _walk_blocks(run, n_blocks, per_dir=8)


def _ret_scan(log_decay, q, kt, v, n_batch, seq):
    assert RET_CHUNK == CTX_LEN
    return pl.pallas_call(
        functools.partial(_ret_scan_kernel, chunk=RET_CHUNK, seq=seq),
        grid=(n_batch, RET_HEADS),
        in_specs=[pl.BlockSpec(memory_space=pltpu.SMEM),
                  pl.BlockSpec((seq, RET_DK), lambda b, h: (b, h)),
                  pl.BlockSpec((RET_DK, seq), lambda b, h: (h, b)),
                  pl.BlockSpec((seq, RET_DV), lambda b, h: (b, h))],
        out_specs=pl.BlockSpec((seq, RET_DV), lambda b, h: (b, h)),
        out_shape=jax.ShapeDtypeStruct((n_batch * seq, RET_HEADS * RET_DV), F32),
        scratch_shapes=[pltpu.VMEM((RET_DK, RET_DV), F32), pltpu.VMEM((RET_DK, RET_DV), F32),
                        pltpu.VMEM((RET_CHUNK, RET_CHUNK), F32), pltpu.VMEM((RET_CHUNK, RET_CHUNK), F32)],
        compiler_params=_params(2),
        name="ret_scan",
    )(log_decay, q, kt, v)


def _gla_proj_kernel(x_ref, g_ref, sh_ref, sc_ref, w_ref, wa1_ref, wa2_ref, ba_ref,
                     q_ref, k_ref, v_ref, r_ref, ga_ref):
    xn = _modnorm(x_ref[...], g_ref[...], sh_ref[0], sc_ref[0]).astype(BF16)
    kw, vw = GLA_HEADS * GLA_DK, GLA_HEADS * GLA_DV
    q_ref[...] = (_dot(xn, w_ref[:, :kw]) * GLA_DK ** -0.5).astype(BF16)
    k_ref[...] = _dot(xn, w_ref[:, kw:2 * kw]).astype(BF16)
    v_ref[...] = _dot(xn, w_ref[:, 2 * kw:2 * kw + vw]).astype(BF16)
    r_ref[...] = _dot(xn, w_ref[:, 2 * kw + vw:]).astype(BF16)
    low = _dot(xn, wa1_ref[...]).astype(BF16)
    z = _dot(low, wa2_ref[...]) + ba_ref[...]
    ga_ref[...] = _log_sigmoid(z) / GLA_TAU


def _gla_proj(h, n_batch, n_t, gain, shift, scale, w_in, wa1, wa2, ba):
    ms = _mod_spec(n_batch, 0, True)
    rows = h.shape[0]
    kw, vw = GLA_HEADS * GLA_DK, GLA_HEADS * GLA_DV
    return pl.pallas_call(
        _gla_proj_kernel,
        grid=(n_batch, n_t),
        in_specs=[_row_spec(D_MODEL, n_t, 0), _const_spec((1, D_MODEL)), ms, ms, _const_spec(w_in.shape),
                  _const_spec(wa1.shape), _const_spec(wa2.shape), _const_spec(ba.shape)],
        out_specs=[_row_spec(kw, n_t, 0), _row_spec(kw, n_t, 0), _row_spec(vw, n_t, 0), _row_spec(vw, n_t, 0),
                   _row_spec(2 * kw, n_t, 0)],
        out_shape=[jax.ShapeDtypeStruct((rows, kw), BF16), jax.ShapeDtypeStruct((rows, kw), BF16),
                   jax.ShapeDtypeStruct((rows, vw), BF16), jax.ShapeDtypeStruct((rows, vw), BF16),
                   jax.ShapeDtypeStruct((rows, 2 * kw), F32)],
        compiler_params=_params(2),
        name="gla_proj",
    )(h, gain, shift, scale, w_in, wa1, wa2, ba)


def _gla_scan_kernel(q_ref, k_ref, v_ref, gf_ref, gb_ref, o_ref, sf_ref, sb_ref, trif_ref, trib_ref,
                     *, sub, block, seq):
    C, blk = sub, block
    n_sub, n_blocks = blk // C, seq // blk
    li = lax.broadcasted_iota(jnp.int32, (blk, blk), 0)
    si = lax.broadcasted_iota(jnp.int32, (blk, blk), 1)
    trif_ref[...] = (li >= si).astype(BF16)
    trib_ref[...] = (si >= li).astype(BF16)
    dl = lax.broadcasted_iota(jnp.int32, (C, C), 0)
    ds = lax.broadcasted_iota(jnp.int32, (C, C), 1)
    diag_masks = (dl >= ds, ds >= dl)
    eye = (lax.broadcasted_iota(jnp.int32, (GLA_DK, GLA_DK), 0)
           == lax.broadcasted_iota(jnp.int32, (GLA_DK, GLA_DK), 1))
    sf_ref[...] = jnp.zeros_like(sf_ref)
    sb_ref[...] = jnp.zeros_like(sb_ref)

    def block_step(direction, r0, key, prev_key, pub):
        fwd = direction == 0
        gate_ref, tri_ref, s_ref = (gf_ref, trif_ref, sf_ref) if fwd else (gb_ref, trib_ref, sb_ref)
        g = gate_ref[pl.ds(r0, blk), :]
        g_hi = g.astype(BF16)
        rem = g - g_hi.astype(F32)
        g_mid = rem.astype(BF16)
        g_lo = (rem - g_mid.astype(F32)).astype(BF16)
        parts = _dot(tri_ref[...], jnp.concatenate([g_hi, g_mid, g_lo], axis=1))
        yield
        cum = parts[:, :GLA_DK] + parts[:, GLA_DK:2 * GLA_DK] + parts[:, 2 * GLA_DK:]
        qf = q_ref[pl.ds(r0, blk), :].astype(F32)
        kf = k_ref[pl.ds(r0, blk), :].astype(F32)
        vb = v_ref[pl.ds(r0, blk), :]
        end_row = blk - 1 if fwd else 0
        c_end = cum[end_row:end_row + 1]
        state = s_ref[...] if prev_key is None else pub[prev_key]["state"]
        q_carry = (qf * jnp.exp(cum)).astype(BF16)
        kd = (kf * jnp.exp(c_end - cum)).astype(BF16)
        e_col = jnp.sum(jnp.where(eye, jnp.exp(c_end), 0.0), axis=1, keepdims=True)
        pub[key] = {"state": e_col * state + lax.dot_general(kd, vb, _TN, preferred_element_type=F32)}
        att_rows = {}
        for i in range(n_sub):
            yield
            lo = i * C if fwd else blk - (i + 1) * C
            rows = slice(lo, lo + C)
            mid = lo + (C // 2 if fwd else C - 1 - C // 2)
            cq = cum[rows]
            c_mid = cum[mid:mid + 1]
            qd = (qf[rows] * jnp.exp(cq - c_mid)).astype(BF16)
            kg = (kf[rows] * jnp.exp(c_mid - cq)).astype(BF16)
            diag = jnp.where(diag_masks[direction], lax.dot_general(qd, kg, _NT, preferred_element_type=F32), 0.0)
            earlier = None
            if i > 0:
                prev = slice(0, lo) if fwd else slice(lo + C, blk)
                edge = lo - 1 if fwd else lo + C
                c_edge = cum[edge:edge + 1]
                ql = (qf[rows] * jnp.exp(cq - c_edge)).astype(BF16)
                kk = (kf[prev] * jnp.exp(c_edge - cum[prev])).astype(BF16)
                earlier = lax.dot_general(ql, kk, _NT, preferred_element_type=F32)
            unseen = jnp.zeros((C, blk - (i + 1) * C), F32) if i < n_sub - 1 else None
            cols = [earlier, diag, unseen] if fwd else [unseen, diag, earlier]
            att_rows[lo] = jnp.concatenate([c for c in cols if c is not None], axis=1).astype(BF16)
        yield
        att = jnp.concatenate([att_rows[lo] for lo in sorted(att_rows)], axis=0)
        out = _dot(jnp.concatenate([att, q_carry], axis=1), jnp.concatenate([vb, state.astype(BF16)], axis=0))
        pub[key]["out"] = out

    def run(blocks_f, blocks_b, first_visit):
        pub, gens, starts = {}, [], {}
        for n in range(len(blocks_f)):
            for direction, blocks in ((0, blocks_f), (1, blocks_b)):
                starts[direction, n] = pl.multiple_of(blocks[n] * blk, blk)
                gens.append(block_step(direction, starts[direction, n], (direction, n),
                                       (direction, n - 1) if n else None, pub))
        for _ in itertools.zip_longest(*gens):
            pass
        last = len(blocks_f) - 1
        sf_ref[...] = pub[0, last]["state"]
        sb_ref[...] = pub[1, last]["state"]
        for key in sorted(pub, key=lambda kn: (kn[1], kn[0])):
            if first_visit[key[0]]:
                o_ref[pl.ds(starts[key], blk), :] = pub[key]["out"]
            else:
                o_ref[pl.ds(starts[key], blk), :] += pub[key]["out"]

    _walk_blocks(run, n_blocks, per_dir=8)


def _gla_scan(q, k, v, ga, n_batch, seq):
    return pl.pallas_call(
        functools.partial(_gla_scan_kernel, sub=GLA_CHUNK, block=CTX_LEN, seq=seq),
        grid=(n_batch, GLA_HEADS),
        in_specs=[pl.BlockSpec((seq, GLA_DK), lambda b, h: (b, h)),
                  pl.BlockSpec((seq, GLA_DK), lambda b, h: (b, h)),
                  pl.BlockSpec((seq, GLA_DV), lambda b, h: (b, h)),
                  pl.BlockSpec((seq, GLA_DK), lambda b, h: (b, h)),
                  pl.BlockSpec((seq, GLA_DK), lambda b, h: (b, GLA_HEADS + h))],
        out_specs=pl.BlockSpec((seq, GLA_DV), lambda b, h: (b, h)),
        out_shape=jax.ShapeDtypeStruct((n_batch * seq, GLA_HEADS * GLA_DV), F32),
        scratch_shapes=[pltpu.VMEM((GLA_DK, GLA_DV), F32), pltpu.VMEM((GLA_DK, GLA_DV), F32),
                        pltpu.VMEM((CTX_LEN, CTX_LEN), BF16), pltpu.VMEM((CTX_LEN, CTX_LEN), BF16)],
        compiler_params=_params(2),
        name="gla_scan",
    )(q, k, v, ga, ga)


def _ml_proj_kernel(x_ref, g_ref, sh_ref, sc_ref, w_ref, a_ref, o_ref):
    xn = _modnorm(x_ref[...], g_ref[...], sh_ref[0], sc_ref[0]).astype(BF16)
    cw = 1024
    for j in range(MLSTM_INNER // cw):
        a_ref[:, j * cw:(j + 1) * cw] = _dot(xn, w_ref[:, j * cw:(j + 1) * cw]).astype(BF16)
        o_ref[:, j * cw:(j + 1) * cw] = _dot(
            xn, w_ref[:, MLSTM_INNER + j * cw:MLSTM_INNER + (j + 1) * cw]).astype(BF16)


def _ml_proj(h, n_batch, n_t, gain, shift, scale, w_in):
    ms = _mod_spec(n_batch, 0, True)
    rows = h.shape[0]
    return pl.pallas_call(
        _ml_proj_kernel,
        grid=(n_batch, n_t),
        in_specs=[_row_spec(D_MODEL, n_t, 0), _const_spec((1, D_MODEL)), ms, ms, _const_spec(w_in.shape)],
        out_specs=[_row_spec(MLSTM_INNER, n_t, 0), _row_spec(MLSTM_INNER, n_t, 0)],
        out_shape=[jax.ShapeDtypeStruct((rows, MLSTM_INNER), BF16)] * 2,
        compiler_params=_params(2),
        name="ml_proj",
    )(h, gain, shift, scale, w_in)


def _ml_qkv_kernel(a_ref, prev_ref, next_ref, cw_ref, cb_ref, wq_ref, wkt_ref, wv_ref, bif_ref,
                   ac_ref, q_ref, kt_ref, v_ref, gc_ref, gr_ref, *, n_t):
    t = pl.program_id(1)
    has_prev = jnp.where(t >= 2, 1.0, 0.0).astype(F32)
    has_next = jnp.where(jnp.logical_and(t >= 1, t <= n_t - 2), 1.0, 0.0).astype(F32)
    row = lax.broadcasted_iota(jnp.int32, (TM, 1), 0)
    dh = MLSTM_DH
    pre = jnp.zeros((TM, LANES), F32)
    for hd in range(MLSTM_HEADS):
        lo = hd * dh
        a_bf = a_ref[:, lo:lo + dh]
        a = a_bf.astype(F32)
        before = prev_ref[:, lo:lo + dh].astype(F32)[BF16_ROWS - 1:BF16_ROWS] * has_prev
        after = next_ref[:, lo:lo + dh].astype(F32)[0:1] * has_next
        up = jnp.where(row == 0, before, pltpu.roll(a, 1, axis=0))
        dn = jnp.where(row == TM - 1, after, pltpu.roll(a, TM - 1, axis=0))
        conv = (cw_ref[0:1, lo:lo + dh] * up + cw_ref[1:2, lo:lo + dh] * a + cw_ref[2:3, lo:lo + dh] * dn
                + cb_ref[:, lo:lo + dh])
        ac = _silu(conv).astype(BF16)
        ac_ref[:, lo:lo + dh] = ac
        q_and_gates = _dot(ac, wq_ref[hd])
        kt = (lax.dot_general(wkt_ref[hd], ac, _NT, preferred_element_type=F32) * dh ** -0.5).astype(BF16)
        v_and_gates = _dot(a_bf, wv_ref[hd])
        q_ref[:, lo:lo + dh] = q_and_gates[:, :dh].astype(BF16)
        kt_ref[lo:lo + dh, :] = kt
        v_ref[:, lo:lo + dh] = v_and_gates[:, :dh].astype(BF16)
        pre = pre + q_and_gates[:, dh:] + v_and_gates[:, dh:]
    pre = pre + bif_ref[...]
    lane = lax.broadcasted_iota(jnp.int32, (TM, LANES), 1)
    is_forget = (lane % (2 * MLSTM_HEADS)) >= MLSTM_HEADS
    gates = jnp.where(is_forget, _log_sigmoid(pre), pre)
    n_g = 4 * MLSTM_HEADS
    gc_ref[...] = gates[:, :n_g]
    gr_ref[...] = gates.T[:n_g, :]


def _ml_qkv(a, n_batch, n_t, conv_w, conv_b, wq, wkt, wv, bif):
    rows = a.shape[0]
    per_tile = TM // BF16_ROWS
    n_halo = rows // BF16_ROWS
    n_g = 4 * MLSTM_HEADS
    wide = _row_spec(MLSTM_INNER, n_t, 0)
    prev_spec = pl.BlockSpec((BF16_ROWS, MLSTM_INNER),
                             lambda b, t: (jnp.maximum((b * n_t + t) * per_tile - 1, 0), 0))
    next_spec = pl.BlockSpec((BF16_ROWS, MLSTM_INNER),
                             lambda b, t: (jnp.minimum((b * n_t + t + 1) * per_tile, n_halo - 1), 0))
    return pl.pallas_call(
        functools.partial(_ml_qkv_kernel, n_t=n_t),
        grid=(n_batch, n_t),
        in_specs=[wide, prev_spec, next_spec, _const_spec(conv_w.shape), _const_spec(conv_b.shape),
                  _const_spec(wq.shape), _const_spec(wkt.shape), _const_spec(wv.shape), _const_spec(bif.shape)],
        out_specs=[wide, wide, pl.BlockSpec((MLSTM_INNER, TM), lambda b, t: (0, b * n_t + t)), wide,
                   _row_spec(n_g, n_t, 0), pl.BlockSpec((n_g, TM), lambda b, t: (0, b * n_t + t))],
        out_shape=[jax.ShapeDtypeStruct((rows, MLSTM_INNER), BF16)] * 2
        + [jax.ShapeDtypeStruct((MLSTM_INNER, rows), BF16), jax.ShapeDtypeStruct((rows, MLSTM_INNER), BF16),
           jax.ShapeDtypeStruct((rows, n_g), F32), jax.ShapeDtypeStruct((n_g, rows), F32)],
        compiler_params=_params(2),
        name="ml_qkv",
    )(a, a, a, conv_w, conv_b, wq, wkt, wv, bif)


def _ml_scan_kernel(q_ref, kt_ref, v_ref, gc_ref, gr_ref, o_ref, ctf_ref, ctb_ref, *, chunk, seq):
    L = chunk
    dv = v_ref.shape[1]
    n_blocks = seq // L
    li = lax.broadcasted_iota(jnp.int32, (L, L), 0)
    si = lax.broadcasted_iota(jnp.int32, (L, L), 1)
    ones = jnp.ones((L, LANES), BF16)
    for ref in (ctf_ref, ctb_ref):
        ref[...] = jnp.zeros_like(ref)

    def chunk_step(direction, r0, m_prev, result, first_visit):
        ct_ref = ctf_ref if direction == 0 else ctb_ref
        col_i, col_f = (0, 1) if direction == 0 else (2, 3)
        seen = (li >= si) if direction == 0 else (si >= li)
        seen_t = (li <= si) if direction == 0 else (si <= li)
        end_row = L - 1 if direction == 0 else 0
        qc = q_ref[pl.ds(r0, L), :]
        ktc = kt_ref[:, pl.ds(r0, L)]
        vc = v_ref[pl.ds(r0, L), :]
        state = ct_ref[...]
        carried = _dot(qc, state.astype(BF16))
        yield
        qk = _dot(qc, ktc)
        yield
        f_col = gc_ref[pl.ds(r0, L), col_f:col_f + 1]
        f_row = gr_ref[col_f:col_f + 1, pl.ds(r0, L)]
        i_row = gr_ref[col_i:col_i + 1, pl.ds(r0, L)]
        b_col = jnp.sum(jnp.where(seen, f_row, 0.0), axis=1, keepdims=True)
        b_row = jnp.sum(jnp.where(seen_t, f_col, 0.0), axis=0, keepdims=True)
        b_end = b_col[end_row:end_row + 1]
        d_end = b_end - b_row + i_row
        m_new = jnp.maximum(b_end + m_prev, jnp.max(d_end, axis=1, keepdims=True))
        a_prev = jnp.exp(b_end + m_prev - m_new)
        ktw = (ktc.astype(F32) * jnp.exp(d_end - m_new)).astype(BF16)
        ct_ref[:, :dv] = a_prev * state[:, :dv] + _dot(ktw, vc)
        ct_ref[:, dv:] = a_prev * state[:, dv:] + _dot(ktw, ones)
        yield
        d_in = jnp.where(seen, b_col - b_row + i_row, -jnp.inf)
        g = b_col + m_prev
        m_t = jnp.maximum(g, jnp.max(d_in, axis=1, keepdims=True))
        w_prev = jnp.exp(g - m_t)
        s = qk * jnp.exp(d_in - m_t)
        num = _dot(s.astype(BF16), vc) + w_prev * carried[:, :dv]
        den = jnp.sum(s, axis=1, keepdims=True) + w_prev * carried[:, dv:dv + 1]
        out = num * (1.0 / jnp.maximum(jnp.abs(den), jnp.exp(-m_t)))
        if not first_visit:
            out = o_ref[pl.ds(r0, L), :].astype(F32) + out
        o_ref[pl.ds(r0, L), :] = out.astype(o_ref.dtype)
        result[direction] = m_new

    def body(step, carry, first_visit):
        blk_f, blk_b = _block_starts(step, n_blocks)
        result = {}
        for _ in itertools.zip_longest(
                chunk_step(0, pl.multiple_of(blk_f * L, L), carry[0], result, first_visit[0]),
                chunk_step(1, pl.multiple_of(blk_b * L, L), carry[1], result, first_visit[1])):
            pass
        return result[0], result[1]

    assert n_blocks % 2 == 1
    half = (n_blocks + 1) // 2
    m = body(0, (jnp.zeros((1, 1), F32), jnp.zeros((1, 1), F32)), (True, False))
    m = lax.fori_loop(1, half, functools.partial(body, first_visit=(True, True)), m)
    lax.fori_loop(half, n_blocks, functools.partial(body, first_visit=(False, False)), m)


def _ml_scan(q, kt, v, gc, gr, n_batch, seq):
    assert MLSTM_CHUNK == CTX_LEN
    dh = MLSTM_DH
    blk = pl.BlockSpec((seq, dh), lambda b, h: (b, h))
    return pl.pallas_call(
        functools.partial(_ml_scan_kernel, chunk=MLSTM_CHUNK, seq=seq),
        grid=(n_batch, MLSTM_HEADS),
        in_specs=[blk, pl.BlockSpec((dh, seq), lambda b, h: (h, b)), blk,
                  pl.BlockSpec((None, None, seq, 4), lambda b, h: (b, h, 0, 0)),
                  pl.BlockSpec((None, None, 4, seq), lambda b, h: (b, h, 0, 0))],
        out_specs=blk,
        out_shape=jax.ShapeDtypeStruct((n_batch * seq, MLSTM_INNER), BF16),
        scratch_shapes=[pltpu.VMEM((dh, dh + LANES), F32), pltpu.VMEM((dh, dh + LANES), F32)],
        compiler_params=_params(2),
        name="ml_scan",
    )(q, kt, v, gc, gr)


def _rope_tables(t_lat):
    n_f = RET_DK // 4
    pos = jnp.arange(t_lat)
    inv = ROPE_BASE ** (-jnp.arange(n_f, dtype=F32) / n_f)
    ang = jnp.concatenate([(pos // GRID_W).astype(F32)[:, None] * inv,
                           (pos % GRID_W).astype(F32)[:, None] * inv], axis=-1)
    cos = jnp.concatenate([jnp.ones((CTX_LEN, 2 * n_f), F32), jnp.cos(ang)], axis=0)
    sin = jnp.concatenate([jnp.zeros((CTX_LEN, 2 * n_f), F32), jnp.sin(ang)], axis=0)
    return cos, sin


def _ret_weights(w_in):
    qk_w = RET_HEADS * RET_DK
    perm = jnp.concatenate([jnp.arange(0, RET_DK, 2), jnp.arange(1, RET_DK, 2)])
    head_cols = (jnp.arange(RET_HEADS)[:, None] * RET_DK + perm[None, :]).reshape(-1)
    w_qvg = jnp.concatenate([w_in[:, head_cols], w_in[:, 2 * qk_w:]], axis=1).astype(BF16)
    wkt = w_in[:, qk_w + head_cols].T.astype(BF16)
    return w_qvg, wkt


def _ml_gate_weights(w_q, w_k, w_v, w_if_f, w_if_b, b_if_f, b_if_b):
    n_g = 4 * MLSTM_HEADS
    w_if = jnp.concatenate([w_if_f, w_if_b], axis=-1)
    fold = functools.partial(jnp.einsum, "hde,heg->hdg", precision=lax.Precision.HIGHEST)
    gates_u = fold(w_q, w_if[0]) + MLSTM_DH ** -0.5 * fold(w_k, w_if[1])
    gates_a = fold(w_v, w_if[2])
    pad = ((0, 0), (0, 0), (0, LANES - n_g))
    wq_aug = jnp.concatenate([w_q, jnp.pad(gates_u, pad)], axis=-1).astype(BF16)
    wv_aug = jnp.concatenate([w_v, jnp.pad(gates_a, pad)], axis=-1).astype(BF16)
    bif = jnp.pad(jnp.concatenate([b_if_f, b_if_b]), (0, LANES - n_g)).reshape(1, LANES)
    return wq_aug, wv_aug, bif


def kernel(x, c, ctx, c_ctx, ada_w, ada_b, norm1_g, norm2_g, ffn_w1, ffn_w2, final_g, ret_w_in, ret_log_decay_f, ret_log_decay_b, ret_norm_g, ret_w_out, gla_w_in, gla_wa1_f, gla_wa2_f, gla_ba_f, gla_wa1_b, gla_wa2_b, gla_ba_b, gla_norm_g, gla_w_out, ml_w_in, ml_conv_w, ml_conv_b, ml_w_q, ml_w_k, ml_w_v, ml_w_if_f, ml_b_if_f, ml_w_if_b, ml_b_if_b, ml_skip, ml_w_out):
    n_batch, t_lat, _ = x.shape
    assert ctx.shape[1] == CTX_LEN == TM and t_lat % TM == 0
    seq = CTX_LEN + t_lat
    n_t = seq // TM

    stream = (x.reshape(n_batch * t_lat, D_MODEL), ctx.reshape(n_batch * CTX_LEN, D_MODEL))

    n_rows = n_batch + 1
    pad = -n_rows % 8
    cvec = jnp.concatenate([c, c_ctx[None, :], jnp.zeros((pad, D_MODEL), F32)], axis=0)
    mod = _ada(cvec, ada_w, ada_b)[:, :n_rows]
    mod = mod.reshape(DEPTH, n_rows, 6, 1, D_MODEL).transpose(0, 2, 1, 3, 4)

    cos, sin = _rope_tables(t_lat)
    final_g2 = final_g.reshape(1, D_MODEL)

    for i in range(DEPTH):
        last = i == DEPTH - 1
        off = 1 if last else 0
        kind, j = i % 3, i // 3
        sh1, sc1, g1, sh2, sc2, g2 = (mod[i, p] for p in range(6))
        gain1 = norm1_g[i].reshape(1, D_MODEL)
        gain2 = norm2_g[i].reshape(1, D_MODEL)
        if kind == 0:
            q, kt, v, g = _ret_proj(stream, n_batch, n_t, gain1, sh1, sc1, cos, sin, *_ret_weights(ret_w_in[j]))
            decay = jnp.stack([ret_log_decay_f[j], ret_log_decay_b[j]])
            y = _ret_scan(decay, q, kt, v, n_batch, seq)
            mix = functools.partial(_gated_mix, heads=RET_HEADS, dv=RET_DV)
            mix_rows, mix_consts, w_out = [y, g], [ret_norm_g[j]], ret_w_out[j]
        elif kind == 1:
            kw = GLA_HEADS * GLA_DK
            wa1 = jnp.zeros((D_MODEL, LANES), F32)
            wa1 = wa1.at[:, :GLA_RANK].set(gla_wa1_f[j]).at[:, GLA_RANK:2 * GLA_RANK].set(gla_wa1_b[j])
            wa2 = jnp.zeros((LANES, 2 * kw), F32)
            wa2 = wa2.at[:GLA_RANK, :kw].set(gla_wa2_f[j]).at[GLA_RANK:2 * GLA_RANK, kw:].set(gla_wa2_b[j])
            ba = jnp.concatenate([gla_ba_f[j], gla_ba_b[j]]).reshape(1, 2 * kw)
            q, k, v, r, ga = _gla_proj(h, n_batch, n_t, gain1, sh1, sc1, gla_w_in[j].astype(BF16),
                                       wa1.astype(BF16), wa2.astype(BF16), ba)
            y = _gla_scan(q, k, v, ga, n_batch, seq)
            mix = functools.partial(_gated_mix, heads=GLA_HEADS, dv=GLA_DV)
            mix_rows, mix_consts, w_out = [y, r], [gla_norm_g[j]], gla_w_out[j]
        else:
            a, o_pre = _ml_proj(h, n_batch, n_t, gain1, sh1, sc1, ml_w_in[j].astype(BF16))
            wq_aug, wv_aug, bif = _ml_gate_weights(ml_w_q[j], ml_w_k[j], ml_w_v[j], ml_w_if_f[j], ml_w_if_b[j],
                                                   ml_b_if_f[j], ml_b_if_b[j])
            ac, q, kt, v, gc, gr = _ml_qkv(a, n_batch, n_t, ml_conv_w[j], ml_conv_b[j].reshape(1, MLSTM_INNER),
                                           wq_aug, ml_w_k[j].transpose(0, 2, 1).astype(BF16), wv_aug, bif)
            gc = gc.reshape(n_batch, seq, 4, MLSTM_HEADS).transpose(0, 3, 1, 2)
            gr = gr.reshape(4, MLSTM_HEADS, n_batch, seq).transpose(2, 1, 0, 3)
            hs = _ml_scan(q, kt, v, gc, gr, n_batch, seq)
            mix = _ml_mix
            mix_rows, mix_consts, w_out = [hs, o_pre, ac], [ml_skip[j].reshape(1, MLSTM_INNER)], ml_w_out[j]
        h = _mix_ffn(mix, mix_rows, mix_consts, w_out.astype(BF16), stream, n_batch, n_t, off, g1, gain2, sh2, sc2,
                     g2, final_g2, ffn_w1[i].astype(BF16), ffn_w2[i].astype(BF16), final_norm=last)
        stream = (h,)
    return h.reshape(n_batch, t_lat, D_MODEL)
```
